```python
import jax
import jax.numpy as jnp
from jax import lax
import numpy as np

D_MODEL = 1024
BATCH = 8
SEQ = 2048
DEPTH = 1
DEC_BATCH = 128
DEC_SEQ = 8
PAST_LEN = 16384
PAGE_SIZE = 128

PLE_DIM = 256
D_CONV = D_MODEL // 2
CONV_WIDTH = 31
N_DN_HEADS = 4
DN_HEAD_DIM = 128
D_DN = N_DN_HEADS * DN_HEAD_DIM
SHORT_CONV = 4
CHUNK = 64
D_MIX = D_CONV + D_DN
D_IN = 3 * D_CONV + 4 * D_DN + 2 * N_DN_HEADS
EPS = 1e-6

kernel_name = 'hymba_conformer_gdn_step'


def rms_norm(x, g):
    xf = x.astype(jnp.float32)
    y = xf * lax.rsqrt(jnp.mean(xf * xf, axis=-1, keepdims=True) + EPS)
    return (y * g.astype(jnp.float32)).astype(x.dtype)


def layer_norm(x, g, b):
    xf = x.astype(jnp.float32)
    xc = xf - jnp.mean(xf, axis=-1, keepdims=True)
    y = xc * lax.rsqrt(jnp.mean(xc * xc, axis=-1, keepdims=True) + EPS)
    return (y * g.astype(jnp.float32) + b.astype(jnp.float32)).astype(x.dtype)


def l2_normalize(x):
    xf = x.astype(jnp.float32)
    return xf * lax.rsqrt(jnp.sum(xf * xf, axis=-1, keepdims=True) + EPS)


def causal_depthwise_conv(x, buf, w):
    width, ch = w.shape
    xc = jnp.concatenate([buf.astype(x.dtype), x], axis=1)
    y = lax.conv_general_dilated(xc, w[:, None, :].astype(x.dtype), window_strides=(1,),
                                 padding='VALID', dimension_numbers=('NWC', 'WIO', 'NWC'),
                                 feature_group_count=ch)
    return y, xc[:, xc.shape[1] - (width - 1):]


def gated_delta_rule(q, k, v, g, beta, s0):
    bsz, seqlen, nh, dk = q.shape
    dv = v.shape[-1]
    c = min(CHUNK, seqlen)
    n = -(-seqlen // c)
    pad = n * c - seqlen

    def blocks(t):
        t = t.astype(jnp.float32)
        t = jnp.pad(t, [(0, 0), (0, pad)] + [(0, 0)] * (t.ndim - 2))
        t = t.reshape((bsz, n, c) + t.shape[2:])
        return jnp.moveaxis(t, 3, 1)

    qb, kb, vb, gb, bb = blocks(q), blocks(k), blocks(v), blocks(g), blocks(beta)
    gc = jnp.cumsum(gb, axis=-1)
    incl = jnp.tril(jnp.ones((c, c), dtype=bool))
    strict = jnp.tril(jnp.ones((c, c), dtype=bool), -1)
    diff = gc[..., :, None] - gc[..., None, :]
    decay = jnp.where(incl, jnp.exp(jnp.where(incl, diff, 0.0)), 0.0)
    k_beta = kb * bb[..., None]
    a_mat = jnp.where(strict, jnp.einsum('bhnid,bhnjd->bhnij', k_beta, kb) * decay, 0.0)
    eye = jnp.broadcast_to(jnp.eye(c, dtype=jnp.float32), a_mat.shape)
    rhs = jnp.concatenate([vb * bb[..., None], k_beta * jnp.exp(gc)[..., None]], axis=-1)
    sol = lax.linalg.triangular_solve(eye + a_mat, rhs, left_side=True, lower=True,
                                      unit_diagonal=True)
    u, w = sol[..., :dv], sol[..., dv:]
    qk = jnp.einsum('bhnid,bhnjd->bhnij', qb, kb) * decay
    q_dec = qb * jnp.exp(gc)[..., None]
    k_dec = kb * jnp.exp(gc[..., -1:] - gc)[..., None]
    g_last = jnp.exp(gc[..., -1])

    def step(s, inp):
        qd, kd, uc, wc, qkc, gl = inp
        v_new = uc - jnp.einsum('bhik,bhkv->bhiv', wc, s)
        o = jnp.einsum('bhik,bhkv->bhiv', qd, s) + jnp.einsum('bhij,bhjv->bhiv', qkc, v_new)
        s = s * gl[..., None, None] + jnp.einsum('bhik,bhiv->bhkv', kd, v_new)
        return s, o

    xs = (jnp.moveaxis(q_dec, 2, 0), jnp.moveaxis(k_dec, 2, 0), jnp.moveaxis(u, 2, 0),
          jnp.moveaxis(w, 2, 0), jnp.moveaxis(qk, 2, 0), jnp.moveaxis(g_last, 2, 0))
    s_final, o = lax.scan(step, s0.astype(jnp.float32), xs)
    o = jnp.moveaxis(o, 0, 2)
    o = jnp.moveaxis(o, 1, 3).reshape(bsz, n * c, nh, dv)[:, :seqlen]
    return o, s_final


def hybrid_layer(x, p_emb, conv_buf, dn_buf, s0, norm_mix_g, w_in, conv_dw_w, conv_dw_b,
                 conv_ln_g, conv_ln_b, conv_pw_w, dn_conv_w, dn_a_log, dn_dt_bias, dn_norm_g,
                 w_out, ple_norm_g, ple_gate_w, ple_proj_w):
    bsz, seqlen, _ = x.shape
    h = rms_norm(x, norm_mix_g)
    proj = h @ w_in
    cuts = [int(i) for i in np.cumsum([D_CONV, D_CONV, D_CONV, 3 * D_DN, D_DN, N_DN_HEADS])]
    glu_a, glu_b, c_gate, qkv, z, b_raw, a_raw = jnp.split(proj, cuts, axis=-1)

    u = glu_a * jax.nn.sigmoid(glu_b)
    c, new_conv_buf = causal_depthwise_conv(u, conv_buf, conv_dw_w)
    c = jax.nn.silu(layer_norm(c + conv_dw_b, conv_ln_g, conv_ln_b))
    c = (c @ conv_pw_w) * jax.nn.silu(c_gate)

    qkv_c, new_dn_buf = causal_depthwise_conv(qkv, dn_buf, dn_conv_w)
    qkv_c = jax.nn.silu(qkv_c)
    q, k, v = jnp.split(qkv_c, 3, axis=-1)
    heads = (bsz, seqlen, N_DN_HEADS, DN_HEAD_DIM)
    q = l2_normalize(q.reshape(heads)) * (DN_HEAD_DIM ** -0.5)
    k = l2_normalize(k.reshape(heads))
    v = v.reshape(heads)
    beta = jax.nn.sigmoid(b_raw.astype(jnp.float32))
    g = -jnp.exp(dn_a_log.astype(jnp.float32)) * jax.nn.softplus(
        a_raw.astype(jnp.float32) + dn_dt_bias.astype(jnp.float32))
    o, s_new = gated_delta_rule(q, k, v, g, beta, s0)
    o = rms_norm(o.astype(x.dtype), dn_norm_g) * jax.nn.silu(z.reshape(heads))

    mix = jnp.concatenate([c, o.reshape(bsz, seqlen, D_DN)], axis=-1) @ w_out
    x = x + mix

    gate = jax.nn.sigmoid(rms_norm(x, ple_norm_g) @ ple_gate_w)
    x = x + gate * (p_emb.astype(x.dtype) @ ple_proj_w)
    return x, new_conv_buf, new_dn_buf, s_new


def setup_inputs(seed: int = 0) -> dict:
    key = jax.random.key(seed)
    ks = jax.random.split(key, 24)
    nrm = lambda k, shape, s: jax.random.normal(k, shape, jnp.float32) * s
    dt = jnp.exp(jax.random.uniform(ks[13], (DEPTH, N_DN_HEADS), jnp.float32,
                                    np.log(1e-3), np.log(1e-1)))
    return {
        'x_prompt': nrm(ks[0], (BATCH, SEQ, D_MODEL), 1.0),
        'x_sample': nrm(ks[1], (DEC_BATCH, DEC_SEQ, D_MODEL), 1.0),
        'state_conv': nrm(ks[2], (DEPTH, DEC_BATCH, CONV_WIDTH - 1, D_CONV), 0.5),
        'state_dn_conv': nrm(ks[3], (DEPTH, DEC_BATCH, SHORT_CONV - 1, 3 * D_DN), 1.0),
        'state_dn_S': nrm(ks[4], (DEPTH, DEC_BATCH, N_DN_HEADS, DN_HEAD_DIM, DN_HEAD_DIM), 0.1),
        'p_prompt': nrm(ks[5], (DEPTH, BATCH, SEQ, PLE_DIM), 1.0),
        'p_sample': nrm(ks[6], (DEPTH, DEC_BATCH, DEC_SEQ, PLE_DIM), 1.0),
        'norm_mix_g': 1.0 + nrm(ks[7], (DEPTH, D_MODEL), 0.02),
        'w_in': nrm(ks[8], (DEPTH, D_MODEL, D_IN), D_MODEL ** -0.5),
        'conv_dw_w': nrm(ks[9], (DEPTH, CONV_WIDTH, D_CONV), CONV_WIDTH ** -0.5),
        'conv_dw_b': nrm(ks[10], (DEPTH, D_CONV), 0.02),
        'conv_ln_g': 1.0 + nrm(ks[11], (DEPTH, D_CONV), 0.02),
        'conv_ln_b': nrm(ks[12], (DEPTH, D_CONV), 0.02),
        'conv_pw_w': nrm(ks[14], (DEPTH, D_CONV, D_CONV), D_CONV ** -0.5),
        'dn_conv_w': nrm(ks[15], (DEPTH, SHORT_CONV, 3 * D_DN), SHORT_CONV ** -0.5),
        'dn_a_log': jnp.log(jax.random.uniform(ks[16], (DEPTH, N_DN_HEADS), jnp.float32, 1.0, 16.0)),
        'dn_dt_bias': dt + jnp.log(-jnp.expm1(-dt)),
        'dn_norm_g': 1.0 + nrm(ks[17], (DEPTH, DN_HEAD_DIM), 0.02),
        'w_out': nrm(ks[18], (DEPTH, D_MIX, D_MODEL), D_MIX ** -0.5),
        'ple_norm_g': 1.0 + nrm(ks[19], (DEPTH, D_MODEL), 0.02),
        'ple_gate_w': nrm(ks[20], (DEPTH, D_MODEL, D_MODEL), D_MODEL ** -0.5),
        'ple_proj_w': nrm(ks[21], (DEPTH, PLE_DIM, D_MODEL), PLE_DIM ** -0.5),
        'final_norm_g': 1.0 + nrm(ks[22], (D_MODEL,), 0.02),
    }


def reference(x_prompt, x_sample, state_conv, state_dn_conv, state_dn_S, p_prompt, p_sample,
              norm_mix_g, w_in, conv_dw_w, conv_dw_b, conv_ln_g, conv_ln_b, conv_pw_w,
              dn_conv_w, dn_a_log, dn_dt_bias, dn_norm_g, w_out, ple_norm_g, ple_gate_w,
              ple_proj_w, final_norm_g):
    xp, xs = x_prompt, x_sample
    bp = x_prompt.shape[0]
    cp, dp, sp, cs, ds, ss = [], [], [], [], [], []
    for i in range(DEPTH):
        lw = (norm_mix_g[i], w_in[i], conv_dw_w[i], conv_dw_b[i], conv_ln_g[i], conv_ln_b[i],
              conv_pw_w[i], dn_conv_w[i], dn_a_log[i], dn_dt_bias[i], dn_norm_g[i], w_out[i],
              ple_norm_g[i], ple_gate_w[i], ple_proj_w[i])
        zc = jnp.zeros((bp, CONV_WIDTH - 1, D_CONV), xp.dtype)
        zd = jnp.zeros((bp, SHORT_CONV - 1, 3 * D_DN), xp.dtype)
        zs = jnp.zeros((bp, N_DN_HEADS, DN_HEAD_DIM, DN_HEAD_DIM), jnp.float32)
        xp, c1, d1, s1 = hybrid_layer(xp, p_prompt[i], zc, zd, zs, *lw)
        xs, c2, d2, s2 = hybrid_layer(xs, p_sample[i], state_conv[i], state_dn_conv[i],
                                      state_dn_S[i], *lw)
        cp.append(c1); dp.append(d1); sp.append(s1)
        cs.append(c2); ds.append(d2); ss.append(s2)
    y_prompt = rms_norm(xp, final_norm_g)
    y_sample = rms_norm(xs, final_norm_g)
    new_conv_prompt = jnp.stack(cp)
    new_dn_conv_prompt = jnp.stack(dp)
    new_S_prompt = jnp.stack(sp)
    new_conv_sample = jnp.stack(cs)
    new_dn_conv_sample = jnp.stack(ds)
    new_S_sample = jnp.stack(ss)
    return (y_prompt, y_sample, new_conv_prompt, new_dn_conv_prompt, new_S_prompt,
            new_conv_sample, new_dn_conv_sample, new_S_sample)
```

```python
import functools

import jax
import jax.numpy as jnp
from jax import lax
from jax.experimental import pallas as pl
from jax.experimental.pallas import tpu as pltpu

EPS = 1e-6
N_HEADS = 4
HEAD_DIM = 128
SUB = 16
PROMPT_TILE = 256
PROMPT_CHUNK = 64
SAMPLE_SEQS = 16
CONV_ROWS = 32
VMEM_LIMIT_BYTES = 56 * 1024 * 1024

F32 = jnp.float32
BF16 = jnp.bfloat16


def _mm(a, b):
    return jnp.dot(a.astype(BF16), b.astype(BF16), preferred_element_type=F32)


def _mm_nt(a, b):
    return lax.dot_general(a.astype(BF16), b.astype(BF16), (((1,), (1,)), ((), ())),
                           preferred_element_type=F32)


def _mm_tn(a, b):
    return lax.dot_general(a.astype(BF16), b.astype(BF16), (((0,), (0,)), ((), ())),
                           preferred_element_type=F32)


def _mm_f32(a, b):
    return jnp.dot(a, b, preferred_element_type=F32, precision=lax.Precision.HIGHEST)


def _rms(x, g):
    return x * lax.rsqrt(jnp.mean(x * x, axis=-1, keepdims=True) + EPS) * g


def _silu(x):
    return x * jax.nn.sigmoid(x)


def _softplus(x):
    return jnp.maximum(x, 0.0) + jnp.log(1.0 + jnp.exp(-jnp.abs(x)))


def _project(x, w):
    d_conv = w["dw_w"].shape[-1]
    h = _rms(x, w["g_mix"][...]).astype(BF16)
    win = w["w_in"]
    ab = jnp.dot(h, win[:, 0:2 * d_conv], preferred_element_type=F32)
    u = ab[:, :d_conv] * jax.nn.sigmoid(ab[:, d_conv:])
    c_gate = jnp.dot(h, win[:, 2 * d_conv:3 * d_conv], preferred_element_type=F32)
    o1 = 3 * d_conv
    d_dn = N_HEADS * HEAD_DIM
    qkv = jnp.dot(h, win[:, o1:o1 + 3 * d_dn], preferred_element_type=F32)
    z = jnp.dot(h, win[:, o1 + 3 * d_dn:o1 + 4 * d_dn], preferred_element_type=F32)
    tail = jnp.dot(h, w["w_tail"][...], preferred_element_type=F32)
    lane = lax.broadcasted_iota(jnp.int32, tail.shape, 1)
    beta = jax.nn.sigmoid(tail)
    g = -jnp.exp(w["a_log"][...]) * _softplus(tail + w["dt_bias"][...])
    bg = jnp.where(lane < N_HEADS, beta, jnp.where(lane < 2 * N_HEADS, g, 0.0))
    return u, c_gate, qkv, z, bg


def _conv_branch_tail(c, c_gate, w):
    c = c + w["dw_b"][...]
    cc = c - jnp.mean(c, axis=-1, keepdims=True)
    c = cc * lax.rsqrt(jnp.mean(cc * cc, axis=-1, keepdims=True) + EPS) * w["ln_g"][...] + w["ln_b"][...]
    c = _silu(c)
    return _mm(c, w["pw"][...]) * _silu(c_gate)


def _qkv_heads(qkv_c):
    d_dn = N_HEADS * HEAD_DIM
    a = _silu(qkv_c)
    qs, ks, vs = [], [], []
    for h in range(N_HEADS):
        q = a[:, h * HEAD_DIM:(h + 1) * HEAD_DIM]
        k = a[:, d_dn + h * HEAD_DIM:d_dn + (h + 1) * HEAD_DIM]
        v = a[:, 2 * d_dn + h * HEAD_DIM:2 * d_dn + (h + 1) * HEAD_DIM]
        q = q * lax.rsqrt(jnp.sum(q * q, axis=-1, keepdims=True) + EPS) * (HEAD_DIM ** -0.5)
        k = k * lax.rsqrt(jnp.sum(k * k, axis=-1, keepdims=True) + EPS)
        qs.append(q); ks.append(k); vs.append(v)
    return qs, ks, vs


def _chunk_masks(rows, chunk, sub):
    ri = lax.broadcasted_iota(jnp.int32, (rows, rows), 0)
    ci = lax.broadcasted_iota(jnp.int32, (rows, rows), 1)
    same = (ri // chunk) == (ci // chunk)
    incl = same & (ci <= ri)
    strict = same & (ci < ri)
    diag = (ri // sub) == (ci // sub)
    eye = (ri == ci).astype(F32)
    return same, incl, strict, diag, eye


def _solve_unit_lower(a, rhs, diag, eye, chunk, sub):
    d = jnp.where(diag, a, 0.0)
    p = eye - d
    pw, span = d, 1
    while 2 * span < sub:
        pw = _mm(pw, pw)
        p = _mm(p, eye + pw)
        span *= 2
    nblk = chunk // sub
    if nblk == 1:
        return _mm(p, rhs)
    n = a - d
    b = _mm(p, n)
    y = _mm(p, rhs)
    powers, span, bp = [], 1, b
    while 2 * span < nblk:
        bp = _mm(bp, bp)
        powers.append(bp)
        span *= 2
    for bp in reversed(powers):
        y = y + _mm(bp, y)
    return y - _mm(b, y)


def _chunk_local(q, k, v, beta, gcc, grow, gl, masks, chunk, sub):
    same, incl, strict, diag, eye = masks
    decay = jnp.where(incl, jnp.exp(jnp.where(incl, gcc - grow, 0.0)), 0.0)
    kb = k * beta
    a = jnp.where(strict, _mm_nt(kb, k) * decay, 0.0)
    qk = _mm_nt(q, k) * decay
    egc = jnp.exp(gcc)
    rhs = jnp.concatenate([v * beta, kb * egc], axis=1)
    y = _solve_unit_lower(a, rhs, diag, eye, chunk, sub)
    u, wk = y[:, :HEAD_DIM], y[:, HEAD_DIM:]
    return u, wk, qk, q * egc, k * jnp.exp(gl - gcc)


def _gate_scalars(bg, masks):
    same, incl = masks[0], masks[1]
    gc = _mm_f32(incl.astype(F32), bg)
    gtot = _mm_f32(same.astype(F32), bg)
    return gc, gc.T, gtot


def _finish(x, c_out, o_heads, z, p_emb, w):
    outs = []
    for h in range(N_HEADS):
        o = o_heads[h]
        o = o * lax.rsqrt(jnp.mean(o * o, axis=-1, keepdims=True) + EPS) * w["dn_g"][...]
        outs.append(o * _silu(z[:, h * HEAD_DIM:(h + 1) * HEAD_DIM]))
    mix_in = jnp.concatenate([c_out] + outs, axis=1)
    x = x + _mm(mix_in, w["w_out"][...])
    gate = jax.nn.sigmoid(_mm(_rms(x, w["ple_g"][...]), w["ple_gate"][...]))
    x = x + gate * _mm(p_emb, w["ple_proj"][...])
    return _rms(x, w["fin_g"][...])


_WEIGHT_NAMES = ("g_mix", "w_in", "w_tail", "dw_w", "dw_b", "ln_g", "ln_b", "pw", "dn_w", "a_log",
                 "dt_bias", "dn_g", "w_out", "ple_g", "ple_gate", "ple_proj", "fin_g")


def _prompt_kernel(*refs, tile, chunk):
    x_ref, p_ref = refs[0], refs[1]
    nw = len(_WEIGHT_NAMES)
    w = dict(zip(_WEIGHT_NAMES, refs[2:2 + nw]))
    y_ref, nconv_ref, ndn_ref, ns_ref = refs[2 + nw:6 + nw]
    ubuf, qbuf, cbuf, s_scr = refs[6 + nw:]
    t = pl.program_id(1)
    kw = w["dw_w"].shape[0]
    ks = w["dn_w"].shape[0]
    hist, qhist = kw - 1, ks - 1
    upad, qpad = ubuf.shape[0] - tile, qbuf.shape[0] - tile

    @pl.when(t == 0)
    def _():
        ubuf[0:upad, :] = jnp.zeros((upad, ubuf.shape[1]), F32)
        qbuf[0:qpad, :] = jnp.zeros((qpad, qbuf.shape[1]), F32)
        s_scr[...] = jnp.zeros(s_scr.shape, F32)

    x = x_ref[0]
    u, c_gate, qkv, z, bg = _project(x, w)

    ubuf[upad:upad + tile, :] = u
    for r0 in range(0, tile, CONV_ROWS):
        acc = jnp.zeros((CONV_ROWS, ubuf.shape[1]), F32)
        for j in range(kw):
            s = upad - hist + r0 + j
            acc = acc + ubuf[s:s + CONV_ROWS, :] * w["dw_w"][j:j + 1, :]
        cbuf[r0:r0 + CONV_ROWS, :] = acc
    c_out = _conv_branch_tail(cbuf[...], c_gate, w)

    qbuf[qpad:qpad + tile, :] = qkv
    qkv_c = jnp.zeros(qkv.shape, F32)
    for j in range(ks):
        s = qpad - qhist + j
        qkv_c = qkv_c + qbuf[s:s + tile, :] * w["dn_w"][j:j + 1, :]
    qs, kss, vs = _qkv_heads(qkv_c)

    @pl.when(t == pl.num_programs(1) - 1)
    def _():
        nconv_ref[0] = ubuf[upad + tile - hist:upad + tile, :]
        ndn_ref[0] = qbuf[qpad + tile - qhist:qpad + tile, :]

    ubuf[0:upad, :] = ubuf[tile:tile + upad, :]
    qbuf[0:qpad, :] = qbuf[tile:tile + qpad, :]

    masks = _chunk_masks(tile, chunk, SUB)
    gc, gct, gtot = _gate_scalars(bg, masks)
    o_heads = []
    for h in range(N_HEADS):
        beta = bg[:, h:h + 1]
        gcc = gc[:, N_HEADS + h:N_HEADS + h + 1]
        grow = gct[N_HEADS + h:N_HEADS + h + 1, :]
        gl = gtot[:, N_HEADS + h:N_HEADS + h + 1]
        uu, wk, qk, qd, kd = _chunk_local(qs[h], kss[h], vs[h], beta, gcc, grow, gl, masks, chunk, SUB)
        o_rows = []
        for n in range(tile // chunk):
            r = slice(n * chunk, (n + 1) * chunk)
            s_old = s_scr[h]
            m1 = _mm(jnp.concatenate([wk[r], qd[r]], axis=0), s_old)
            v_new = uu[r] - m1[:chunk]
            o_rows.append(m1[chunk:] + _mm(qk[r, n * chunk:(n + 1) * chunk], v_new))
            s_scr[h] = s_old * jnp.exp(gl[n * chunk:n * chunk + 1, :]) + _mm_tn(kd[r], v_new)
        o_heads.append(jnp.concatenate(o_rows, axis=0))

    @pl.when(t == pl.num_programs(1) - 1)
    def _():
        ns_ref[0] = s_scr[...]

    y_ref[0] = _finish(x, c_out, o_heads, z, p_ref[0], w)


def _sample_kernel(*refs, seqs, steps):
    x_ref, p_ref, sc_ref, sdn_ref, s_ref = refs[:5]
    nw = len(_WEIGHT_NAMES)
    w = dict(zip(_WEIGHT_NAMES, refs[5:5 + nw]))
    y_ref, nconv_ref, ndn_ref, ns_ref = refs[5 + nw:9 + nw]
    cb3, qb3 = refs[9 + nw:]
    rows = seqs * steps
    kw = w["dw_w"].shape[0]
    ks = w["dn_w"].shape[0]
    hist, qhist = kw - 1, ks - 1
    d_conv = cb3.shape[-1]
    upad, qpad = cb3.shape[1] - steps, qb3.shape[1] - steps

    x = x_ref[...]
    u, c_gate, qkv, z, bg = _project(x, w)

    cb3[:, upad - hist:upad, :] = sc_ref[...]
    cb3[:, upad:upad + steps, :] = u.reshape(seqs, steps, d_conv)
    grp = max(1, CONV_ROWS // steps)
    parts = []
    for s0 in range(0, seqs, grp):
        acc = jnp.zeros((grp, steps, d_conv), F32)
        for j in range(kw):
            o = upad - hist + j
            acc = acc + cb3[s0:s0 + grp, o:o + steps, :] * w["dw_w"][j:j + 1, :]
        parts.append(acc)
    c = jnp.concatenate(parts, axis=0).reshape(rows, d_conv)
    nconv_ref[...] = cb3[:, upad + steps - hist:upad + steps, :]
    c_out = _conv_branch_tail(c, c_gate, w)

    qb3[:, qpad - qhist:qpad, :] = sdn_ref[...]
    qb3[:, qpad:qpad + steps, :] = qkv.reshape(seqs, steps, qkv.shape[-1])
    qkv_c = jnp.zeros((seqs, steps, qkv.shape[-1]), F32)
    for j in range(ks):
        o = qpad - qhist + j
        qkv_c = qkv_c + qb3[:, o:o + steps, :] * w["dn_w"][j:j + 1, :]
    ndn_ref[...] = qb3[:, qpad + steps - qhist:qpad + steps, :]
    qs, kss, vs = _qkv_heads(qkv_c.reshape(rows, qkv.shape[-1]))

    masks = _chunk_masks(rows, steps, steps)
    gc, gct, gtot = _gate_scalars(bg, masks)
    seq_of_col = lax.broadcasted_iota(jnp.int32, (seqs, 1, rows), 2) // steps
    seq_id = lax.broadcasted_iota(jnp.int32, (seqs, 1, rows), 0)
    col_mask = (seq_of_col == seq_id).astype(F32)
    o_heads = []
    for h in range(N_HEADS):
        beta = bg[:, h:h + 1]
        gcc = gc[:, N_HEADS + h:N_HEADS + h + 1]
        grow = gct[N_HEADS + h:N_HEADS + h + 1, :]
        gl = gtot[:, N_HEADS + h:N_HEADS + h + 1]
        uu, wk, qk, qd, kd = _chunk_local(qs[h], kss[h], vs[h], beta, gcc, grow, gl, masks, steps, steps)
        s_old = s_ref[:, h]
        lhs = jnp.concatenate([wk.reshape(seqs, steps, HEAD_DIM), qd.reshape(seqs, steps, HEAD_DIM)], axis=1)
        m1 = lax.dot_general(lhs.astype(BF16), s_old.astype(BF16), (((2,), (1,)), ((0,), (0,))),
                             preferred_element_type=F32)
        v_new = uu - m1[:, :steps, :].reshape(rows, HEAD_DIM)
        o_heads.append(m1[:, steps:, :].reshape(rows, HEAD_DIM) + _mm(qk, v_new))
        kd_rows = (kd.T[None, :, :] * col_mask).reshape(seqs * HEAD_DIM, rows)
        ds = _mm(kd_rows, v_new).reshape(seqs, HEAD_DIM, HEAD_DIM)
        gl_seq = jnp.exp(gl.reshape(seqs, steps, 1)[:, 0:1, :])
        ns_ref[:, h] = s_old * gl_seq + ds

    y_ref[...] = _finish(x, c_out, o_heads, z, p_ref[...], w)


def _full_spec(a):
    nd = a.ndim
    return pl.BlockSpec(a.shape, lambda *_: (0,) * nd)


def kernel(x_prompt, x_sample, state_conv, state_dn_conv, state_dn_S, p_prompt, p_sample, norm_mix_g, w_in, conv_dw_w, conv_dw_b, conv_ln_g, conv_ln_b, conv_pw_w, dn_conv_w, dn_a_log, dn_dt_bias, dn_norm_g, w_out, ple_norm_g, ple_gate_w, ple_proj_w, final_norm_g):
    depth = w_in.shape[0]
    assert depth == 1, "single trunk layer"
    bsz, seqlen, d_model = x_prompt.shape
    dec_b, dec_l, _ = x_sample.shape
    d_conv = conv_dw_w.shape[-1]
    d_dn = N_HEADS * HEAD_DIM
    kw, ks = conv_dw_w.shape[1], dn_conv_w.shape[1]
    d_main = 3 * d_conv + 4 * d_dn
    assert w_in.shape[-1] == d_main + 2 * N_HEADS
    assert dn_conv_w.shape[-1] == 3 * d_dn and dn_norm_g.shape[-1] == HEAD_DIM
    tile, chunk = min(PROMPT_TILE, seqlen), min(PROMPT_CHUNK, seqlen)
    assert seqlen % tile == 0 and tile % chunk == 0 and chunk % SUB == 0 and tile % CONV_ROWS == 0
    assert tile >= 32 and dec_b % SAMPLE_SEQS == 0 and dec_l == 8

    row = lambda v: v.reshape(1, -1).astype(F32)
    lanes = jnp.zeros((1, 128), F32)
    weights = dict(
        g_mix=row(norm_mix_g[0]),
        w_in=w_in[0, :, :d_main].astype(BF16),
        w_tail=jnp.zeros((d_model, 128), BF16).at[:, :2 * N_HEADS].set(w_in[0, :, d_main:].astype(BF16)),
        dw_w=conv_dw_w[0].astype(F32), dw_b=row(conv_dw_b[0]), ln_g=row(conv_ln_g[0]), ln_b=row(conv_ln_b[0]),
        pw=conv_pw_w[0].astype(BF16),
        dn_w=dn_conv_w[0].astype(F32),
        a_log=lanes.at[0, N_HEADS:2 * N_HEADS].set(dn_a_log[0]),
        dt_bias=lanes.at[0, N_HEADS:2 * N_HEADS].set(dn_dt_bias[0]),
        dn_g=row(dn_norm_g[0]),
        w_out=w_out[0].astype(BF16),
        ple_g=row(ple_norm_g[0]), ple_gate=ple_gate_w[0].astype(BF16), ple_proj=ple_proj_w[0].astype(BF16),
        fin_g=row(final_norm_g),
    )
    wlist = [weights[n] for n in _WEIGHT_NAMES]
    wspecs = [_full_spec(a) for a in wlist]
    params = dict(vmem_limit_bytes=VMEM_LIMIT_BYTES)

    nt = seqlen // tile
    upad = -(-(kw - 1) // 8) * 8
    qpad = -(-(ks - 1) // 8) * 8
    y_p, nconv_p, ndn_p, ns_p = pl.pallas_call(
        functools.partial(_prompt_kernel, tile=tile, chunk=chunk),
        grid=(bsz, nt),
        in_specs=[pl.BlockSpec((1, tile, d_model), lambda b, t: (b, t, 0)),
                  pl.BlockSpec((1, tile, p_prompt.shape[-1]), lambda b, t: (b, t, 0))] + wspecs,
        out_specs=[pl.BlockSpec((1, tile, d_model), lambda b, t: (b, t, 0)),
                   pl.BlockSpec((1, kw - 1, d_conv), lambda b, t: (b, 0, 0)),
                   pl.BlockSpec((1, ks - 1, 3 * d_dn), lambda b, t: (b, 0, 0)),
                   pl.BlockSpec((1, N_HEADS, HEAD_DIM, HEAD_DIM), lambda b, t: (b, 0, 0, 0))],
        out_shape=[jax.ShapeDtypeStruct((bsz, seqlen, d_model), F32),
                   jax.ShapeDtypeStruct((bsz, kw - 1, d_conv), F32),
                   jax.ShapeDtypeStruct((bsz, ks - 1, 3 * d_dn), F32),
                   jax.ShapeDtypeStruct((bsz, N_HEADS, HEAD_DIM, HEAD_DIM), F32)],
        scratch_shapes=[pltpu.VMEM((upad + tile, d_conv), F32),
                        pltpu.VMEM((qpad + tile, 3 * d_dn), F32),
                        pltpu.VMEM((tile, d_conv), F32),
                        pltpu.VMEM((N_HEADS, HEAD_DIM, HEAD_DIM), F32)],
        compiler_params=pltpu.CompilerParams(dimension_semantics=("arbitrary", "arbitrary"), **params),
        name="prompt_layer",
    )(x_prompt, p_prompt[0], *wlist)

    seqs = SAMPLE_SEQS
    rows = seqs * dec_l
    xs = x_sample.reshape(dec_b * dec_l, d_model)
    ps = p_sample[0].reshape(dec_b * dec_l, -1)
    y_s, nconv_s, ndn_s, ns_s = pl.pallas_call(
        functools.partial(_sample_kernel, seqs=seqs, steps=dec_l),
        grid=(dec_b // seqs,),
        in_specs=[pl.BlockSpec((rows, d_model), lambda i: (i, 0)),
                  pl.BlockSpec((rows, ps.shape[-1]), lambda i: (i, 0)),
                  pl.BlockSpec((seqs, kw - 1, d_conv), lambda i: (i, 0, 0)),
                  pl.BlockSpec((seqs, ks - 1, 3 * d_dn), lambda i: (i, 0, 0)),
                  pl.BlockSpec((seqs, N_HEADS, HEAD_DIM, HEAD_DIM), lambda i: (i, 0, 0, 0))] + wspecs,
        out_specs=[pl.BlockSpec((rows, d_model), lambda i: (i, 0)),
                   pl.BlockSpec((seqs, kw - 1, d_conv), lambda i: (i, 0, 0)),
                   pl.BlockSpec((seqs, ks - 1, 3 * d_dn), lambda i: (i, 0, 0)),
                   pl.BlockSpec((seqs, N_HEADS, HEAD_DIM, HEAD_DIM), lambda i: (i, 0, 0, 0))],
        out_shape=[jax.ShapeDtypeStruct((dec_b * dec_l, d_model), F32),
                   jax.ShapeDtypeStruct((dec_b, kw - 1, d_conv), F32),
                   jax.ShapeDtypeStruct((dec_b, ks - 1, 3 * d_dn), F32),
                   jax.ShapeDtypeStruct((dec_b, N_HEADS, HEAD_DIM, HEAD_DIM), F32)],
        scratch_shapes=[pltpu.VMEM((seqs, upad + dec_l, d_conv), F32),
                        pltpu.VMEM((seqs, qpad + dec_l, 3 * d_dn), F32)],
        compiler_params=pltpu.CompilerParams(dimension_semantics=("arbitrary",), **params),
        name="sample_layer",
    )(xs, ps, state_conv[0], state_dn_conv[0], state_dn_S[0], *wlist)

    return (y_p, y_s.reshape(dec_b, dec_l, d_model), nconv_p[None], ndn_p[None], ns_p[None],
            nconv_s[None], ndn_s[None], ns_s[None])
```

```python
import functools

import jax
import jax.numpy as jnp
from jax import lax
from jax.experimental import pallas as pl
from jax.experimental.pallas import tpu as pltpu

EPS = 1e-6
N_HEADS = 4
HEAD_DIM = 128
SUB = 16
PROMPT_TILE = 256
PROMPT_CHUNK = 64
SAMPLE_SEQS = 16
CONV_ROWS = 32
VMEM_LIMIT_BYTES = 56 * 1024 * 1024

F32 = jnp.float32
BF16 = jnp.bfloat16


def _mm(a, b):
    return jnp.dot(a.astype(BF16), b.astype(BF16), preferred_element_type=F32)


def _mm_nt(a, b):
    return lax.dot_general(a.astype(BF16), b.astype(BF16), (((1,), (1,)), ((), ())),
                           preferred_element_type=F32)


def _mm_tn(a, b):
    return lax.dot_general(a.astype(BF16), b.astype(BF16), (((0,), (0,)), ((), ())),
                           preferred_element_type=F32)


def _mm_f32(a, b):
    return jnp.dot(a, b, preferred_element_type=F32, precision=lax.Precision.HIGHEST)


def _rms(x, g):
    return x * lax.rsqrt(jnp.mean(x * x, axis=-1, keepdims=True) + EPS) * g


def _silu(x):
    return x * jax.nn.sigmoid(x)


def _softplus(x):
    return jnp.maximum(x, 0.0) + jnp.log(1.0 + jnp.exp(-jnp.abs(x)))


def _project(x, w):
    d_conv = w["dw_w"].shape[-1]
    h = _rms(x, w["g_mix"][...]).astype(BF16)
    win = w["w_in"]
    ab = jnp.dot(h, win[:, 0:2 * d_conv], preferred_element_type=F32)
    u = ab[:, :d_conv] * jax.nn.sigmoid(ab[:, d_conv:])
    c_gate = jnp.dot(h, win[:, 2 * d_conv:3 * d_conv], preferred_element_type=F32)
    o1 = 3 * d_conv
    d_dn = N_HEADS * HEAD_DIM
    qkv = jnp.dot(h, win[:, o1:o1 + 3 * d_dn], preferred_element_type=F32)
    z = jnp.dot(h, win[:, o1 + 3 * d_dn:o1 + 4 * d_dn], preferred_element_type=F32)
    tail = jnp.dot(h, w["w_tail"][...], preferred_element_type=F32)
    lane = lax.broadcasted_iota(jnp.int32, tail.shape, 1)
    beta = jax.nn.sigmoid(tail)
    g = -jnp.exp(w["a_log"][...]) * _softplus(tail + w["dt_bias"][...])
    bg = jnp.where(lane < N_HEADS, beta, jnp.where(lane < 2 * N_HEADS, g, 0.0))
    return u, c_gate, qkv, z, bg


def _conv_branch_tail(c, c_gate, w):
    c = c + w["dw_b"][...]
    cc = c - jnp.mean(c, axis=-1, keepdims=True)
    c = cc * lax.rsqrt(jnp.mean(cc * cc, axis=-1, keepdims=True) + EPS) * w["ln_g"][...] + w["ln_b"][...]
    c = _silu(c)
    return _mm(c, w["pw"][...]) * _silu(c_gate)


def _qkv_heads(qkv_c):
    d_dn = N_HEADS * HEAD_DIM
    a = _silu(qkv_c)
    qs, ks, vs = [], [], []
    for h in range(N_HEADS):
        q = a[:, h * HEAD_DIM:(h + 1) * HEAD_DIM]
        k = a[:, d_dn + h * HEAD_DIM:d_dn + (h + 1) * HEAD_DIM]
        v = a[:, 2 * d_dn + h * HEAD_DIM:2 * d_dn + (h + 1) * HEAD_DIM]
        q = q * lax.rsqrt(jnp.sum(q * q, axis=-1, keepdims=True) + EPS) * (HEAD_DIM ** -0.5)
        k = k * lax.rsqrt(jnp.sum(k * k, axis=-1, keepdims=True) + EPS)
        qs.append(q); ks.append(k); vs.append(v)
    return qs, ks, vs


def _chunk_masks(rows, chunk, sub):
    ri = lax.broadcasted_iota(jnp.int32, (rows, rows), 0)
    ci = lax.broadcasted_iota(jnp.int32, (rows, rows), 1)
    same = (ri // chunk) == (ci // chunk)
    incl = same & (ci <= ri)
    strict = same & (ci < ri)
    diag = (ri // sub) == (ci // sub)
    eye = (ri == ci).astype(F32)
    return same, incl, strict, diag, eye


def _solve_unit_lower(a_list, rhs_list, diag, eye, chunk, sub):
    hs = range(len(a_list))
    d = [jnp.where(diag, a_list[h], 0.0) for h in hs]
    p = [eye - d[h] for h in hs]
    pw, span = d, 1
    while 2 * span < sub:
        pw = [_mm(pw[h], pw[h]) for h in hs]
        p = [_mm(p[h], eye + pw[h]) for h in hs]
        span *= 2
    nblk = chunk // sub
    if nblk == 1:
        return [_mm(p[h], rhs_list[h]) for h in hs]
    b = [_mm(p[h], a_list[h] - d[h]) for h in hs]
    y = [_mm(p[h], rhs_list[h]) for h in hs]
    powers, span, bp = [], 1, b
    while 2 * span < nblk:
        bp = [_mm(bp[h], bp[h]) for h in hs]
        powers.append(bp)
        span *= 2
    for bp in reversed(powers):
        y = [y[h] + _mm(bp[h], y[h]) for h in hs]
    return [y[h] - _mm(b[h], y[h]) for h in hs]


def _chunk_local(qs, ks, vs, bg, gc, gct, gtot, masks, chunk, sub):
    same, incl, strict, diag, eye = masks
    hs = range(N_HEADS)
    beta = [bg[:, h:h + 1] for h in hs]
    gcc = [gc[:, N_HEADS + h:N_HEADS + h + 1] for h in hs]
    gl = [gtot[:, N_HEADS + h:N_HEADS + h + 1] for h in hs]
    decay = [jnp.where(incl, jnp.exp(jnp.where(incl, gcc[h] - gct[N_HEADS + h:N_HEADS + h + 1, :], 0.0)), 0.0)
             for h in hs]
    kb = [ks[h] * beta[h] for h in hs]
    a = [jnp.where(strict, _mm_nt(kb[h], ks[h]) * decay[h], 0.0) for h in hs]
    qk = [_mm_nt(qs[h], ks[h]) * decay[h] for h in hs]
    egc = [jnp.exp(gcc[h]) for h in hs]
    rhs = [jnp.concatenate([vs[h] * beta[h], kb[h] * egc[h]], axis=1) for h in hs]
    y = _solve_unit_lower(a, rhs, diag, eye, chunk, sub)
    u = [y[h][:, :HEAD_DIM] for h in hs]
    wk = [y[h][:, HEAD_DIM:] for h in hs]
    qd = [qs[h] * egc[h] for h in hs]
    kd = [ks[h] * jnp.exp(gl[h] - gcc[h]) for h in hs]
    return u, wk, qk, qd, kd, gl


def _gate_scalars(bg, masks):
    same, incl = masks[0], masks[1]
    gc = _mm_f32(incl.astype(F32), bg)
    gtot = _mm_f32(same.astype(F32), bg)
    return gc, gc.T, gtot


def _finish(x, c_out, o_heads, z, p_emb, w):
    outs = []
    for h in range(N_HEADS):
        o = o_heads[h]
        o = o * lax.rsqrt(jnp.mean(o * o, axis=-1, keepdims=True) + EPS) * w["dn_g"][...]
        outs.append(o * _silu(z[:, h * HEAD_DIM:(h + 1) * HEAD_DIM]))
    mix_in = jnp.concatenate([c_out] + outs, axis=1)
    x = x + _mm(mix_in, w["w_out"][...])
    gate = jax.nn.sigmoid(_mm(_rms(x, w["ple_g"][...]), w["ple_gate"][...]))
    x = x + gate * _mm(p_emb, w["ple_proj"][...])
    return _rms(x, w["fin_g"][...])


_WEIGHT_NAMES = ("g_mix", "w_in", "w_tail", "dw_w", "dw_b", "ln_g", "ln_b", "pw", "dn_w", "a_log",
                 "dt_bias", "dn_g", "w_out", "ple_g", "ple_gate", "ple_proj", "fin_g")


def _prompt_kernel(*refs, tile, chunk):
    x_ref, p_ref = refs[0], refs[1]
    nw = len(_WEIGHT_NAMES)
    w = dict(zip(_WEIGHT_NAMES, refs[2:2 + nw]))
    y_ref, nconv_ref, ndn_ref, ns_ref = refs[2 + nw:6 + nw]
    ubuf, qbuf, cbuf, s_scr, ush = refs[6 + nw:]
    t = pl.program_id(1)
    kw = w["dw_w"].shape[0]
    ks = w["dn_w"].shape[0]
    hist, qhist = kw - 1, ks - 1
    upad, qpad = ubuf.shape[0] - tile, qbuf.shape[0] - tile

    @pl.when(t == 0)
    def _():
        ubuf[0:upad, :] = jnp.zeros((upad, ubuf.shape[1]), F32)
        qbuf[0:qpad, :] = jnp.zeros((qpad, qbuf.shape[1]), F32)
        s_scr[...] = jnp.zeros(s_scr.shape, F32)

    x = x_ref[0]
    u, c_gate, qkv, z, bg = _project(x, w)

    ubuf[upad:upad + tile, :] = u
    span = ush.shape[1]
    for r in range(1, 8):
        ush[r - 1] = ubuf[r:r + span, :]
    for r0 in range(0, tile, CONV_ROWS):
        acc = jnp.zeros((CONV_ROWS, ubuf.shape[1]), F32)
        for j in range(kw):
            a8, r = divmod(upad - hist + j, 8)
            s = r0 + 8 * a8
            win = ubuf[s:s + CONV_ROWS, :] if r == 0 else ush[r - 1, s:s + CONV_ROWS, :]
            acc = acc + win * w["dw_w"][j:j + 1, :]
        cbuf[r0:r0 + CONV_ROWS, :] = acc
    c_out = _conv_branch_tail(cbuf[...], c_gate, w)

    qbuf[qpad:qpad + tile, :] = qkv
    qkv_c = jnp.zeros(qkv.shape, F32)
    for j in range(ks):
        s = qpad - qhist + j
        qkv_c = qkv_c + qbuf[s:s + tile, :] * w["dn_w"][j:j + 1, :]
    qs, kss, vs = _qkv_heads(qkv_c)

    @pl.when(t == pl.num_programs(1) - 1)
    def _():
        nconv_ref[0] = ubuf[upad + tile - hist:upad + tile, :]
        ndn_ref[0] = qbuf[qpad + tile - qhist:qpad + tile, :]

    ubuf[0:upad, :] = ubuf[tile:tile + upad, :]
    qbuf[0:qpad, :] = qbuf[tile:tile + qpad, :]

    masks = _chunk_masks(tile, chunk, SUB)
    gc, gct, gtot = _gate_scalars(bg, masks)
    uu, wk, qk, qd, kd, gl = _chunk_local(qs, kss, vs, bg, gc, gct, gtot, masks, chunk, SUB)
    hs = range(N_HEADS)
    state = [s_scr[h] for h in hs]
    o_rows = [[] for _ in hs]
    for n in range(tile // chunk):
        r = slice(n * chunk, (n + 1) * chunk)
        m1 = [_mm(jnp.concatenate([wk[h][r], qd[h][r]], axis=0), state[h]) for h in hs]
        v_new = [uu[h][r] - m1[h][:chunk] for h in hs]
        for h in hs:
            o_rows[h].append(m1[h][chunk:] + _mm(qk[h][r, n * chunk:(n + 1) * chunk], v_new[h]))
        state = [state[h] * jnp.exp(gl[h][n * chunk:n * chunk + 1, :]) + _mm_tn(kd[h][r], v_new[h]) for h in hs]
    for h in hs:
        s_scr[h] = state[h]
    o_heads = [jnp.concatenate(o_rows[h], axis=0) for h in hs]

    @pl.when(t == pl.num_programs(1) - 1)
    def _():
        ns_ref[0] = s_scr[...]

    y_ref[0] = _finish(x, c_out, o_heads, z, p_ref[0], w)


def _sample_kernel(*refs, seqs, steps):
    x_ref, p_ref, sc_ref, sdn_ref, s_ref = refs[:5]
    nw = len(_WEIGHT_NAMES)
    w = dict(zip(_WEIGHT_NAMES, refs[5:5 + nw]))
    y_ref, nconv_ref, ndn_ref, ns_ref = refs[5 + nw:9 + nw]
    cb3, qb3 = refs[9 + nw:]
    rows = seqs * steps
    kw = w["dw_w"].shape[0]
    ks = w["dn_w"].shape[0]
    hist, qhist = kw - 1, ks - 1
    d_conv = cb3.shape[-1]
    upad, qpad = cb3.shape[1] - steps, qb3.shape[1] - steps

    x = x_ref[...]
    u, c_gate, qkv, z, bg = _project(x, w)

    cb3[:, upad - hist:upad, :] = sc_ref[...]
    cb3[:, upad:upad + steps, :] = u.reshape(seqs, steps, d_conv)
    grp = max(1, CONV_ROWS // steps)
    parts = []
    for s0 in range(0, seqs, grp):
        acc = jnp.zeros((grp, steps, d_conv), F32)
        for j in range(kw):
            o = upad - hist + j
            acc = acc + cb3[s0:s0 + grp, o:o + steps, :] * w["dw_w"][j:j + 1, :]
        parts.append(acc)
    c = jnp.concatenate(parts, axis=0).reshape(rows, d_conv)
    nconv_ref[...] = cb3[:, upad + steps - hist:upad + steps, :]
    c_out = _conv_branch_tail(c, c_gate, w)

    qb3[:, qpad - qhist:qpad, :] = sdn_ref[...]
    qb3[:, qpad:qpad + steps, :] = qkv.reshape(seqs, steps, qkv.shape[-1])
    qkv_c = jnp.zeros((seqs, steps, qkv.shape[-1]), F32)
    for j in range(ks):
        o = qpad - qhist + j
        qkv_c = qkv_c + qb3[:, o:o + steps, :] * w["dn_w"][j:j + 1, :]
    ndn_ref[...] = qb3[:, qpad + steps - qhist:qpad + steps, :]
    qs, kss, vs = _qkv_heads(qkv_c.reshape(rows, qkv.shape[-1]))

    masks = _chunk_masks(rows, steps, steps)
    gc, gct, gtot = _gate_scalars(bg, masks)
    seq_of_col = lax.broadcasted_iota(jnp.int32, (seqs, 1, rows), 2) // steps
    seq_id = lax.broadcasted_iota(jnp.int32, (seqs, 1, rows), 0)
    col_mask = (seq_of_col == seq_id).astype(F32)
    uu, wk, qk, qd, kd, gl = _chunk_local(qs, kss, vs, bg, gc, gct, gtot, masks, steps, steps)
    o_heads = []
    for h in range(N_HEADS):
        s_old = s_ref[:, h]
        lhs = jnp.concatenate([wk[h].reshape(seqs, steps, HEAD_DIM), qd[h].reshape(seqs, steps, HEAD_DIM)], axis=1)
        m1 = lax.dot_general(lhs.astype(BF16), s_old.astype(BF16), (((2,), (1,)), ((0,), (0,))),
                             preferred_element_type=F32)
        v_new = uu[h] - m1[:, :steps, :].reshape(rows, HEAD_DIM)
        o_heads.append(m1[:, steps:, :].reshape(rows, HEAD_DIM) + _mm(qk[h], v_new))
        kd_rows = (kd[h].T[None, :, :] * col_mask).reshape(seqs * HEAD_DIM, rows)
        ds = _mm(kd_rows, v_new).reshape(seqs, HEAD_DIM, HEAD_DIM)
        gl_seq = jnp.exp(gl[h].reshape(seqs, steps, 1)[:, 0:1, :])
        ns_ref[:, h] = s_old * gl_seq + ds

    y_ref[...] = _finish(x, c_out, o_heads, z, p_ref[...], w)


def _full_spec(a):
    nd = a.ndim
    return pl.BlockSpec(a.shape, lambda *_: (0,) * nd)


def kernel(x_prompt, x_sample, state_conv, state_dn_conv, state_dn_S, p_prompt, p_sample, norm_mix_g, w_in, conv_dw_w, conv_dw_b, conv_ln_g, conv_ln_b, conv_pw_w, dn_conv_w, dn_a_log, dn_dt_bias, dn_norm_g, w_out, ple_norm_g, ple_gate_w, ple_proj_w, final_norm_g):
    depth = w_in.shape[0]
    assert depth == 1, "single trunk layer"
    bsz, seqlen, d_model = x_prompt.shape
    dec_b, dec_l, _ = x_sample.shape
    d_conv = conv_dw_w.shape[-1]
    d_dn = N_HEADS * HEAD_DIM
    kw, ks = conv_dw_w.shape[1], dn_conv_w.shape[1]
    d_main = 3 * d_conv + 4 * d_dn
    assert w_in.shape[-1] == d_main + 2 * N_HEADS
    assert dn_conv_w.shape[-1] == 3 * d_dn and dn_norm_g.shape[-1] == HEAD_DIM
    tile, chunk = min(PROMPT_TILE, seqlen), min(PROMPT_CHUNK, seqlen)
    assert seqlen % tile == 0 and tile % chunk == 0 and chunk % SUB == 0 and tile % CONV_ROWS == 0
    assert tile >= 32 and dec_b % SAMPLE_SEQS == 0 and dec_l == 8

    row = lambda v: v.reshape(1, -1).astype(F32)
    lanes = jnp.zeros((1, 128), F32)
    weights = dict(
        g_mix=row(norm_mix_g[0]),
        w_in=w_in[0, :, :d_main].astype(BF16),
        w_tail=jnp.zeros((d_model, 128), BF16).at[:, :2 * N_HEADS].set(w_in[0, :, d_main:].astype(BF16)),
        dw_w=conv_dw_w[0].astype(F32), dw_b=row(conv_dw_b[0]), ln_g=row(conv_ln_g[0]), ln_b=row(conv_ln_b[0]),
        pw=conv_pw_w[0].astype(BF16),
        dn_w=dn_conv_w[0].astype(F32),
        a_log=lanes.at[0, N_HEADS:2 * N_HEADS].set(dn_a_log[0]),
        dt_bias=lanes.at[0, N_HEADS:2 * N_HEADS].set(dn_dt_bias[0]),
        dn_g=row(dn_norm_g[0]),
        w_out=w_out[0].astype(BF16),
        ple_g=row(ple_norm_g[0]), ple_gate=ple_gate_w[0].astype(BF16), ple_proj=ple_proj_w[0].astype(BF16),
        fin_g=row(final_norm_g),
    )
    wlist = [weights[n] for n in _WEIGHT_NAMES]
    wspecs = [_full_spec(a) for a in wlist]
    params = dict(vmem_limit_bytes=VMEM_LIMIT_BYTES)

    nt = seqlen // tile
    upad = -(-(kw - 1) // 8) * 8
    qpad = -(-(ks - 1) // 8) * 8
    y_p, nconv_p, ndn_p, ns_p = pl.pallas_call(
        functools.partial(_prompt_kernel, tile=tile, chunk=chunk),
        grid=(bsz, nt),
        in_specs=[pl.BlockSpec((1, tile, d_model), lambda b, t: (b, t, 0)),
                  pl.BlockSpec((1, tile, p_prompt.shape[-1]), lambda b, t: (b, t, 0))] + wspecs,
        out_specs=[pl.BlockSpec((1, tile, d_model), lambda b, t: (b, t, 0)),
                   pl.BlockSpec((1, kw - 1, d_conv), lambda b, t: (b, 0, 0)),
                   pl.BlockSpec((1, ks - 1, 3 * d_dn), lambda b, t: (b, 0, 0)),
                   pl.BlockSpec((1, N_HEADS, HEAD_DIM, HEAD_DIM), lambda b, t: (b, 0, 0, 0))],
        out_shape=[jax.ShapeDtypeStruct((bsz, seqlen, d_model), F32),
                   jax.ShapeDtypeStruct((bsz, kw - 1, d_conv), F32),
                   jax.ShapeDtypeStruct((bsz, ks - 1, 3 * d_dn), F32),
                   jax.ShapeDtypeStruct((bsz, N_HEADS, HEAD_DIM, HEAD_DIM), F32)],
        scratch_shapes=[pltpu.VMEM((upad + tile, d_conv), F32),
                        pltpu.VMEM((qpad + tile, 3 * d_dn), F32),
                        pltpu.VMEM((tile, d_conv), F32),
                        pltpu.VMEM((N_HEADS, HEAD_DIM, HEAD_DIM), F32),
                        pltpu.VMEM((7, tile + upad - 8, d_conv), F32)],
        compiler_params=pltpu.CompilerParams(dimension_semantics=("arbitrary", "arbitrary"), **params),
        name="prompt_layer",
    )(x_prompt, p_prompt[0], *wlist)

    seqs = SAMPLE_SEQS
    rows = seqs * dec_l
    xs = x_sample.reshape(dec_b * dec_l, d_model)
    ps = p_sample[0].reshape(dec_b * dec_l, -1)
    y_s, nconv_s, ndn_s, ns_s = pl.pallas_call(
        functools.partial(_sample_kernel, seqs=seqs, steps=dec_l),
        grid=(dec_b // seqs,),
        in_specs=[pl.BlockSpec((rows, d_model), lambda i: (i, 0)),
                  pl.BlockSpec((rows, ps.shape[-1]), lambda i: (i, 0)),
                  pl.BlockSpec((seqs, kw - 1, d_conv), lambda i: (i, 0, 0)),
                  pl.BlockSpec((seqs, ks - 1, 3 * d_dn), lambda i: (i, 0, 0)),
                  pl.BlockSpec((seqs, N_HEADS, HEAD_DIM, HEAD_DIM), lambda i: (i, 0, 0, 0))] + wspecs,
        out_specs=[pl.BlockSpec((rows, d_model), lambda i: (i, 0)),
                   pl.BlockSpec((seqs, kw - 1, d_conv), lambda i: (i, 0, 0)),
                   pl.BlockSpec((seqs, ks - 1, 3 * d_dn), lambda i: (i, 0, 0)),
                   pl.BlockSpec((seqs, N_HEADS, HEAD_DIM, HEAD_DIM), lambda i: (i, 0, 0, 0))],
        out_shape=[jax.ShapeDtypeStruct((dec_b * dec_l, d_model), F32),
                   jax.ShapeDtypeStruct((dec_b, kw - 1, d_conv), F32),
                   jax.ShapeDtypeStruct((dec_b, ks - 1, 3 * d_dn), F32),
                   jax.ShapeDtypeStruct((dec_b, N_HEADS, HEAD_DIM, HEAD_DIM), F32)],
        scratch_shapes=[pltpu.VMEM((seqs, upad + dec_l, d_conv), F32),
                        pltpu.VMEM((seqs, qpad + dec_l, 3 * d_dn), F32)],
        compiler_params=pltpu.CompilerParams(dimension_semantics=("arbitrary",), **params),
        name="sample_layer",
    )(xs, ps, state_conv[0], state_dn_conv[0], state_dn_S[0], *wlist)

    return (y_p, y_s.reshape(dec_b, dec_l, d_model), nconv_p[None], ndn_p[None], ns_p[None],
            nconv_s[None], ndn_s[None], ns_s[None])
```

```python
import functools

import jax
import jax.numpy as jnp
from jax import lax
from jax.experimental import pallas as pl
from jax.experimental.pallas import tpu as pltpu

EPS = 1e-6
N_HEADS = 4
HEAD_DIM = 128
SUB = 16
PROMPT_TILE = 256
PROMPT_CHUNK = 64
SAMPLE_SEQS = 16
CONV_ROWS = 32
VMEM_LIMIT_BYTES = 56 * 1024 * 1024

F32 = jnp.float32
BF16 = jnp.bfloat16


def _mm(a, b):
    return jnp.dot(a.astype(BF16), b.astype(BF16), preferred_element_type=F32)


def _mm_nt(a, b):
    return lax.dot_general(a.astype(BF16), b.astype(BF16), (((1,), (1,)), ((), ())),
                           preferred_element_type=F32)


def _rms(x, g):
    return x * lax.rsqrt(jnp.mean(x * x, axis=-1, keepdims=True) + EPS) * g


def _silu(x):
    return x * jax.nn.sigmoid(x)


def _softplus(x):
    return jnp.maximum(x, 0.0) + jnp.log(1.0 + jnp.exp(-jnp.abs(x)))


def _project(x, w):
    d_conv = w["dw_w"].shape[-1]
    h = _rms(x, w["g_mix"][...]).astype(BF16)
    win = w["w_in"]
    ab = jnp.dot(h, win[:, 0:2 * d_conv], preferred_element_type=F32)
    u = ab[:, :d_conv] * jax.nn.sigmoid(ab[:, d_conv:])
    c_gate = jnp.dot(h, win[:, 2 * d_conv:3 * d_conv], preferred_element_type=F32)
    o1 = 3 * d_conv
    d_dn = N_HEADS * HEAD_DIM
    qkv = jnp.dot(h, win[:, o1:o1 + 3 * d_dn], preferred_element_type=F32)
    z = jnp.dot(h, win[:, o1 + 3 * d_dn:o1 + 4 * d_dn], preferred_element_type=F32)
    tail = jnp.dot(h, w["w_tail"][...], preferred_element_type=F32)
    lane = lax.broadcasted_iota(jnp.int32, tail.shape, 1)
    beta = jax.nn.sigmoid(tail)
    g = -jnp.exp(w["a_log"][...]) * _softplus(tail + w["dt_bias"][...])
    bg = jnp.where(lane < N_HEADS, beta, jnp.where(lane < 2 * N_HEADS, g, 0.0))
    return u, c_gate, qkv, z, bg


def _conv_branch_tail(c, c_gate, w):
    c = c + w["dw_b"][...]
    cc = c - jnp.mean(c, axis=-1, keepdims=True)
    c = cc * lax.rsqrt(jnp.mean(cc * cc, axis=-1, keepdims=True) + EPS) * w["ln_g"][...] + w["ln_b"][...]
    c = _silu(c)
    return _mm(c, w["pw"][...]) * _silu(c_gate)


def _qkv_heads(qkv_c):
    d_dn = N_HEADS * HEAD_DIM
    a = _silu(qkv_c)
    qs, ks, vs = [], [], []
    for h in range(N_HEADS):
        q = a[:, h * HEAD_DIM:(h + 1) * HEAD_DIM]
        k = a[:, d_dn + h * HEAD_DIM:d_dn + (h + 1) * HEAD_DIM]
        v = a[:, 2 * d_dn + h * HEAD_DIM:2 * d_dn + (h + 1) * HEAD_DIM]
        q = q * lax.rsqrt(jnp.sum(q * q, axis=-1, keepdims=True) + EPS) * (HEAD_DIM ** -0.5)
        k = k * lax.rsqrt(jnp.sum(k * k, axis=-1, keepdims=True) + EPS)
        qs.append(q); ks.append(k); vs.append(v)
    return qs, ks, vs


def _chunk_masks(rows, chunk, sub):
    ri = lax.broadcasted_iota(jnp.int32, (rows, rows), 0)
    ci = lax.broadcasted_iota(jnp.int32, (rows, rows), 1)
    same = (ri // chunk) == (ci // chunk)
    incl = same & (ci <= ri)
    strict = same & (ci < ri)
    offdiag = (ri // sub) != (ci // sub)
    return incl, strict, offdiag


def _mmb(a, b_bf16):
    if a.shape[0] % 16:
        return jnp.dot(a.astype(F32), b_bf16.astype(F32), preferred_element_type=F32)
    return jnp.dot(a.astype(BF16), b_bf16, preferred_element_type=F32)


def _tri_inverse(a_list, offdiag, chunk, sub):
    rows = a_list[0].shape[0]
    nc = rows // chunk
    hs = range(len(a_list))
    pi = lax.broadcasted_iota(jnp.int32, (chunk, rows), 0)
    pl_ = lax.broadcasted_iota(jnp.int32, (chunk, rows), 1)
    lane_in, lane_blk = pl_ % chunk, pl_ // chunk
    eye_pan = (lane_in == pi).astype(F32)
    diag_pan = (pi // sub) == (lane_in // sub)
    packed = chunk % 16 == 0
    blk_masks = [(lane_blk == c).astype(BF16 if packed else F32) for c in range(nc)]

    def fold(full):
        out = full[0:chunk]
        for c in range(1, nc):
            out = out + full[c * chunk:(c + 1) * chunk]
        return out

    def expand(pan):
        src = pan.astype(BF16) if packed else pan
        return jnp.concatenate([src * blk_masks[c] for c in range(nc)], axis=0).astype(BF16)

    a_pan = [fold(a_list[h]) for h in hs]
    d_pan = [jnp.where(diag_pan, a_pan[h], 0.0) for h in hs]
    p_pan = [eye_pan - d_pan[h] for h in hs]
    pw_pan, span = d_pan, 1
    pw_full = [expand(d_pan[h]) for h in hs]
    while 2 * span < sub:
        pw_pan = [_mmb(pw_pan[h], pw_full[h]) for h in hs]
        pw_full = [expand(pw_pan[h]) for h in hs]
        p_pan = [p_pan[h] + _mmb(p_pan[h], pw_full[h]) for h in hs]
        span *= 2
    nblk = chunk // sub
    if nblk == 1:
        return [expand(p_pan[h]) for h in hs]
    p_full = [expand(p_pan[h]) for h in hs]
    n_full = [jnp.where(offdiag, a_list[h], 0.0).astype(BF16) for h in hs]
    b_pan = [_mmb(p_pan[h], n_full[h]) for h in hs]
    t_pan = [eye_pan - b_pan[h] for h in hs]
    bp_pan, span = b_pan, 1
    bp_full = [expand(b_pan[h]) for h in hs] if nblk > 2 else None
    while 2 * span < nblk:
        bp_pan = [_mmb(bp_pan[h], bp_full[h]) for h in hs]
        bp_full = [expand(bp_pan[h]) for h in hs]
        t_pan = [t_pan[h] + _mmb(t_pan[h], bp_full[h]) for h in hs]
        span *= 2
    return [expand(_mmb(t_pan[h], p_full[h])) for h in hs]


def _chunk_local(qs, ks, vs, bg, gc, gct, gtot, masks, chunk, sub):
    incl, strict, offdiag = masks
    hs = range(N_HEADS)
    beta = [bg[:, h:h + 1] for h in hs]
    gcc = [gc[:, N_HEADS + h:N_HEADS + h + 1] for h in hs]
    gl = [gtot[:, N_HEADS + h:N_HEADS + h + 1] for h in hs]
    decay = [jnp.where(incl, jnp.exp(jnp.where(incl, gcc[h] - gct[N_HEADS + h:N_HEADS + h + 1, :], 0.0)), 0.0)
             for h in hs]
    kb = [ks[h] * beta[h] for h in hs]
    a = [jnp.where(strict, _mm_nt(kb[h], ks[h]) * decay[h], 0.0) for h in hs]
    qk = [_mm_nt(qs[h], ks[h]) * decay[h] for h in hs]
    egc = [jnp.exp(gcc[h]) for h in hs]
    rhs = [jnp.concatenate([vs[h] * beta[h], kb[h] * egc[h]], axis=1) for h in hs]
    tinv = _tri_inverse(a, offdiag, chunk, sub)
    y = [_mmb(tinv[h], rhs[h].astype(BF16)) for h in hs]
    u = [y[h][:, :HEAD_DIM] for h in hs]
    wk = [y[h][:, HEAD_DIM:] for h in hs]
    qd = [qs[h] * egc[h] for h in hs]
    kd = [ks[h] * jnp.exp(gl[h] - gcc[h]) for h in hs]
    return u, wk, qk, qd, kd, gl


def _gate_scalars(bg, chunk):
    rows, lanes = bg.shape
    pos = lax.broadcasted_iota(jnp.int32, bg.shape, 0) % chunk
    gc, s = bg, 1
    while s < chunk:
        gc = gc + jnp.where(pos >= s, pltpu.roll(gc, s, 0), 0.0)
        s *= 2
    gtot = jnp.concatenate([jnp.broadcast_to(gc[e - 1:e, :], (chunk, lanes)) for e in range(chunk, rows + 1, chunk)],
                           axis=0)
    return gc, gc.T, gtot


def _finish(x, c_out, o_heads, z, p_emb, w):
    outs = []
    for h in range(N_HEADS):
        o = o_heads[h]
        o = o * lax.rsqrt(jnp.mean(o * o, axis=-1, keepdims=True) + EPS) * w["dn_g"][...]
        outs.append(o * _silu(z[:, h * HEAD_DIM:(h + 1) * HEAD_DIM]))
    mix_in = jnp.concatenate([c_out] + outs, axis=1)
    x = x + _mm(mix_in, w["w_out"][...])
    gate = jax.nn.sigmoid(_mm(_rms(x, w["ple_g"][...]), w["ple_gate"][...]))
    x = x + gate * _mm(p_emb, w["ple_proj"][...])
    return _rms(x, w["fin_g"][...])


_WEIGHT_NAMES = ("g_mix", "w_in", "w_tail", "dw_w", "dw_b", "ln_g", "ln_b", "pw", "dn_w", "a_log",
                 "dt_bias", "dn_g", "w_out", "ple_g", "ple_gate", "ple_proj", "fin_g")


def _prompt_kernel(*refs, tile, chunk):
    x_ref, p_ref = refs[0], refs[1]
    nw = len(_WEIGHT_NAMES)
    w = dict(zip(_WEIGHT_NAMES, refs[2:2 + nw]))
    y_ref, nconv_ref, ndn_ref, ns_ref = refs[2 + nw:6 + nw]
    ubuf, qbuf, cbuf, s_scr, ush = refs[6 + nw:]
    t = pl.program_id(1)
    kw = w["dw_w"].shape[0]
    ks = w["dn_w"].shape[0]
    hist, qhist = kw - 1, ks - 1
    upad, qpad = ubuf.shape[0] - tile, qbuf.shape[0] - tile

    @pl.when(t == 0)
    def _():
        ubuf[0:upad, :] = jnp.zeros((upad, ubuf.shape[1]), F32)
        qbuf[0:qpad, :] = jnp.zeros((qpad, qbuf.shape[1]), F32)
        s_scr[...] = jnp.zeros(s_scr.shape, F32)

    x = x_ref[0]
    u, c_gate, qkv, z, bg = _project(x, w)

    ubuf[upad:upad + tile, :] = u
    span = ush.shape[1]
    for r in range(1, 8):
        ush[r - 1] = ubuf[r:r + span, :]
    for r0 in range(0, tile, CONV_ROWS):
        acc = jnp.zeros((CONV_ROWS, ubuf.shape[1]), F32)
        for j in range(kw):
            a8, r = divmod(upad - hist + j, 8)
            s = r0 + 8 * a8
            win = ubuf[s:s + CONV_ROWS, :] if r == 0 else ush[r - 1, s:s + CONV_ROWS, :]
            acc = acc + win * w["dw_w"][j:j + 1, :]
        cbuf[r0:r0 + CONV_ROWS, :] = acc
    c_out = _conv_branch_tail(cbuf[...], c_gate, w)

    qbuf[qpad:qpad + tile, :] = qkv
    qkv_c = jnp.zeros(qkv.shape, F32)
    for j in range(ks):
        s = qpad - qhist + j
        qkv_c = qkv_c + qbuf[s:s + tile, :] * w["dn_w"][j:j + 1, :]
    qs, kss, vs = _qkv_heads(qkv_c)

    @pl.when(t == pl.num_programs(1) - 1)
    def _():
        nconv_ref[0] = ubuf[upad + tile - hist:upad + tile, :]
        ndn_ref[0] = qbuf[qpad + tile - qhist:qpad + tile, :]

    ubuf[0:upad, :] = ubuf[tile:tile + upad, :]
    qbuf[0:qpad, :] = qbuf[tile:tile + qpad, :]

    masks = _chunk_masks(tile, chunk, SUB)
    gc, gct, gtot = _gate_scalars(bg, chunk)
    uu, wk, qk, qd, kd, gl = _chunk_local(qs, kss, vs, bg, gc, gct, gtot, masks, chunk, SUB)
    hs = range(N_HEADS)
    half = HEAD_DIM // 2
    lane_c = lax.broadcasted_iota(jnp.int32, (chunk, HEAD_DIM), 1)
    lane_t = lax.broadcasted_iota(jnp.int32, (HEAD_DIM, HEAD_DIM), 1)
    qk_fold, kd_t = [], []
    for h in hs:
        f = qk[h][:, 0:HEAD_DIM]
        for c in range(1, tile // HEAD_DIM):
            f = f + qk[h][:, c * HEAD_DIM:(c + 1) * HEAD_DIM]
        qk_fold.append(f + pltpu.roll(f, half, 1))
        kd_t.append(kd[h].T)
    zs = jnp.zeros((HEAD_DIM, HEAD_DIM), BF16)
    zv = jnp.zeros((chunk, HEAD_DIM), BF16)
    state = [s_scr[h] for h in hs]
    o_rows = [[] for _ in hs]
    for n in range(tile // chunk):
        r = slice(n * chunk, (n + 1) * chunk)
        col, odd = divmod(n * chunk, HEAD_DIM)
        cs = slice(col * HEAD_DIM, (col + 1) * HEAD_DIM)
        new_state = list(state)
        for h0 in range(0, N_HEADS, 2):
            h1 = h0 + 1
            lhs1 = jnp.concatenate([jnp.concatenate([wk[h0][r], wk[h1][r]], axis=1),
                                    jnp.concatenate([qd[h0][r], qd[h1][r]], axis=1)], axis=0)
            s0, s1 = state[h0].astype(BF16), state[h1].astype(BF16)
            sbd = jnp.concatenate([jnp.concatenate([s0, zs], axis=1), jnp.concatenate([zs, s1], axis=1)], axis=0)
            m1 = _mmb(lhs1, sbd)
            v0 = uu[h0][r] - m1[:chunk, :HEAD_DIM]
            v1 = uu[h1][r] - m1[:chunk, HEAD_DIM:]
            vbd = jnp.concatenate([jnp.concatenate([v0.astype(BF16), zv], axis=1),
                                   jnp.concatenate([zv, v1.astype(BF16)], axis=1)], axis=0)
            k0, k1 = kd_t[h0][:, cs], kd_t[h1][:, cs]
            if odd:
                kpair = jnp.where(lane_t < half, pltpu.roll(k0, half, 1), k1)
            else:
                kpair = jnp.where(lane_t < half, k0, pltpu.roll(k1, half, 1))
            qpair = jnp.where(lane_c < half, qk_fold[h0][r], qk_fold[h1][r])
            m2 = _mmb(jnp.concatenate([qpair, kpair], axis=0), vbd)
            o_rows[h0].append(m1[chunk:, :HEAD_DIM] + m2[:chunk, :HEAD_DIM])
            o_rows[h1].append(m1[chunk:, HEAD_DIM:] + m2[:chunk, HEAD_DIM:])
            new_state[h0] = state[h0] * jnp.exp(gl[h0][n * chunk:n * chunk + 1, :]) + m2[chunk:, :HEAD_DIM]
            new_state[h1] = state[h1] * jnp.exp(gl[h1][n * chunk:n * chunk + 1, :]) + m2[chunk:, HEAD_DIM:]
        state = new_state
    for h in hs:
        s_scr[h] = state[h]
    o_heads = [jnp.concatenate(o_rows[h], axis=0) for h in hs]

    @pl.when(t == pl.num_programs(1) - 1)
    def _():
        ns_ref[0] = s_scr[...]

    y_ref[0] = _finish(x, c_out, o_heads, z, p_ref[0], w)


def _sample_kernel(*refs, seqs, steps):
    x_ref, p_ref, sc_ref, sdn_ref, s_ref = refs[:5]
    nw = len(_WEIGHT_NAMES)
    w = dict(zip(_WEIGHT_NAMES, refs[5:5 + nw]))
    y_ref, nconv_ref, ndn_ref, ns_ref = refs[5 + nw:9 + nw]
    cb3, qb3 = refs[9 + nw:]
    rows = seqs * steps
    kw = w["dw_w"].shape[0]
    ks = w["dn_w"].shape[0]
    hist, qhist = kw - 1, ks - 1
    d_conv = cb3.shape[-1]
    upad, qpad = cb3.shape[1] - steps, qb3.shape[1] - steps

    x = x_ref[...]
    u, c_gate, qkv, z, bg = _project(x, w)

    cb3[:, upad - hist:upad, :] = sc_ref[...]
    cb3[:, upad:upad + steps, :] = u.reshape(seqs, steps, d_conv)
    grp = max(1, CONV_ROWS // steps)
    parts = []
    for s0 in range(0, seqs, grp):
        acc = jnp.zeros((grp, steps, d_conv), F32)
        for j in range(kw):
            o = upad - hist + j
            acc = acc + cb3[s0:s0 + grp, o:o + steps, :] * w["dw_w"][j:j + 1, :]
        parts.append(acc)
    c = jnp.concatenate(parts, axis=0).reshape(rows, d_conv)
    nconv_ref[...] = cb3[:, upad + steps - hist:upad + steps, :]
    c_out = _conv_branch_tail(c, c_gate, w)

    qb3[:, qpad - qhist:qpad, :] = sdn_ref[...]
    qb3[:, qpad:qpad + steps, :] = qkv.reshape(seqs, steps, qkv.shape[-1])
    qkv_c = jnp.zeros((seqs, steps, qkv.shape[-1]), F32)
    for j in range(ks):
        o = qpad - qhist + j
        qkv_c = qkv_c + qb3[:, o:o + steps, :] * w["dn_w"][j:j + 1, :]
    ndn_ref[...] = qb3[:, qpad + steps - qhist:qpad + steps, :]
    qs, kss, vs = _qkv_heads(qkv_c.reshape(rows, qkv.shape[-1]))

    masks = _chunk_masks(rows, steps, steps)
    gc, gct, gtot = _gate_scalars(bg, steps)
    seq_of_col = lax.broadcasted_iota(jnp.int32, (seqs, 1, rows), 2) // steps
    seq_id = lax.broadcasted_iota(jnp.int32, (seqs, 1, rows), 0)
    col_mask = (seq_of_col == seq_id).astype(F32)
    uu, wk, qk, qd, kd, gl = _chunk_local(qs, kss, vs, bg, gc, gct, gtot, masks, steps, steps)
    o_heads = []
    for h in range(N_HEADS):
        s_old = s_ref[:, h]
        lhs = jnp.concatenate([wk[h].reshape(seqs, steps, HEAD_DIM), qd[h].reshape(seqs, steps, HEAD_DIM)], axis=1)
        m1 = lax.dot_general(lhs.astype(BF16), s_old.astype(BF16), (((2,), (1,)), ((0,), (0,))),
                             preferred_element_type=F32)
        v_new = uu[h] - m1[:, :steps, :].reshape(rows, HEAD_DIM)
        o_heads.append(m1[:, steps:, :].reshape(rows, HEAD_DIM) + _mm(qk[h], v_new))
        kd_rows = (kd[h].T[None, :, :] * col_mask).reshape(seqs * HEAD_DIM, rows)
        ds = _mm(kd_rows, v_new).reshape(seqs, HEAD_DIM, HEAD_DIM)
        gl_seq = jnp.exp(gl[h].reshape(seqs, steps, 1)[:, 0:1, :])
        ns_ref[:, h] = s_old * gl_seq + ds

    y_ref[...] = _finish(x, c_out, o_heads, z, p_ref[...], w)


def _full_spec(a):
    nd = a.ndim
    return pl.BlockSpec(a.shape, lambda *_: (0,) * nd)


def kernel(x_prompt, x_sample, state_conv, state_dn_conv, state_dn_S, p_prompt, p_sample, norm_mix_g, w_in, conv_dw_w, conv_dw_b, conv_ln_g, conv_ln_b, conv_pw_w, dn_conv_w, dn_a_log, dn_dt_bias, dn_norm_g, w_out, ple_norm_g, ple_gate_w, ple_proj_w, final_norm_g):
    depth = w_in.shape[0]
    assert depth == 1, "single trunk layer"
    bsz, seqlen, d_model = x_prompt.shape
    dec_b, dec_l, _ = x_sample.shape
    d_conv = conv_dw_w.shape[-1]
    d_dn = N_HEADS * HEAD_DIM
    kw, ks = conv_dw_w.shape[1], dn_conv_w.shape[1]
    d_main = 3 * d_conv + 4 * d_dn
    assert w_in.shape[-1] == d_main + 2 * N_HEADS
    assert dn_conv_w.shape[-1] == 3 * d_dn and dn_norm_g.shape[-1] == HEAD_DIM
    tile, chunk = min(PROMPT_TILE, seqlen), min(PROMPT_CHUNK, seqlen)
    assert seqlen % tile == 0 and tile % chunk == 0 and chunk % SUB == 0 and tile % CONV_ROWS == 0
    assert tile >= 32 and dec_b % SAMPLE_SEQS == 0 and dec_l == 8
    assert 2 * chunk == HEAD_DIM and tile % HEAD_DIM == 0

    row = lambda v: v.reshape(1, -1).astype(F32)
    lanes = jnp.zeros((1, 128), F32)
    weights = dict(
        g_mix=row(norm_mix_g[0]),
        w_in=w_in[0, :, :d_main].astype(BF16),
        w_tail=jnp.zeros((d_model, 128), BF16).at[:, :2 * N_HEADS].set(w_in[0, :, d_main:].astype(BF16)),
        dw_w=conv_dw_w[0].astype(F32), dw_b=row(conv_dw_b[0]), ln_g=row(conv_ln_g[0]), ln_b=row(conv_ln_b[0]),
        pw=conv_pw_w[0].astype(BF16),
        dn_w=dn_conv_w[0].astype(F32),
        a_log=lanes.at[0, N_HEADS:2 * N_HEADS].set(dn_a_log[0]),
        dt_bias=lanes.at[0, N_HEADS:2 * N_HEADS].set(dn_dt_bias[0]),
        dn_g=row(dn_norm_g[0]),
        w_out=w_out[0].astype(BF16),
        ple_g=row(ple_norm_g[0]), ple_gate=ple_gate_w[0].astype(BF16), ple_proj=ple_proj_w[0].astype(BF16),
        fin_g=row(final_norm_g),
    )
    wlist = [weights[n] for n in _WEIGHT_NAMES]
    wspecs = [_full_spec(a) for a in wlist]
    params = dict(vmem_limit_bytes=VMEM_LIMIT_BYTES)

    nt = seqlen // tile
    upad = -(-(kw - 1) // 8) * 8
    qpad = -(-(ks - 1) // 8) * 8
    y_p, nconv_p, ndn_p, ns_p = pl.pallas_call(
        functools.partial(_prompt_kernel, tile=tile, chunk=chunk),
        grid=(bsz, nt),
        in_specs=[pl.BlockSpec((1, tile, d_model), lambda b, t: (b, t, 0)),
                  pl.BlockSpec((1, tile, p_prompt.shape[-1]), lambda b, t: (b, t, 0))] + wspecs,
        out_specs=[pl.BlockSpec((1, tile, d_model), lambda b, t: (b, t, 0)),
                   pl.BlockSpec((1, kw - 1, d_conv), lambda b, t: (b, 0, 0)),
                   pl.BlockSpec((1, ks - 1, 3 * d_dn), lambda b, t: (b, 0, 0)),
                   pl.BlockSpec((1, N_HEADS, HEAD_DIM, HEAD_DIM), lambda b, t: (b, 0, 0, 0))],
        out_shape=[jax.ShapeDtypeStruct((bsz, seqlen, d_model), F32),
                   jax.ShapeDtypeStruct((bsz, kw - 1, d_conv), F32),
                   jax.ShapeDtypeStruct((bsz, ks - 1, 3 * d_dn), F32),
                   jax.ShapeDtypeStruct((bsz, N_HEADS, HEAD_DIM, HEAD_DIM), F32)],
        scratch_shapes=[pltpu.VMEM((upad + tile, d_conv), F32),
                        pltpu.VMEM((qpad + tile, 3 * d_dn), F32),
                        pltpu.VMEM((tile, d_conv), F32),
                        pltpu.VMEM((N_HEADS, HEAD_DIM, HEAD_DIM), F32),
                        pltpu.VMEM((7, tile + upad - 8, d_conv), F32)],
        compiler_params=pltpu.CompilerParams(dimension_semantics=("arbitrary", "arbitrary"), **params),
        name="prompt_layer",
    )(x_prompt, p_prompt[0], *wlist)

    seqs = SAMPLE_SEQS
    rows = seqs * dec_l
    xs = x_sample.reshape(dec_b * dec_l, d_model)
    ps = p_sample[0].reshape(dec_b * dec_l, -1)
    y_s, nconv_s, ndn_s, ns_s = pl.pallas_call(
        functools.partial(_sample_kernel, seqs=seqs, steps=dec_l),
        grid=(dec_b // seqs,),
        in_specs=[pl.BlockSpec((rows, d_model), lambda i: (i, 0)),
                  pl.BlockSpec((rows, ps.shape[-1]), lambda i: (i, 0)),
                  pl.BlockSpec((seqs, kw - 1, d_conv), lambda i: (i, 0, 0)),
                  pl.BlockSpec((seqs, ks - 1, 3 * d_dn), lambda i: (i, 0, 0)),
                  pl.BlockSpec((seqs, N_HEADS, HEAD_DIM, HEAD_DIM), lambda i: (i, 0, 0, 0))] + wspecs,
        out_specs=[pl.BlockSpec((rows, d_model), lambda i: (i, 0)),
                   pl.BlockSpec((seqs, kw - 1, d_conv), lambda i: (i, 0, 0)),
                   pl.BlockSpec((seqs, ks - 1, 3 * d_dn), lambda i: (i, 0, 0)),
                   pl.BlockSpec((seqs, N_HEADS, HEAD_DIM, HEAD_DIM), lambda i: (i, 0, 0, 0))],
        out_shape=[jax.ShapeDtypeStruct((dec_b * dec_l, d_model), F32),
                   jax.ShapeDtypeStruct((dec_b, kw - 1, d_conv), F32),
                   jax.ShapeDtypeStruct((dec_b, ks - 1, 3 * d_dn), F32),
                   jax.ShapeDtypeStruct((dec_b, N_HEADS, HEAD_DIM, HEAD_DIM), F32)],
        scratch_shapes=[pltpu.VMEM((seqs, upad + dec_l, d_conv), F32),
                        pltpu.VMEM((seqs, qpad + dec_l, 3 * d_dn), F32)],
        compiler_params=pltpu.CompilerParams(dimension_semantics=("arbitrary",), **params),
        name="sample_layer",
    )(xs, ps, state_conv[0], state_dn_conv[0], state_dn_S[0], *wlist)

    return (y_p, y_s.reshape(dec_b, dec_l, d_model), nconv_p[None], ndn_p[None], ns_p[None],
            nconv_s[None], ndn_s[None], ns_s[None])
```

```python
import functools

import jax
import jax.numpy as jnp
from jax import lax
from jax.experimental import pallas as pl
from jax.experimental.pallas import tpu as pltpu

EPS = 1e-6
N_HEADS = 4
HEAD_DIM = 128
SUB = 16
PROMPT_TILE = 256
PROMPT_CHUNK = 64
SAMPLE_SEQS = 16
CONV_ROWS = 32
VMEM_LIMIT_BYTES = 56 * 1024 * 1024

F32 = jnp.float32
BF16 = jnp.bfloat16


def _run(gen):
    try:
        while True:
            next(gen)
    except StopIteration as stop:
        return stop.value


def _interleave(gens, weights):
    n = len(gens)
    done, alive, out = [0] * n, [True] * n, [None] * n
    while any(alive):
        k = min((i for i in range(n) if alive[i]), key=lambda i: (done[i] + 1) / weights[i])
        try:
            next(gens[k])
            done[k] += 1
        except StopIteration as stop:
            out[k], alive[k] = stop.value, False
    return out


def _mm(a, b):
    return jnp.dot(a.astype(BF16), b.astype(BF16), preferred_element_type=F32)


def _mm_nt(a, b):
    return lax.dot_general(a.astype(BF16), b.astype(BF16), (((1,), (1,)), ((), ())),
                           preferred_element_type=F32)


def _mmb(a, b_bf16):
    if a.shape[0] % 16:
        return jnp.dot(a.astype(F32), b_bf16.astype(F32), preferred_element_type=F32)
    return jnp.dot(a.astype(BF16), b_bf16, preferred_element_type=F32)


def _rms(x, g):
    return x * lax.rsqrt(jnp.mean(x * x, axis=-1, keepdims=True) + EPS) * g


def _silu(x):
    return x * jax.nn.sigmoid(x)


def _softplus(x):
    return jnp.maximum(x, 0.0) + jnp.log(1.0 + jnp.exp(-jnp.abs(x)))


def _head(a, h):
    return a[:, h * HEAD_DIM:(h + 1) * HEAD_DIM]


class _Projection:
    def __init__(self, x, w):
        self.w, self.d_conv, self.d_dn = w, w["dw_w"].shape[-1], N_HEADS * HEAD_DIM
        self.h = _rms(x, w["g_mix"][...]).astype(BF16)

    def _cols(self, lo, n):
        return jnp.dot(self.h, self.w["w_in"][:, lo:lo + n], preferred_element_type=F32)

    def glu(self):
        ab = self._cols(0, 2 * self.d_conv)
        return ab[:, :self.d_conv] * jax.nn.sigmoid(ab[:, self.d_conv:])

    def c_gate(self):
        return self._cols(2 * self.d_conv, self.d_conv)

    def qkv(self, part):
        return self._cols(3 * self.d_conv + part * self.d_dn, self.d_dn)

    def z(self):
        return self._cols(3 * self.d_conv + 3 * self.d_dn, self.d_dn)

    def beta_g(self):
        w = self.w
        tail = jnp.dot(self.h, w["w_tail"][...], preferred_element_type=F32)
        lane = lax.broadcasted_iota(jnp.int32, tail.shape, 1)
        beta = jax.nn.sigmoid(tail)
        g = -jnp.exp(w["a_log"][...]) * _softplus(tail + w["dt_bias"][...])
        return jnp.where(lane < N_HEADS, beta, jnp.where(lane < 2 * N_HEADS, g, 0.0))


def _conv_branch_tail(c, c_gate, w):
    c = c + w["dw_b"][...]
    cc = c - jnp.mean(c, axis=-1, keepdims=True)
    c = cc * lax.rsqrt(jnp.mean(cc * cc, axis=-1, keepdims=True) + EPS) * w["ln_g"][...] + w["ln_b"][...]
    c = _silu(c)
    return _mm(c, w["pw"][...]) * _silu(c_gate)


def _qkv_heads(qkv_c):
    d_dn = N_HEADS * HEAD_DIM
    qs, ks, vs = [], [], []
    for h in range(N_HEADS):
        q = _silu(_head(qkv_c, h))
        k = _silu(_head(qkv_c[:, d_dn:2 * d_dn], h))
        v = _silu(_head(qkv_c[:, 2 * d_dn:], h))
        q = q * (lax.rsqrt(jnp.sum(q * q, axis=-1, keepdims=True) + EPS) * (HEAD_DIM ** -0.5))
        k = k * lax.rsqrt(jnp.sum(k * k, axis=-1, keepdims=True) + EPS)
        qs.append(q); ks.append(k); vs.append(v)
        yield
    return qs, ks, vs


def _chunk_masks(rows, chunk, sub):
    ri = lax.broadcasted_iota(jnp.int32, (rows, rows), 0)
    ci = lax.broadcasted_iota(jnp.int32, (rows, rows), 1)
    same = (ri // chunk) == (ci // chunk)
    incl = same & (ci <= ri)
    strict = same & (ci < ri)
    offdiag = (ri // sub) != (ci // sub)
    return incl, strict, offdiag


def _tri_inverse(a_list, offdiag, chunk, sub):
    rows = a_list[0].shape[0]
    nc = rows // chunk
    hs = range(len(a_list))
    pi = lax.broadcasted_iota(jnp.int32, (chunk, rows), 0)
    pl_ = lax.broadcasted_iota(jnp.int32, (chunk, rows), 1)
    lane_in, lane_blk = pl_ % chunk, pl_ // chunk
    eye_pan = (lane_in == pi).astype(F32)
    diag_pan = (pi // sub) == (lane_in // sub)
    packed = chunk % 16 == 0
    blk_masks = [(lane_blk == c).astype(BF16 if packed else F32) for c in range(nc)]

    def fold(full):
        out = full[0:chunk]
        for c in range(1, nc):
            out = out + full[c * chunk:(c + 1) * chunk]
        return out

    def expand(pan):
        src = pan.astype(BF16) if packed else pan
        return jnp.concatenate([src * blk_masks[c] for c in range(nc)], axis=0).astype(BF16)

    a_pan = [fold(a_list[h]) for h in hs]
    d_pan = [jnp.where(diag_pan, a_pan[h], 0.0) for h in hs]
    p_pan = [eye_pan - d_pan[h] for h in hs]
    pw_pan, span = d_pan, 1
    pw_full = [expand(d_pan[h]) for h in hs]
    while 2 * span < sub:
        pw_pan = [_mmb(pw_pan[h], pw_full[h]) for h in hs]
        yield
        pw_full = [expand(pw_pan[h]) for h in hs]
        p_pan = [p_pan[h] + _mmb(p_pan[h], pw_full[h]) for h in hs]
        span *= 2
    nblk = chunk // sub
    if nblk == 1:
        return [expand(p_pan[h]) for h in hs]
    yield
    p_full = [expand(p_pan[h]) for h in hs]
    n_full = [jnp.where(offdiag, a_list[h], 0.0).astype(BF16) for h in hs]
    b_pan = [_mmb(p_pan[h], n_full[h]) for h in hs]
    yield
    t_pan = [eye_pan - b_pan[h] for h in hs]
    bp_pan, span = b_pan, 1
    bp_full = [expand(b_pan[h]) for h in hs] if nblk > 2 else None
    while 2 * span < nblk:
        bp_pan = [_mmb(bp_pan[h], bp_full[h]) for h in hs]
        yield
        bp_full = [expand(bp_pan[h]) for h in hs]
        t_pan = [t_pan[h] + _mmb(t_pan[h], bp_full[h]) for h in hs]
        yield
        span *= 2
    tinv = [expand(_mmb(t_pan[h], p_full[h])) for h in hs]
    yield
    return tinv


def _gate_scalars(bg, chunk):
    rows, lanes = bg.shape
    pos = lax.broadcasted_iota(jnp.int32, bg.shape, 0) % chunk
    gc, s = bg, 1
    while s < chunk:
        gc = gc + jnp.where(pos >= s, pltpu.roll(gc, s, 0), 0.0)
        s *= 2
    gtot = jnp.concatenate([jnp.broadcast_to(gc[e - 1:e, :], (chunk, lanes)) for e in range(chunk, rows + 1, chunk)],
                           axis=0)
    return gc, gc.T, gtot


def _chunk_local(qs, ks, vs, bg, chunk, sub):
    incl, strict, offdiag = _chunk_masks(bg.shape[0], chunk, sub)
    gc, gct, gtot = _gate_scalars(bg, chunk)
    hs = range(N_HEADS)
    beta = [bg[:, h:h + 1] for h in hs]
    gcc = [gc[:, N_HEADS + h:N_HEADS + h + 1] for h in hs]
    gl = [gtot[:, N_HEADS + h:N_HEADS + h + 1] for h in hs]
    yield
    decay = [jnp.where(incl, jnp.exp(jnp.where(incl, gcc[h] - gct[N_HEADS + h:N_HEADS + h + 1, :], 0.0)), 0.0)
             for h in hs]
    kb = [ks[h] * beta[h] for h in hs]
    yield
    a = [jnp.where(strict, _mm_nt(kb[h], ks[h]) * decay[h], 0.0) for h in hs]
    yield
    qk = [_mm_nt(qs[h], ks[h]) * decay[h] for h in hs]
    egc = [jnp.exp(gcc[h]) for h in hs]
    rhs = [jnp.concatenate([vs[h] * beta[h], kb[h] * egc[h]], axis=1).astype(BF16) for h in hs]
    qd = [qs[h] * egc[h] for h in hs]
    kd = [ks[h] * jnp.exp(gl[h] - gcc[h]) for h in hs]
    yield
    tinv = yield from _tri_inverse(a, offdiag, chunk, sub)
    y = [_mmb(tinv[h], rhs[h]) for h in hs]
    yield
    u = [y[h][:, :HEAD_DIM] for h in hs]
    wk = [y[h][:, HEAD_DIM:] for h in hs]
    return u, wk, qk, qd, kd, gl, gtot


def _finish(x, c_out, o_heads, z, p_emb, w):
    outs = []
    for h in range(N_HEADS):
        o = o_heads[h]
        o = o * lax.rsqrt(jnp.mean(o * o, axis=-1, keepdims=True) + EPS) * w["dn_g"][...]
        outs.append(o * _silu(_head(z, h)))
    mix_in = jnp.concatenate([c_out] + outs, axis=1)
    x = x + _mm(mix_in, w["w_out"][...])
    yield
    gate = jax.nn.sigmoid(_mm(_rms(x, w["ple_g"][...]), w["ple_gate"][...]))
    yield
    x = x + gate * _mm(p_emb, w["ple_proj"][...])
    return _rms(x, w["fin_g"][...])


_WEIGHT_NAMES = ("g_mix", "w_in", "w_tail", "dw_w", "dw_b", "ln_g", "ln_b", "pw", "dn_w", "a_log",
                 "dt_bias", "dn_g", "w_out", "ple_g", "ple_gate", "ple_proj", "fin_g")


def _prompt_kernel(*refs, tile, chunk, nt):
    x_ref, xb_ref, pb_ref = refs[0], refs[1], refs[2]
    nw = len(_WEIGHT_NAMES)
    w = dict(zip(_WEIGHT_NAMES, refs[3:3 + nw]))
    y_ref, nconv_ref, ndn_ref, ns_ref = refs[3 + nw:7 + nw]
    (ubuf, qbuf, cbuf, s_scr, ush, st_c, st_q, st_k, st_v, st_z, st_bg,
     s2_c, s2_z, s2_u, s2_w, s2_qd, s2_qk, s2_kt, s2_gl) = refs[7 + nw:]
    s = pl.program_id(0)
    t_a = s % nt
    t_c = (s + nt - 2) % nt
    kw = w["dw_w"].shape[0]
    ks = w["dn_w"].shape[0]
    hist, qhist = kw - 1, ks - 1
    upad, qpad = ubuf.shape[0] - tile, qbuf.shape[0] - tile
    hs = range(N_HEADS)
    half = HEAD_DIM // 2

    @pl.when(s == 0)
    def _():
        for ref in (st_c, st_q, st_k, st_v, st_z, st_bg, s_scr, s2_c, s2_z, s2_u, s2_w, s2_qd, s2_qk, s2_kt, s2_gl):
            ref[...] = jnp.zeros(ref.shape, F32)

    @pl.when(t_a == 0)
    def _():
        ubuf[0:upad, :] = jnp.zeros((upad, ubuf.shape[1]), F32)
        qbuf[0:qpad, :] = jnp.zeros((qpad, qbuf.shape[1]), F32)

    c3, z3 = s2_c[...], s2_z[...]
    u3, w3, qd3, qk3, kt3, gl3 = s2_u[...], s2_w[...], s2_qd[...], s2_qk[...], s2_kt[...], s2_gl[...]
    c2, z2, bg2 = st_c[...], st_z[...], st_bg[...]
    q2, k2, v2 = st_q[...], st_k[...], st_v[...]

    def stage3():
        lane_c = lax.broadcasted_iota(jnp.int32, (chunk, HEAD_DIM), 1)
        lane_t = lax.broadcasted_iota(jnp.int32, (HEAD_DIM, HEAD_DIM), 1)
        uu, wk, qd, qk_fold = ([_head(a, h) for h in hs] for a in (u3, w3, qd3, qk3))
        kd_t = [kt3[h * HEAD_DIM:(h + 1) * HEAD_DIM, :] for h in hs]
        gl = [gl3[:, N_HEADS + h:N_HEADS + h + 1] for h in hs]
        zs = jnp.zeros((HEAD_DIM, HEAD_DIM), BF16)
        zv = jnp.zeros((chunk, HEAD_DIM), BF16)
        state = [jnp.where(t_c == 0, 0.0, s_scr[h]) for h in hs]
        o_rows = [[] for _ in hs]
        for n in range(tile // chunk):
            r = slice(n * chunk, (n + 1) * chunk)
            col, odd = divmod(n * chunk, HEAD_DIM)
            cs = slice(col * HEAD_DIM, (col + 1) * HEAD_DIM)
            pairs = range(0, N_HEADS, 2)
            m1 = {}
            for h0 in pairs:
                h1 = h0 + 1
                lhs1 = jnp.concatenate([jnp.concatenate([wk[h0][r], wk[h1][r]], axis=1),
                                        jnp.concatenate([qd[h0][r], qd[h1][r]], axis=1)], axis=0)
                s0, s1 = state[h0].astype(BF16), state[h1].astype(BF16)
                sbd = jnp.concatenate([jnp.concatenate([s0, zs], axis=1), jnp.concatenate([zs, s1], axis=1)], axis=0)
                m1[h0] = _mmb(lhs1, sbd)
            yield
            new_state = list(state)
            for h0 in pairs:
                h1 = h0 + 1
                v0 = uu[h0][r] - m1[h0][:chunk, :HEAD_DIM]
                v1 = uu[h1][r] - m1[h0][:chunk, HEAD_DIM:]
                vbd = jnp.concatenate([jnp.concatenate([v0.astype(BF16), zv], axis=1),
                                       jnp.concatenate([zv, v1.astype(BF16)], axis=1)], axis=0)
                k0, k1 = kd_t[h0][:, cs], kd_t[h1][:, cs]
                if odd:
                    kpair = jnp.where(lane_t < half, pltpu.roll(k0, half, 1), k1)
                else:
                    kpair = jnp.where(lane_t < half, k0, pltpu.roll(k1, half, 1))
                qpair = jnp.where(lane_c < half, qk_fold[h0][r], qk_fold[h1][r])
                m2 = _mmb(jnp.concatenate([qpair, kpair], axis=0), vbd)
                o_rows[h0].append(m1[h0][chunk:, :HEAD_DIM] + m2[:chunk, :HEAD_DIM])
                o_rows[h1].append(m1[h0][chunk:, HEAD_DIM:] + m2[:chunk, HEAD_DIM:])
                new_state[h0] = state[h0] * jnp.exp(gl[h0][n * chunk:n * chunk + 1, :]) + m2[chunk:, :HEAD_DIM]
                new_state[h1] = state[h1] * jnp.exp(gl[h1][n * chunk:n * chunk + 1, :]) + m2[chunk:, HEAD_DIM:]
            state = new_state
            yield
        for h in hs:
            s_scr[h] = state[h]
        o_heads = [jnp.concatenate(o_rows[h], axis=0) for h in hs]
        y_ref[0] = yield from _finish(xb_ref[0], c3, o_heads, z3, pb_ref[0], w)

    def stage2():
        uu2, wk2, qk2, qd2, kd2, _, gtot2 = yield from _chunk_local(
            [_head(q2, h) for h in hs], [_head(k2, h) for h in hs], [_head(v2, h) for h in hs], bg2, chunk, SUB)
        folds = []
        for h in hs:
            f = qk2[h][:, 0:HEAD_DIM]
            for c in range(1, tile // HEAD_DIM):
                f = f + qk2[h][:, c * HEAD_DIM:(c + 1) * HEAD_DIM]
            folds.append(f + pltpu.roll(f, half, 1))
        s2_u[...] = jnp.concatenate(uu2, axis=1)
        s2_w[...] = jnp.concatenate(wk2, axis=1)
        s2_qd[...] = jnp.concatenate(qd2, axis=1)
        s2_qk[...] = jnp.concatenate(folds, axis=1)
        yield
        s2_kt[...] = jnp.concatenate([kd2[h].T for h in hs], axis=0)
        s2_gl[...] = gtot2
        s2_c[...] = c2
        s2_z[...] = z2

    def stage1():
        proj = _Projection(x_ref[0], w)
        d_dn = N_HEADS * HEAD_DIM
        ubuf[upad:upad + tile, :] = proj.glu()
        yield
        span = ush.shape[1]
        for r in range(1, 8):
            ush[r - 1] = ubuf[r:r + span, :]
            if r % 2 == 0:
                part = r // 2 - 1
                qbuf[qpad:qpad + tile, part * d_dn:(part + 1) * d_dn] = proj.qkv(part)
            yield
        c_gate = None
        for i, r0 in enumerate(range(0, tile, CONV_ROWS)):
            acc = jnp.zeros((CONV_ROWS, ubuf.shape[1]), F32)
            for j in range(kw):
                a8, r = divmod(upad - hist + j, 8)
                o = r0 + 8 * a8
                win = ubuf[o:o + CONV_ROWS, :] if r == 0 else ush[r - 1, o:o + CONV_ROWS, :]
                acc = acc + win * w["dw_w"][j:j + 1, :]
            cbuf[r0:r0 + CONV_ROWS, :] = acc
            if i == 0:
                c_gate = proj.c_gate()
            elif i == 2:
                st_z[...] = proj.z()
            elif i == 4:
                st_bg[...] = proj.beta_g()
            yield
        st_c[...] = _conv_branch_tail(cbuf[...], c_gate, w)
        yield
        qkv_c = jnp.zeros((tile, 3 * d_dn), F32)
        for j in range(ks):
            o = qpad - qhist + j
            qkv_c = qkv_c + qbuf[o:o + tile, :] * w["dn_w"][j:j + 1, :]
            yield
        qa, ka, va = yield from _qkv_heads(qkv_c)
        st_q[...] = jnp.concatenate(qa, axis=1)
        st_k[...] = jnp.concatenate(ka, axis=1)
        st_v[...] = jnp.concatenate(va, axis=1)
        ubuf[0:upad, :] = ubuf[tile:tile + upad, :]
        qbuf[0:qpad, :] = qbuf[tile:tile + qpad, :]

    _interleave([stage3(), stage2(), stage1()], [11, 14, 29])

    @pl.when((t_a == nt - 1) & (s < pl.num_programs(0) - 2))
    def _():
        nconv_ref[0] = ubuf[upad + tile - hist:upad + tile, :]
        ndn_ref[0] = qbuf[qpad + tile - qhist:qpad + tile, :]

    @pl.when((t_c == nt - 1) & (s > 1))
    def _():
        ns_ref[0] = s_scr[...]


def _sample_kernel(*refs, seqs, steps):
    x_ref, p_ref, sc_ref, sdn_ref, s_ref = refs[:5]
    nw = len(_WEIGHT_NAMES)
    w = dict(zip(_WEIGHT_NAMES, refs[5:5 + nw]))
    y_ref, nconv_ref, ndn_ref, ns_ref = refs[5 + nw:9 + nw]
    cb3, qb3 = refs[9 + nw:]
    rows = seqs * steps
    kw = w["dw_w"].shape[0]
    ks = w["dn_w"].shape[0]
    hist, qhist = kw - 1, ks - 1
    d_conv = cb3.shape[-1]
    upad, qpad = cb3.shape[1] - steps, qb3.shape[1] - steps

    x = x_ref[...]
    proj = _Projection(x, w)
    u, c_gate, z, bg = proj.glu(), proj.c_gate(), proj.z(), proj.beta_g()
    qkv = jnp.concatenate([proj.qkv(part) for part in range(3)], axis=1)

    cb3[:, upad - hist:upad, :] = sc_ref[...]
    cb3[:, upad:upad + steps, :] = u.reshape(seqs, steps, d_conv)
    grp = max(1, CONV_ROWS // steps)
    parts = []
    for s0 in range(0, seqs, grp):
        acc = jnp.zeros((grp, steps, d_conv), F32)
        for j in range(kw):
            o = upad - hist + j
            acc = acc + cb3[s0:s0 + grp, o:o + steps, :] * w["dw_w"][j:j + 1, :]
        parts.append(acc)
    c = jnp.concatenate(parts, axis=0).reshape(rows, d_conv)
    nconv_ref[...] = cb3[:, upad + steps - hist:upad + steps, :]
    c_out = _conv_branch_tail(c, c_gate, w)

    qb3[:, qpad - qhist:qpad, :] = sdn_ref[...]
    qb3[:, qpad:qpad + steps, :] = qkv.reshape(seqs, steps, qkv.shape[-1])
    qkv_c = jnp.zeros((seqs, steps, qkv.shape[-1]), F32)
    for j in range(ks):
        o = qpad - qhist + j
        qkv_c = qkv_c + qb3[:, o:o + steps, :] * w["dn_w"][j:j + 1, :]
    ndn_ref[...] = qb3[:, qpad + steps - qhist:qpad + steps, :]
    qs, kss, vs = _run(_qkv_heads(qkv_c.reshape(rows, qkv.shape[-1])))

    seq_of_col = lax.broadcasted_iota(jnp.int32, (seqs, 1, rows), 2) // steps
    seq_id = lax.broadcasted_iota(jnp.int32, (seqs, 1, rows), 0)
    col_mask = (seq_of_col == seq_id).astype(F32)
    uu, wk, qk, qd, kd, gl, _ = _run(_chunk_local(qs, kss, vs, bg, steps, steps))
    o_heads = []
    for h in range(N_HEADS):
        s_old = s_ref[:, h]
        lhs = jnp.concatenate([wk[h].reshape(seqs, steps, HEAD_DIM), qd[h].reshape(seqs, steps, HEAD_DIM)], axis=1)
        m1 = lax.dot_general(lhs.astype(BF16), s_old.astype(BF16), (((2,), (1,)), ((0,), (0,))),
                             preferred_element_type=F32)
        v_new = uu[h] - m1[:, :steps, :].reshape(rows, HEAD_DIM)
        o_heads.append(m1[:, steps:, :].reshape(rows, HEAD_DIM) + _mm(qk[h], v_new))
        kd_rows = (kd[h].T[None, :, :] * col_mask).reshape(seqs * HEAD_DIM, rows)
        ds = _mm(kd_rows, v_new).reshape(seqs, HEAD_DIM, HEAD_DIM)
        gl_seq = jnp.exp(gl[h].reshape(seqs, steps, 1)[:, 0:1, :])
        ns_ref[:, h] = s_old * gl_seq + ds

    y_ref[...] = _run(_finish(x, c_out, o_heads, z, p_ref[...], w))


def _full_spec(a):
    nd = a.ndim
    return pl.BlockSpec(a.shape, lambda *_: (0,) * nd)


def kernel(x_prompt, x_sample, state_conv, state_dn_conv, state_dn_S, p_prompt, p_sample, norm_mix_g, w_in, conv_dw_w, conv_dw_b, conv_ln_g, conv_ln_b, conv_pw_w, dn_conv_w, dn_a_log, dn_dt_bias, dn_norm_g, w_out, ple_norm_g, ple_gate_w, ple_proj_w, final_norm_g):
    depth = w_in.shape[0]
    assert depth == 1, "single trunk layer"
    bsz, seqlen, d_model = x_prompt.shape
    dec_b, dec_l, _ = x_sample.shape
    d_conv = conv_dw_w.shape[-1]
    d_dn = N_HEADS * HEAD_DIM
    kw, ks = conv_dw_w.shape[1], dn_conv_w.shape[1]
    d_main = 3 * d_conv + 4 * d_dn
    assert w_in.shape[-1] == d_main + 2 * N_HEADS
    assert dn_conv_w.shape[-1] == 3 * d_dn and dn_norm_g.shape[-1] == HEAD_DIM
    tile, chunk = min(PROMPT_TILE, seqlen), min(PROMPT_CHUNK, seqlen)
    assert seqlen % tile == 0 and tile % chunk == 0 and chunk % SUB == 0 and tile % CONV_ROWS == 0
    assert tile >= 32 and dec_b % SAMPLE_SEQS == 0 and dec_l == 8
    assert 2 * chunk == HEAD_DIM and tile % HEAD_DIM == 0

    row = lambda v: v.reshape(1, -1).astype(F32)
    lanes = jnp.zeros((1, 128), F32)
    weights = dict(
        g_mix=row(norm_mix_g[0]),
        w_in=w_in[0, :, :d_main].astype(BF16),
        w_tail=jnp.zeros((d_model, 128), BF16).at[:, :2 * N_HEADS].set(w_in[0, :, d_main:].astype(BF16)),
        dw_w=conv_dw_w[0].astype(F32), dw_b=row(conv_dw_b[0]), ln_g=row(conv_ln_g[0]), ln_b=row(conv_ln_b[0]),
        pw=conv_pw_w[0].astype(BF16),
        dn_w=dn_conv_w[0].astype(F32),
        a_log=lanes.at[0, N_HEADS:2 * N_HEADS].set(dn_a_log[0]),
        dt_bias=lanes.at[0, N_HEADS:2 * N_HEADS].set(dn_dt_bias[0]),
        dn_g=row(dn_norm_g[0]),
        w_out=w_out[0].astype(BF16),
        ple_g=row(ple_norm_g[0]), ple_gate=ple_gate_w[0].astype(BF16), ple_proj=ple_proj_w[0].astype(BF16),
        fin_g=row(final_norm_g),
    )
    wlist = [weights[n] for n in _WEIGHT_NAMES]
    wspecs = [_full_spec(a) for a in wlist]
    params = dict(vmem_limit_bytes=VMEM_LIMIT_BYTES)

    nt = seqlen // tile
    upad = -(-(kw - 1) // 8) * 8
    qpad = -(-(ks - 1) // 8) * 8
    assert nt > 1
    n_tiles = bsz * nt
    front = lambda s: (jnp.minimum(s, n_tiles - 1) // nt, jnp.minimum(s, n_tiles - 1) % nt)
    back = lambda s: (jnp.maximum(s - 2, 0) // nt, jnp.maximum(s - 2, 0) % nt)
    stage = lambda cols: pltpu.VMEM((tile, cols), F32)
    y_p, nconv_p, ndn_p, ns_p = pl.pallas_call(
        functools.partial(_prompt_kernel, tile=tile, chunk=chunk, nt=nt),
        grid=(n_tiles + 2,),
        in_specs=[pl.BlockSpec((1, tile, d_model), lambda s: (*front(s), 0)),
                  pl.BlockSpec((1, tile, d_model), lambda s: (*back(s), 0)),
                  pl.BlockSpec((1, tile, p_prompt.shape[-1]), lambda s: (*back(s), 0))] + wspecs,
        out_specs=[pl.BlockSpec((1, tile, d_model), lambda s: (*back(s), 0)),
                   pl.BlockSpec((1, kw - 1, d_conv), lambda s: (front(s)[0], 0, 0)),
                   pl.BlockSpec((1, ks - 1, 3 * d_dn), lambda s: (front(s)[0], 0, 0)),
                   pl.BlockSpec((1, N_HEADS, HEAD_DIM, HEAD_DIM), lambda s: (back(s)[0], 0, 0, 0))],
        out_shape=[jax.ShapeDtypeStruct((bsz, seqlen, d_model), F32),
                   jax.ShapeDtypeStruct((bsz, kw - 1, d_conv), F32),
                   jax.ShapeDtypeStruct((bsz, ks - 1, 3 * d_dn), F32),
                   jax.ShapeDtypeStruct((bsz, N_HEADS, HEAD_DIM, HEAD_DIM), F32)],
        scratch_shapes=[pltpu.VMEM((upad + tile, d_conv), F32),
                        pltpu.VMEM((qpad + tile, 3 * d_dn), F32),
                        stage(d_conv),
                        pltpu.VMEM((N_HEADS, HEAD_DIM, HEAD_DIM), F32),
                        pltpu.VMEM((7, tile + upad - 8, d_conv), F32),
                        stage(d_conv), stage(d_dn), stage(d_dn), stage(d_dn), stage(d_dn), stage(128),
                        stage(d_conv), stage(d_dn), stage(d_dn), stage(d_dn), stage(d_dn), stage(d_dn),
                        pltpu.VMEM((d_dn, tile), F32), stage(128)],
        compiler_params=pltpu.CompilerParams(dimension_semantics=("arbitrary",), **params),
        name="prompt_layer",
    )(x_prompt, x_prompt, p_prompt[0], *wlist)

    seqs = SAMPLE_SEQS
    rows = seqs * dec_l
    xs = x_sample.reshape(dec_b * dec_l, d_model)
    ps = p_sample[0].reshape(dec_b * dec_l, -1)
    y_s, nconv_s, ndn_s, ns_s = pl.pallas_call(
        functools.partial(_sample_kernel, seqs=seqs, steps=dec_l),
        grid=(dec_b // seqs,),
        in_specs=[pl.BlockSpec((rows, d_model), lambda i: (i, 0)),
                  pl.BlockSpec((rows, ps.shape[-1]), lambda i: (i, 0)),
                  pl.BlockSpec((seqs, kw - 1, d_conv), lambda i: (i, 0, 0)),
                  pl.BlockSpec((seqs, ks - 1, 3 * d_dn), lambda i: (i, 0, 0)),
                  pl.BlockSpec((seqs, N_HEADS, HEAD_DIM, HEAD_DIM), lambda i: (i, 0, 0, 0))] + wspecs,
        out_specs=[pl.BlockSpec((rows, d_model), lambda i: (i, 0)),
                   pl.BlockSpec((seqs, kw - 1, d_conv), lambda i: (i, 0, 0)),
                   pl.BlockSpec((seqs, ks - 1, 3 * d_dn), lambda i: (i, 0, 0)),
                   pl.BlockSpec((seqs, N_HEADS, HEAD_DIM, HEAD_DIM), lambda i: (i, 0, 0, 0))],
        out_shape=[jax.ShapeDtypeStruct((dec_b * dec_l, d_model), F32),
                   jax.ShapeDtypeStruct((dec_b, kw - 1, d_conv), F32),
                   jax.ShapeDtypeStruct((dec_b, ks - 1, 3 * d_dn), F32),
                   jax.ShapeDtypeStruct((dec_b, N_HEADS, HEAD_DIM, HEAD_DIM), F32)],
        scratch_shapes=[pltpu.VMEM((seqs, upad + dec_l, d_conv), F32),
                        pltpu.VMEM((seqs, qpad + dec_l, 3 * d_dn), F32)],
        compiler_params=pltpu.CompilerParams(dimension_semantics=("arbitrary",), **params),
        name="sample_layer",
    )(xs, ps, state_conv[0], state_dn_conv[0], state_dn_S[0], *wlist)

    return (y_p, y_s.reshape(dec_b, dec_l, d_model), nconv_p[None], ndn_p[None], ns_p[None],
            nconv_s[None], ndn_s[None], ns_s[None])
```

```python
import functools

import jax
import jax.numpy as jnp
from jax import lax
from jax.experimental import pallas as pl
from jax.experimental.pallas import tpu as pltpu

EPS = 1e-6
N_HEADS = 4
HEAD_DIM = 128
LANES = 128
SUB = 16
PROMPT_TILE = 256
PROMPT_CHUNK = 64
SAMPLE_SEQS = 16
CONV_ROWS = 32
VMEM_LIMIT_BYTES = 56 * 1024 * 1024

F32 = jnp.float32
BF16 = jnp.bfloat16


def _run(gen):
    try:
        while True:
            next(gen)
    except StopIteration as stop:
        return stop.value


def _interleave(gens, weights):
    n = len(gens)
    done, alive, out = [0] * n, [True] * n, [None] * n
    while any(alive):
        k = min((i for i in range(n) if alive[i]), key=lambda i: (done[i] + 1) / weights[i])
        try:
            next(gens[k])
            done[k] += 1
        except StopIteration as stop:
            out[k], alive[k] = stop.value, False
    return out


def _mm(a, b):
    return jnp.dot(a.astype(BF16), b.astype(BF16), preferred_element_type=F32)


def _mm_nt(a, b):
    return lax.dot_general(a.astype(BF16), b.astype(BF16), (((1,), (1,)), ((), ())),
                           preferred_element_type=F32)


def _mmb(a, b_bf16):
    if a.shape[0] % 16:
        return jnp.dot(a.astype(F32), b_bf16.astype(F32), preferred_element_type=F32)
    return jnp.dot(a.astype(BF16), b_bf16, preferred_element_type=F32)


def _rms(x, g):
    return x * lax.rsqrt(jnp.mean(x * x, axis=-1, keepdims=True) + EPS) * g


def _silu(x):
    return x * jax.nn.sigmoid(x)


def _softplus(x):
    return jnp.maximum(x, 0.0) + jnp.log(1.0 + jnp.exp(-jnp.abs(x)))


def _head(a, h):
    return a[:, h * HEAD_DIM:(h + 1) * HEAD_DIM]


class _Projection:
    def __init__(self, x, w):
        self.w, self.d_conv, self.d_dn = w, w["dw_w"].shape[-1], N_HEADS * HEAD_DIM
        self.h = _rms(x, w["g_mix"][...]).astype(BF16)

    def _cols(self, lo, n):
        return jnp.dot(self.h, self.w["w_in"][:, lo:lo + n], preferred_element_type=F32)

    def glu(self):
        ab = self._cols(0, 2 * self.d_conv)
        return ab[:, :self.d_conv] * jax.nn.sigmoid(ab[:, self.d_conv:])

    def c_gate(self):
        return self._cols(2 * self.d_conv, self.d_conv)

    def qkv(self, part):
        return self._cols(3 * self.d_conv + part * self.d_dn, self.d_dn)

    def z(self):
        return self._cols(3 * self.d_conv + 3 * self.d_dn, self.d_dn)

    def beta_g(self):
        w = self.w
        tail = jnp.dot(self.h, w["w_tail"][...], preferred_element_type=F32)
        lane = lax.broadcasted_iota(jnp.int32, tail.shape, 1)
        beta = jax.nn.sigmoid(tail)
        g = -jnp.exp(w["a_log"][...]) * _softplus(tail + w["dt_bias"][...])
        return jnp.where(lane < N_HEADS, beta, jnp.where(lane < 2 * N_HEADS, g, 0.0))


def _conv_branch_tail(c, c_gate, w):
    c = c + w["dw_b"][...]
    cc = c - jnp.mean(c, axis=-1, keepdims=True)
    c = cc * lax.rsqrt(jnp.mean(cc * cc, axis=-1, keepdims=True) + EPS) * w["ln_g"][...] + w["ln_b"][...]
    c = _silu(c)
    return _mm(c, w["pw"][...]) * _silu(c_gate)


def _qkv_heads(qkv_c):
    d_dn = N_HEADS * HEAD_DIM
    qs, ks, vs = [], [], []
    for h in range(N_HEADS):
        q = _silu(_head(qkv_c, h))
        k = _silu(_head(qkv_c[:, d_dn:2 * d_dn], h))
        v = _silu(_head(qkv_c[:, 2 * d_dn:], h))
        q = q * (lax.rsqrt(jnp.sum(q * q, axis=-1, keepdims=True) + EPS) * (HEAD_DIM ** -0.5))
        k = k * lax.rsqrt(jnp.sum(k * k, axis=-1, keepdims=True) + EPS)
        qs.append(q); ks.append(k); vs.append(v)
        yield
    return qs, ks, vs


def _chunk_masks(rows, chunk, sub):
    ri = lax.broadcasted_iota(jnp.int32, (rows, rows), 0)
    ci = lax.broadcasted_iota(jnp.int32, (rows, rows), 1)
    same = (ri // chunk) == (ci // chunk)
    incl = same & (ci <= ri)
    strict = same & (ci < ri)
    offdiag = (ri // sub) != (ci // sub)
    return incl, strict, offdiag


def _tri_inverse(a_list, offdiag, chunk, sub):
    rows = a_list[0].shape[0]
    nc = rows // chunk
    hs = range(len(a_list))
    pi = lax.broadcasted_iota(jnp.int32, (chunk, rows), 0)
    pl_ = lax.broadcasted_iota(jnp.int32, (chunk, rows), 1)
    lane_in, lane_blk = pl_ % chunk, pl_ // chunk
    eye_pan = (lane_in == pi).astype(F32)
    diag_pan = (pi // sub) == (lane_in // sub)
    packed = chunk % 16 == 0
    blk_masks = [(lane_blk == c).astype(BF16 if packed else F32) for c in range(nc)]

    def fold(full):
        out = full[0:chunk]
        for c in range(1, nc):
            out = out + full[c * chunk:(c + 1) * chunk]
        return out

    def expand(pan):
        src = pan.astype(BF16) if packed else pan
        return jnp.concatenate([src * blk_masks[c] for c in range(nc)], axis=0).astype(BF16)

    a_pan = [fold(a_list[h]) for h in hs]
    d_pan = [jnp.where(diag_pan, a_pan[h], 0.0) for h in hs]
    p_pan = [eye_pan - d_pan[h] for h in hs]
    pw_pan, span = d_pan, 1
    pw_full = [expand(d_pan[h]) for h in hs]
    while 2 * span < sub:
        pw_pan = [_mmb(pw_pan[h], pw_full[h]) for h in hs]
        yield
        pw_full = [expand(pw_pan[h]) for h in hs]
        p_pan = [p_pan[h] + _mmb(p_pan[h], pw_full[h]) for h in hs]
        span *= 2
    nblk = chunk // sub
    if nblk == 1:
        return [expand(p_pan[h]) for h in hs]
    yield
    p_full = [expand(p_pan[h]) for h in hs]
    n_full = [jnp.where(offdiag, a_list[h], 0.0).astype(BF16) for h in hs]
    b_pan = [_mmb(p_pan[h], n_full[h]) for h in hs]
    yield
    t_pan = [eye_pan - b_pan[h] for h in hs]
    bp_pan, span = b_pan, 1
    bp_full = [expand(b_pan[h]) for h in hs] if nblk > 2 else None
    while 2 * span < nblk:
        bp_pan = [_mmb(bp_pan[h], bp_full[h]) for h in hs]
        yield
        bp_full = [expand(bp_pan[h]) for h in hs]
        t_pan = [t_pan[h] + _mmb(t_pan[h], bp_full[h]) for h in hs]
        yield
        span *= 2
    tinv = [expand(_mmb(t_pan[h], p_full[h])) for h in hs]
    yield
    return tinv


def _gate_scalars(bg, chunk):
    rows, lanes = bg.shape
    pos = lax.broadcasted_iota(jnp.int32, bg.shape, 0) % chunk
    gc, s = bg, 1
    while s < chunk:
        gc = gc + jnp.where(pos >= s, pltpu.roll(gc, s, 0), 0.0)
        s *= 2
    gtot = jnp.concatenate([jnp.broadcast_to(gc[e - 1:e, :], (chunk, lanes)) for e in range(chunk, rows + 1, chunk)],
                           axis=0)
    return gc, gc.T, gtot


def _chunk_local(qs, ks, vs, bg, chunk, sub):
    incl, strict, offdiag = _chunk_masks(bg.shape[0], chunk, sub)
    gc, gct, gtot = _gate_scalars(bg, chunk)
    hs = range(N_HEADS)
    beta = [bg[:, h:h + 1] for h in hs]
    gcc = [gc[:, N_HEADS + h:N_HEADS + h + 1] for h in hs]
    gl = [gtot[:, N_HEADS + h:N_HEADS + h + 1] for h in hs]
    yield
    decay = [jnp.where(incl, jnp.exp(jnp.where(incl, gcc[h] - gct[N_HEADS + h:N_HEADS + h + 1, :], 0.0)), 0.0)
             for h in hs]
    kb = [ks[h] * beta[h] for h in hs]
    yield
    a = [jnp.where(strict, _mm_nt(kb[h], ks[h]) * decay[h], 0.0) for h in hs]
    yield
    qk = [_mm_nt(qs[h], ks[h]) * decay[h] for h in hs]
    egc = [jnp.exp(gcc[h]) for h in hs]
    rhs = [jnp.concatenate([vs[h] * beta[h], kb[h] * egc[h]], axis=1).astype(BF16) for h in hs]
    qd = [qs[h] * egc[h] for h in hs]
    kd = [ks[h] * jnp.exp(gl[h] - gcc[h]) for h in hs]
    yield
    tinv = yield from _tri_inverse(a, offdiag, chunk, sub)
    y = [_mmb(tinv[h], rhs[h]) for h in hs]
    yield
    u = [y[h][:, :HEAD_DIM] for h in hs]
    wk = [y[h][:, HEAD_DIM:] for h in hs]
    return u, wk, qk, qd, kd, gl, gtot


def _finish(x, c_out, o_heads, z, p_emb, w):
    outs = []
    for h in range(N_HEADS):
        o = o_heads[h]
        o = o * lax.rsqrt(jnp.mean(o * o, axis=-1, keepdims=True) + EPS) * w["dn_g"][...]
        outs.append(o * _silu(_head(z, h)))
    mix_in = jnp.concatenate([c_out] + outs, axis=1)
    x = x + _mm(mix_in, w["w_out"][...])
    yield
    gate = jax.nn.sigmoid(_mm(_rms(x, w["ple_g"][...]), w["ple_gate"][...]))
    yield
    x = x + gate * _mm(p_emb, w["ple_proj"][...])
    return _rms(x, w["fin_g"][...])


_WEIGHT_NAMES = ("g_mix", "w_in", "w_tail", "dw_w", "dw_b", "ln_g", "ln_b", "pw", "dn_w", "a_log",
                 "dt_bias", "dn_g", "w_out", "ple_g", "ple_gate", "ple_proj", "fin_g")


def _prompt_kernel(*refs, tile, chunk, nt):
    x_ref, xb_ref, pb_ref = refs[0], refs[1], refs[2]
    nw = len(_WEIGHT_NAMES)
    w = dict(zip(_WEIGHT_NAMES, refs[3:3 + nw]))
    y_ref, nconv_ref, ndn_ref, ns_ref = refs[3 + nw:7 + nw]
    (ubuf, qbuf, cbuf, s_scr, qcv, st_c, st_q, st_k, st_v, st_z, st_bg,
     s2_c, s2_z, s2_u, s2_w, s2_qd, s2_qk, s2_kt, s2_gl) = refs[7 + nw:]
    s = pl.program_id(0)
    t_a = s % nt
    t_c = (s + nt - 2) % nt
    kw = w["dw_w"].shape[0]
    ks = w["dn_w"].shape[0]
    hist, qhist = kw - 1, ks - 1
    upad, qpad = ubuf.shape[1] - tile, qbuf.shape[1] - tile
    hs = range(N_HEADS)
    half = HEAD_DIM // 2

    @pl.when(s == 0)
    def _():
        for ref in (st_c, st_q, st_k, st_v, st_z, st_bg, s_scr, s2_c, s2_z, s2_u, s2_w, s2_qd, s2_qk, s2_kt, s2_gl):
            ref[...] = jnp.zeros(ref.shape, F32)

    @pl.when(t_a == 0)
    def _():
        ubuf[:, 0:upad, :] = jnp.zeros((ubuf.shape[0], upad, LANES), F32)
        qbuf[:, 0:qpad, :] = jnp.zeros((qbuf.shape[0], qpad, LANES), F32)

    c3, z3 = s2_c[...], s2_z[...]
    u3, w3, qd3, qk3, kt3, gl3 = s2_u[...], s2_w[...], s2_qd[...], s2_qk[...], s2_kt[...], s2_gl[...]
    c2, z2, bg2 = st_c[...], st_z[...], st_bg[...]
    q2, k2, v2 = st_q[...], st_k[...], st_v[...]

    def stage3():
        lane_c = lax.broadcasted_iota(jnp.int32, (chunk, HEAD_DIM), 1)
        lane_t = lax.broadcasted_iota(jnp.int32, (HEAD_DIM, HEAD_DIM), 1)
        uu, wk, qd, qk_fold = ([_head(a, h) for h in hs] for a in (u3, w3, qd3, qk3))
        kd_t = [kt3[h * HEAD_DIM:(h + 1) * HEAD_DIM, :] for h in hs]
        gl = [gl3[:, N_HEADS + h:N_HEADS + h + 1] for h in hs]
        zs = jnp.zeros((HEAD_DIM, HEAD_DIM), BF16)
        zv = jnp.zeros((chunk, HEAD_DIM), BF16)
        state = [jnp.where(t_c == 0, 0.0, s_scr[h]) for h in hs]
        o_rows = [[] for _ in hs]
        for n in range(tile // chunk):
            r = slice(n * chunk, (n + 1) * chunk)
            col, odd = divmod(n * chunk, HEAD_DIM)
            cs = slice(col * HEAD_DIM, (col + 1) * HEAD_DIM)
            pairs = range(0, N_HEADS, 2)
            m1 = {}
            for h0 in pairs:
                h1 = h0 + 1
                lhs1 = jnp.concatenate([jnp.concatenate([wk[h0][r], wk[h1][r]], axis=1),
                                        jnp.concatenate([qd[h0][r], qd[h1][r]], axis=1)], axis=0)
                s0, s1 = state[h0].astype(BF16), state[h1].astype(BF16)
                sbd = jnp.concatenate([jnp.concatenate([s0, zs], axis=1), jnp.concatenate([zs, s1], axis=1)], axis=0)
                m1[h0] = _mmb(lhs1, sbd)
            yield
            new_state = list(state)
            for h0 in pairs:
                h1 = h0 + 1
                v0 = uu[h0][r] - m1[h0][:chunk, :HEAD_DIM]
                v1 = uu[h1][r] - m1[h0][:chunk, HEAD_DIM:]
                vbd = jnp.concatenate([jnp.concatenate([v0.astype(BF16), zv], axis=1),
                                       jnp.concatenate([zv, v1.astype(BF16)], axis=1)], axis=0)
                k0, k1 = kd_t[h0][:, cs], kd_t[h1][:, cs]
                if odd:
                    kpair = jnp.where(lane_t < half, pltpu.roll(k0, half, 1), k1)
                else:
                    kpair = jnp.where(lane_t < half, k0, pltpu.roll(k1, half, 1))
                qpair = jnp.where(lane_c < half, qk_fold[h0][r], qk_fold[h1][r])
                m2 = _mmb(jnp.concatenate([qpair, kpair], axis=0), vbd)
                o_rows[h0].append(m1[h0][chunk:, :HEAD_DIM] + m2[:chunk, :HEAD_DIM])
                o_rows[h1].append(m1[h0][chunk:, HEAD_DIM:] + m2[:chunk, HEAD_DIM:])
                new_state[h0] = state[h0] * jnp.exp(gl[h0][n * chunk:n * chunk + 1, :]) + m2[chunk:, :HEAD_DIM]
                new_state[h1] = state[h1] * jnp.exp(gl[h1][n * chunk:n * chunk + 1, :]) + m2[chunk:, HEAD_DIM:]
            state = new_state
            yield
        for h in hs:
            s_scr[h] = state[h]
        o_heads = [jnp.concatenate(o_rows[h], axis=0) for h in hs]
        y_ref[0] = yield from _finish(xb_ref[0], c3, o_heads, z3, pb_ref[0], w)

    def stage2():
        uu2, wk2, qk2, qd2, kd2, _, gtot2 = yield from _chunk_local(
            [_head(q2, h) for h in hs], [_head(k2, h) for h in hs], [_head(v2, h) for h in hs], bg2, chunk, SUB)
        folds = []
        for h in hs:
            f = qk2[h][:, 0:HEAD_DIM]
            for c in range(1, tile // HEAD_DIM):
                f = f + qk2[h][:, c * HEAD_DIM:(c + 1) * HEAD_DIM]
            folds.append(f + pltpu.roll(f, half, 1))
        s2_u[...] = jnp.concatenate(uu2, axis=1)
        s2_w[...] = jnp.concatenate(wk2, axis=1)
        s2_qd[...] = jnp.concatenate(qd2, axis=1)
        s2_qk[...] = jnp.concatenate(folds, axis=1)
        yield
        s2_kt[...] = jnp.concatenate([kd2[h].T for h in hs], axis=0)
        s2_gl[...] = gtot2
        s2_c[...] = c2
        s2_z[...] = z2

    def stage1():
        proj = _Projection(x_ref[0], w)
        d_dn = N_HEADS * HEAD_DIM
        lanes = lambda c: slice(c * LANES, (c + 1) * LANES)
        u = proj.glu()
        for c in range(ubuf.shape[0]):
            ubuf[c, upad:upad + tile, :] = u[:, lanes(c)]
        yield
        for part in range(3):
            qkv_p = proj.qkv(part)
            for c in range(N_HEADS):
                qbuf[part * N_HEADS + c, qpad:qpad + tile, :] = qkv_p[:, lanes(c)]
            yield
        half_rows = CONV_ROWS
        c_gate = None
        for i, (r0, p) in enumerate((r0, p) for r0 in range(0, tile, 2 * CONV_ROWS) for p in range(2)):
            for c in range(ubuf.shape[0]):
                acc = jnp.zeros((half_rows, LANES), F32)
                for j in range(kw):
                    win = ubuf[c, pl.ds(upad - hist + r0 + p + j, half_rows, stride=2), :]
                    acc = acc + win * w["dw_w"][j:j + 1, lanes(c)]
                cbuf[c, pl.ds(r0 + p, half_rows, stride=2), :] = acc
            if i == 0:
                c_gate = proj.c_gate()
            elif i == 2:
                st_z[...] = proj.z()
            elif i == 4:
                st_bg[...] = proj.beta_g()
            yield
        c_raw = jnp.concatenate([cbuf[c] for c in range(cbuf.shape[0])], axis=1)
        st_c[...] = _conv_branch_tail(c_raw, c_gate, w)
        yield
        half_tile = tile // 2
        for c in range(qbuf.shape[0]):
            for p in range(2):
                acc = jnp.zeros((half_tile, LANES), F32)
                for j in range(ks):
                    win = qbuf[c, pl.ds(qpad - qhist + p + j, half_tile, stride=2), :]
                    acc = acc + win * w["dn_w"][j:j + 1, lanes(c)]
                qcv[c, pl.ds(p, half_tile, stride=2), :] = acc
            if c % 3 == 2:
                yield
        qkv_c = jnp.concatenate([qcv[c] for c in range(qcv.shape[0])], axis=1)
        qa, ka, va = yield from _qkv_heads(qkv_c)
        st_q[...] = jnp.concatenate(qa, axis=1)
        st_k[...] = jnp.concatenate(ka, axis=1)
        st_v[...] = jnp.concatenate(va, axis=1)
        for c in range(ubuf.shape[0]):
            ubuf[c, 0:upad, :] = ubuf[c, tile:tile + upad, :]
        for c in range(qbuf.shape[0]):
            qbuf[c, 0:qpad, :] = qbuf[c, tile:tile + qpad, :]

    _interleave([stage3(), stage2(), stage1()], [11, 14, 21])

    @pl.when((t_a == nt - 1) & (s < pl.num_programs(0) - 2))
    def _():
        nconv_ref[0] = jnp.concatenate([ubuf[c, upad + tile - hist:upad + tile, :] for c in range(ubuf.shape[0])],
                                       axis=1)
        ndn_ref[0] = jnp.concatenate([qbuf[c, qpad + tile - qhist:qpad + tile, :] for c in range(qbuf.shape[0])],
                                     axis=1)

    @pl.when((t_c == nt - 1) & (s > 1))
    def _():
        ns_ref[0] = s_scr[...]


def _sample_kernel(*refs, seqs, steps):
    x_ref, p_ref, sc_ref, sdn_ref, s_ref = refs[:5]
    nw = len(_WEIGHT_NAMES)
    w = dict(zip(_WEIGHT_NAMES, refs[5:5 + nw]))
    y_ref, nconv_ref, ndn_ref, ns_ref = refs[5 + nw:9 + nw]
    cb3, qb3 = refs[9 + nw:]
    rows = seqs * steps
    kw = w["dw_w"].shape[0]
    ks = w["dn_w"].shape[0]
    hist, qhist = kw - 1, ks - 1
    d_conv = cb3.shape[-1]
    upad, qpad = cb3.shape[1] - steps, qb3.shape[1] - steps

    x = x_ref[...]
    proj = _Projection(x, w)
    u, c_gate, z, bg = proj.glu(), proj.c_gate(), proj.z(), proj.beta_g()
    qkv = jnp.concatenate([proj.qkv(part) for part in range(3)], axis=1)

    cb3[:, upad - hist:upad, :] = sc_ref[...]
    cb3[:, upad:upad + steps, :] = u.reshape(seqs, steps, d_conv)
    grp = max(1, CONV_ROWS // steps)
    parts = []
    for s0 in range(0, seqs, grp):
        acc = jnp.zeros((grp, steps, d_conv), F32)
        for j in range(kw):
            o = upad - hist + j
            acc = acc + cb3[s0:s0 + grp, o:o + steps, :] * w["dw_w"][j:j + 1, :]
        parts.append(acc)
    c = jnp.concatenate(parts, axis=0).reshape(rows, d_conv)
    nconv_ref[...] = cb3[:, upad + steps - hist:upad + steps, :]
    c_out = _conv_branch_tail(c, c_gate, w)

    qb3[:, qpad - qhist:qpad, :] = sdn_ref[...]
    qb3[:, qpad:qpad + steps, :] = qkv.reshape(seqs, steps, qkv.shape[-1])
    qkv_c = jnp.zeros((seqs, steps, qkv.shape[-1]), F32)
    for j in range(ks):
        o = qpad - qhist + j
        qkv_c = qkv_c + qb3[:, o:o + steps, :] * w["dn_w"][j:j + 1, :]
    ndn_ref[...] = qb3[:, qpad + steps - qhist:qpad + steps, :]
    qs, kss, vs = _run(_qkv_heads(qkv_c.reshape(rows, qkv.shape[-1])))

    seq_of_col = lax.broadcasted_iota(jnp.int32, (seqs, 1, rows), 2) // steps
    seq_id = lax.broadcasted_iota(jnp.int32, (seqs, 1, rows), 0)
    col_mask = (seq_of_col == seq_id).astype(F32)
    uu, wk, qk, qd, kd, gl, _ = _run(_chunk_local(qs, kss, vs, bg, steps, steps))
    o_heads = []
    for h in range(N_HEADS):
        s_old = s_ref[:, h]
        lhs = jnp.concatenate([wk[h].reshape(seqs, steps, HEAD_DIM), qd[h].reshape(seqs, steps, HEAD_DIM)], axis=1)
        m1 = lax.dot_general(lhs.astype(BF16), s_old.astype(BF16), (((2,), (1,)), ((0,), (0,))),
                             preferred_element_type=F32)
        v_new = uu[h] - m1[:, :steps, :].reshape(rows, HEAD_DIM)
        o_heads.append(m1[:, steps:, :].reshape(rows, HEAD_DIM) + _mm(qk[h], v_new))
        kd_rows = (kd[h].T[None, :, :] * col_mask).reshape(seqs * HEAD_DIM, rows)
        ds = _mm(kd_rows, v_new).reshape(seqs, HEAD_DIM, HEAD_DIM)
        gl_seq = jnp.exp(gl[h].reshape(seqs, steps, 1)[:, 0:1, :])
        ns_ref[:, h] = s_old * gl_seq + ds

    y_ref[...] = _run(_finish(x, c_out, o_heads, z, p_ref[...], w))


def _full_spec(a):
    nd = a.ndim
    return pl.BlockSpec(a.shape, lambda *_: (0,) * nd)


def kernel(x_prompt, x_sample, state_conv, state_dn_conv, state_dn_S, p_prompt, p_sample, norm_mix_g, w_in, conv_dw_w, conv_dw_b, conv_ln_g, conv_ln_b, conv_pw_w, dn_conv_w, dn_a_log, dn_dt_bias, dn_norm_g, w_out, ple_norm_g, ple_gate_w, ple_proj_w, final_norm_g):
    depth = w_in.shape[0]
    assert depth == 1, "single trunk layer"
    bsz, seqlen, d_model = x_prompt.shape
    dec_b, dec_l, _ = x_sample.shape
    d_conv = conv_dw_w.shape[-1]
    d_dn = N_HEADS * HEAD_DIM
    kw, ks = conv_dw_w.shape[1], dn_conv_w.shape[1]
    d_main = 3 * d_conv + 4 * d_dn
    assert w_in.shape[-1] == d_main + 2 * N_HEADS
    assert dn_conv_w.shape[-1] == 3 * d_dn and dn_norm_g.shape[-1] == HEAD_DIM
    tile, chunk = min(PROMPT_TILE, seqlen), min(PROMPT_CHUNK, seqlen)
    assert seqlen % tile == 0 and tile % chunk == 0 and chunk % SUB == 0 and tile % CONV_ROWS == 0
    assert tile >= 32 and dec_b % SAMPLE_SEQS == 0 and dec_l == 8
    assert 2 * chunk == HEAD_DIM and tile % HEAD_DIM == 0

    row = lambda v: v.reshape(1, -1).astype(F32)
    lanes = jnp.zeros((1, 128), F32)
    weights = dict(
        g_mix=row(norm_mix_g[0]),
        w_in=w_in[0, :, :d_main].astype(BF16),
        w_tail=jnp.zeros((d_model, 128), BF16).at[:, :2 * N_HEADS].set(w_in[0, :, d_main:].astype(BF16)),
        dw_w=conv_dw_w[0].astype(F32), dw_b=row(conv_dw_b[0]), ln_g=row(conv_ln_g[0]), ln_b=row(conv_ln_b[0]),
        pw=conv_pw_w[0].astype(BF16),
        dn_w=dn_conv_w[0].astype(F32),
        a_log=lanes.at[0, N_HEADS:2 * N_HEADS].set(dn_a_log[0]),
        dt_bias=lanes.at[0, N_HEADS:2 * N_HEADS].set(dn_dt_bias[0]),
        dn_g=row(dn_norm_g[0]),
        w_out=w_out[0].astype(BF16),
        ple_g=row(ple_norm_g[0]), ple_gate=ple_gate_w[0].astype(BF16), ple_proj=ple_proj_w[0].astype(BF16),
        fin_g=row(final_norm_g),
    )
    wlist = [weights[n] for n in _WEIGHT_NAMES]
    wspecs = [_full_spec(a) for a in wlist]
    params = dict(vmem_limit_bytes=VMEM_LIMIT_BYTES)

    nt = seqlen // tile
    upad = -(-(kw - 1) // 8) * 8
    qpad = -(-(ks - 1) // 8) * 8
    assert nt > 1
    n_tiles = bsz * nt
    front = lambda s: (jnp.minimum(s, n_tiles - 1) // nt, jnp.minimum(s, n_tiles - 1) % nt)
    back = lambda s: (jnp.maximum(s - 2, 0) // nt, jnp.maximum(s - 2, 0) % nt)
    stage = lambda cols: pltpu.VMEM((tile, cols), F32)
    slabs = lambda cols, rows: pltpu.VMEM((cols // LANES, rows, LANES), F32)
    y_p, nconv_p, ndn_p, ns_p = pl.pallas_call(
        functools.partial(_prompt_kernel, tile=tile, chunk=chunk, nt=nt),
        grid=(n_tiles + 2,),
        in_specs=[pl.BlockSpec((1, tile, d_model), lambda s: (*front(s), 0)),
                  pl.BlockSpec((1, tile, d_model), lambda s: (*back(s), 0)),
                  pl.BlockSpec((1, tile, p_prompt.shape[-1]), lambda s: (*back(s), 0))] + wspecs,
        out_specs=[pl.BlockSpec((1, tile, d_model), lambda s: (*back(s), 0)),
                   pl.BlockSpec((1, kw - 1, d_conv), lambda s: (front(s)[0], 0, 0)),
                   pl.BlockSpec((1, ks - 1, 3 * d_dn), lambda s: (front(s)[0], 0, 0)),
                   pl.BlockSpec((1, N_HEADS, HEAD_DIM, HEAD_DIM), lambda s: (back(s)[0], 0, 0, 0))],
        out_shape=[jax.ShapeDtypeStruct((bsz, seqlen, d_model), F32),
                   jax.ShapeDtypeStruct((bsz, kw - 1, d_conv), F32),
                   jax.ShapeDtypeStruct((bsz, ks - 1, 3 * d_dn), F32),
                   jax.ShapeDtypeStruct((bsz, N_HEADS, HEAD_DIM, HEAD_DIM), F32)],
        scratch_shapes=[slabs(d_conv, upad + tile),
                        slabs(3 * d_dn, qpad + tile),
                        slabs(d_conv, tile),
                        pltpu.VMEM((N_HEADS, HEAD_DIM, HEAD_DIM), F32),
                        slabs(3 * d_dn, tile),
                        stage(d_conv), stage(d_dn), stage(d_dn), stage(d_dn), stage(d_dn), stage(128),
                        stage(d_conv), stage(d_dn), stage(d_dn), stage(d_dn), stage(d_dn), stage(d_dn),
                        pltpu.VMEM((d_dn, tile), F32), stage(128)],
        compiler_params=pltpu.CompilerParams(dimension_semantics=("arbitrary",), **params),
        name="prompt_layer",
    )(x_prompt, x_prompt, p_prompt[0], *wlist)

    seqs = SAMPLE_SEQS
    rows = seqs * dec_l
    xs = x_sample.reshape(dec_b * dec_l, d_model)
    ps = p_sample[0].reshape(dec_b * dec_l, -1)
    y_s, nconv_s, ndn_s, ns_s = pl.pallas_call(
        functools.partial(_sample_kernel, seqs=seqs, steps=dec_l),
        grid=(dec_b // seqs,),
        in_specs=[pl.BlockSpec((rows, d_model), lambda i: (i, 0)),
                  pl.BlockSpec((rows, ps.shape[-1]), lambda i: (i, 0)),
                  pl.BlockSpec((seqs, kw - 1, d_conv), lambda i: (i, 0, 0)),
                  pl.BlockSpec((seqs, ks - 1, 3 * d_dn), lambda i: (i, 0, 0)),
                  pl.BlockSpec((seqs, N_HEADS, HEAD_DIM, HEAD_DIM), lambda i: (i, 0, 0, 0))] + wspecs,
        out_specs=[pl.BlockSpec((rows, d_model), lambda i: (i, 0)),
                   pl.BlockSpec((seqs, kw - 1, d_conv), lambda i: (i, 0, 0)),
                   pl.BlockSpec((seqs, ks - 1, 3 * d_dn), lambda i: (i, 0, 0)),
                   pl.BlockSpec((seqs, N_HEADS, HEAD_DIM, HEAD_DIM), lambda i: (i, 0, 0, 0))],
        out_shape=[jax.ShapeDtypeStruct((dec_b * dec_l, d_model), F32),
                   jax.ShapeDtypeStruct((dec_b, kw - 1, d_conv), F32),
                   jax.ShapeDtypeStruct((dec_b, ks - 1, 3 * d_dn), F32),
                   jax.ShapeDtypeStruct((dec_b, N_HEADS, HEAD_DIM, HEAD_DIM), F32)],
        scratch_shapes=[pltpu.VMEM((seqs, upad + dec_l, d_conv), F32),
                        pltpu.VMEM((seqs, qpad + dec_l, 3 * d_dn), F32)],
        compiler_params=pltpu.CompilerParams(dimension_semantics=("arbitrary",), **params),
        name="sample_layer",
    )(xs, ps, state_conv[0], state_dn_conv[0], state_dn_S[0], *wlist)

    return (y_p, y_s.reshape(dec_b, dec_l, d_model), nconv_p[None], ndn_p[None], ns_p[None],
            nconv_s[None], ndn_s[None], ns_s[None])
```

```python
import functools

import jax
import jax.numpy as jnp
from jax import lax
from jax.experimental import pallas as pl
from jax.experimental.pallas import tpu as pltpu

EPS = 1e-6
N_HEADS = 4
HEAD_DIM = 128
LANES = 128
SUB = 16
PROMPT_TILE = 256
PROMPT_CHUNK = 64
SAMPLE_SEQS = 16
CONV_ROWS = 32
VMEM_LIMIT_BYTES = 56 * 1024 * 1024

F32 = jnp.float32
BF16 = jnp.bfloat16


def _run(gen):
    try:
        while True:
            next(gen)
    except StopIteration as stop:
        return stop.value


def _interleave(gens, weights):
    n = len(gens)
    done, alive, out = [0] * n, [True] * n, [None] * n
    while any(alive):
        k = min((i for i in range(n) if alive[i]), key=lambda i: (done[i] + 1) / weights[i])
        try:
            next(gens[k])
            done[k] += 1
        except StopIteration as stop:
            out[k], alive[k] = stop.value, False
    return out


def _mm(a, b):
    return jnp.dot(a.astype(BF16), b.astype(BF16), preferred_element_type=F32)


def _mm_nt(a, b):
    return lax.dot_general(a.astype(BF16), b.astype(BF16), (((1,), (1,)), ((), ())),
                           preferred_element_type=F32)


def _mmb(a, b_bf16):
    if a.shape[0] % 16:
        return jnp.dot(a.astype(F32), b_bf16.astype(F32), preferred_element_type=F32)
    return jnp.dot(a.astype(BF16), b_bf16, preferred_element_type=F32)


def _rms(x, g):
    return x * lax.rsqrt(jnp.mean(x * x, axis=-1, keepdims=True) + EPS) * g


def _silu(x):
    return x * jax.nn.sigmoid(x)


def _softplus(x):
    return jnp.maximum(x, 0.0) + jnp.log(1.0 + jnp.exp(-jnp.abs(x)))


def _head(a, h):
    return a[:, h * HEAD_DIM:(h + 1) * HEAD_DIM]


def _lanes(c):
    return slice(c * LANES, (c + 1) * LANES)


class _Projection:
    def __init__(self, x, w):
        self.w, self.d_conv, self.d_dn = w, w["dw_w"].shape[-1], N_HEADS * HEAD_DIM
        self.h = _rms(x, w["g_mix"][...]).astype(BF16)

    def _cols(self, lo, n):
        return jnp.dot(self.h, self.w["w_in"][:, lo:lo + n], preferred_element_type=F32)

    def glu(self):
        ab = self._cols(0, 2 * self.d_conv)
        return ab[:, :self.d_conv] * jax.nn.sigmoid(ab[:, self.d_conv:])

    def c_gate(self):
        return self._cols(2 * self.d_conv, self.d_conv)

    def qkv(self, part):
        return self._cols(3 * self.d_conv + part * self.d_dn, self.d_dn)

    def z(self):
        return self._cols(3 * self.d_conv + 3 * self.d_dn, self.d_dn)

    def beta_g(self):
        w = self.w
        tail = jnp.dot(self.h, w["w_tail"][...], preferred_element_type=F32)
        lane = lax.broadcasted_iota(jnp.int32, tail.shape, 1)
        beta = jax.nn.sigmoid(tail)
        g = -jnp.exp(w["a_log"][...]) * _softplus(tail + w["dt_bias"][...])
        return jnp.where(lane < N_HEADS, beta, jnp.where(lane < 2 * N_HEADS, g, 0.0))


def _conv_branch_tail(c, c_gate, w):
    c = c + w["dw_b"][...]
    cc = c - jnp.mean(c, axis=-1, keepdims=True)
    c = cc * lax.rsqrt(jnp.mean(cc * cc, axis=-1, keepdims=True) + EPS) * w["ln_g"][...] + w["ln_b"][...]
    c = _silu(c)
    return _mm(c, w["pw"][...]) * _silu(c_gate)


def _qkv_heads(qkv_c):
    d_dn = N_HEADS * HEAD_DIM
    qs, ks, vs = [], [], []
    for h in range(N_HEADS):
        q = _silu(_head(qkv_c, h))
        k = _silu(_head(qkv_c[:, d_dn:2 * d_dn], h))
        v = _silu(_head(qkv_c[:, 2 * d_dn:], h))
        q = q * (lax.rsqrt(jnp.sum(q * q, axis=-1, keepdims=True) + EPS) * (HEAD_DIM ** -0.5))
        k = k * lax.rsqrt(jnp.sum(k * k, axis=-1, keepdims=True) + EPS)
        qs.append(q); ks.append(k); vs.append(v)
        yield
    return qs, ks, vs


def _chunk_masks(rows, chunk, sub):
    ri = lax.broadcasted_iota(jnp.int32, (rows, rows), 0)
    ci = lax.broadcasted_iota(jnp.int32, (rows, rows), 1)
    same = (ri // chunk) == (ci // chunk)
    incl = same & (ci <= ri)
    strict = same & (ci < ri)
    offdiag = (ri // sub) != (ci // sub)
    return incl, strict, offdiag


def _tri_inverse(a_list, offdiag, chunk, sub):
    rows = a_list[0].shape[0]
    nc = rows // chunk
    hs = range(len(a_list))
    pi = lax.broadcasted_iota(jnp.int32, (chunk, rows), 0)
    pl_ = lax.broadcasted_iota(jnp.int32, (chunk, rows), 1)
    lane_in, lane_blk = pl_ % chunk, pl_ // chunk
    eye_pan = (lane_in == pi).astype(F32)
    diag_pan = (pi // sub) == (lane_in // sub)
    packed = chunk % 16 == 0
    blk_masks = [(lane_blk == c).astype(BF16 if packed else F32) for c in range(nc)]

    def fold(full):
        out = full[0:chunk]
        for c in range(1, nc):
            out = out + full[c * chunk:(c + 1) * chunk]
        return out

    def expand(pan):
        src = pan.astype(BF16) if packed else pan
        return jnp.concatenate([src * blk_masks[c] for c in range(nc)], axis=0).astype(BF16)

    a_pan = [fold(a_list[h]) for h in hs]
    d_pan = [jnp.where(diag_pan, a_pan[h], 0.0) for h in hs]
    p_pan = [eye_pan - d_pan[h] for h in hs]
    pw_pan, span = d_pan, 1
    pw_full = [expand(d_pan[h]) for h in hs]
    while 2 * span < sub:
        pw_pan = [_mmb(pw_pan[h], pw_full[h]) for h in hs]
        yield
        pw_full = [expand(pw_pan[h]) for h in hs]
        p_pan = [p_pan[h] + _mmb(p_pan[h], pw_full[h]) for h in hs]
        span *= 2
    nblk = chunk // sub
    if nblk == 1:
        return [expand(p_pan[h]) for h in hs]
    yield
    p_full = [expand(p_pan[h]) for h in hs]
    n_full = [jnp.where(offdiag, a_list[h], 0.0).astype(BF16) for h in hs]
    b_pan = [_mmb(p_pan[h], n_full[h]) for h in hs]
    yield
    t_pan = [eye_pan - b_pan[h] for h in hs]
    bp_pan, span = b_pan, 1
    bp_full = [expand(b_pan[h]) for h in hs] if nblk > 2 else None
    while 2 * span < nblk:
        bp_pan = [_mmb(bp_pan[h], bp_full[h]) for h in hs]
        yield
        bp_full = [expand(bp_pan[h]) for h in hs]
        t_pan = [t_pan[h] + _mmb(t_pan[h], bp_full[h]) for h in hs]
        yield
        span *= 2
    tinv = [expand(_mmb(t_pan[h], p_full[h])) for h in hs]
    yield
    return tinv


def _gate_scalars(bg, chunk):
    rows, lanes = bg.shape
    pos = lax.broadcasted_iota(jnp.int32, bg.shape, 0) % chunk
    gc, s = bg, 1
    while s < chunk:
        gc = gc + jnp.where(pos >= s, pltpu.roll(gc, s, 0), 0.0)
        s *= 2
    gtot = jnp.concatenate([jnp.broadcast_to(gc[e - 1:e, :], (chunk, lanes)) for e in range(chunk, rows + 1, chunk)],
                           axis=0)
    return gc, gc.T, gtot


def _chunk_local(qs, ks, vs, bg, chunk, sub):
    incl, strict, offdiag = _chunk_masks(bg.shape[0], chunk, sub)
    gc, gct, gtot = _gate_scalars(bg, chunk)
    hs = range(N_HEADS)
    beta = [bg[:, h:h + 1] for h in hs]
    gcc = [gc[:, N_HEADS + h:N_HEADS + h + 1] for h in hs]
    gl = [gtot[:, N_HEADS + h:N_HEADS + h + 1] for h in hs]
    yield
    decay = [jnp.where(incl, jnp.exp(jnp.where(incl, gcc[h] - gct[N_HEADS + h:N_HEADS + h + 1, :], 0.0)), 0.0)
             for h in hs]
    kb = [ks[h] * beta[h] for h in hs]
    yield
    a = [jnp.where(strict, _mm_nt(kb[h], ks[h]) * decay[h], 0.0) for h in hs]
    yield
    qk = [_mm_nt(qs[h], ks[h]) * decay[h] for h in hs]
    egc = [jnp.exp(gcc[h]) for h in hs]
    rhs = [jnp.concatenate([vs[h] * beta[h], kb[h] * egc[h]], axis=1).astype(BF16) for h in hs]
    qd = [qs[h] * egc[h] for h in hs]
    kd = [ks[h] * jnp.exp(gl[h] - gcc[h]) for h in hs]
    yield
    tinv = yield from _tri_inverse(a, offdiag, chunk, sub)
    y = [_mmb(tinv[h], rhs[h]) for h in hs]
    yield
    u = [y[h][:, :HEAD_DIM] for h in hs]
    wk = [y[h][:, HEAD_DIM:] for h in hs]
    return u, wk, qk, qd, kd, gl, gtot


def _finish(x, c_out, o_heads, z, p_emb, w):
    outs = []
    for h in range(N_HEADS):
        o = o_heads[h]
        o = o * lax.rsqrt(jnp.mean(o * o, axis=-1, keepdims=True) + EPS) * w["dn_g"][...]
        outs.append(o * _silu(_head(z, h)))
    mix_in = jnp.concatenate([c_out] + outs, axis=1)
    x = x + _mm(mix_in, w["w_out"][...])
    yield
    gate = jax.nn.sigmoid(_mm(_rms(x, w["ple_g"][...]), w["ple_gate"][...]))
    yield
    x = x + gate * _mm(p_emb, w["ple_proj"][...])
    return _rms(x, w["fin_g"][...])


_WEIGHT_NAMES = ("g_mix", "w_in", "w_tail", "dw_w", "dw_b", "ln_g", "ln_b", "pw", "dn_w", "a_log",
                 "dt_bias", "dn_g", "w_out", "ple_g", "ple_gate", "ple_proj", "fin_g")


def _prompt_kernel(*refs, tile, chunk, nt):
    x_ref, xb_ref, pb_ref = refs[0], refs[1], refs[2]
    nw = len(_WEIGHT_NAMES)
    w = dict(zip(_WEIGHT_NAMES, refs[3:3 + nw]))
    y_ref, nconv_ref, ndn_ref, ns_ref = refs[3 + nw:7 + nw]
    (ubuf, qbuf, cbuf, s_scr, qcv, st_c, st_q, st_k, st_v, st_z, st_bg,
     s2_c, s2_z, s2_u, s2_w, s2_qd, s2_qk, s2_kt, s2_gl) = refs[7 + nw:]
    s = pl.program_id(0)
    t_a = s % nt
    t_c = (s + nt - 2) % nt
    kw = w["dw_w"].shape[0]
    ks = w["dn_w"].shape[0]
    hist, qhist = kw - 1, ks - 1
    upad, qpad = ubuf.shape[1] - tile, qbuf.shape[1] - tile
    hs = range(N_HEADS)
    half = HEAD_DIM // 2

    @pl.when(s == 0)
    def _():
        for ref in (st_c, st_q, st_k, st_v, st_z, st_bg, s_scr, s2_c, s2_z, s2_u, s2_w, s2_qd, s2_qk, s2_kt, s2_gl):
            ref[...] = jnp.zeros(ref.shape, F32)

    @pl.when(t_a == 0)
    def _():
        ubuf[:, 0:upad, :] = jnp.zeros((ubuf.shape[0], upad, LANES), F32)
        qbuf[:, 0:qpad, :] = jnp.zeros((qbuf.shape[0], qpad, LANES), F32)

    c3, z3 = s2_c[...], s2_z[...]
    u3, w3, qd3, qk3, kt3, gl3 = s2_u[...], s2_w[...], s2_qd[...], s2_qk[...], s2_kt[...], s2_gl[...]
    c2, z2, bg2 = st_c[...], st_z[...], st_bg[...]
    q2, k2, v2 = st_q[...], st_k[...], st_v[...]

    def stage3():
        lane_c = lax.broadcasted_iota(jnp.int32, (chunk, HEAD_DIM), 1)
        lane_t = lax.broadcasted_iota(jnp.int32, (HEAD_DIM, HEAD_DIM), 1)
        uu, wk, qd, qk_fold = ([_head(a, h) for h in hs] for a in (u3, w3, qd3, qk3))
        kd_t = [kt3[h * HEAD_DIM:(h + 1) * HEAD_DIM, :] for h in hs]
        gl = [gl3[:, N_HEADS + h:N_HEADS + h + 1] for h in hs]
        zs = jnp.zeros((HEAD_DIM, HEAD_DIM), BF16)
        zv = jnp.zeros((chunk, HEAD_DIM), BF16)
        state = [jnp.where(t_c == 0, 0.0, s_scr[h]) for h in hs]
        o_rows = [[] for _ in hs]
        for n in range(tile // chunk):
            r = slice(n * chunk, (n + 1) * chunk)
            col, odd = divmod(n * chunk, HEAD_DIM)
            cs = slice(col * HEAD_DIM, (col + 1) * HEAD_DIM)
            pairs = range(0, N_HEADS, 2)
            m1 = {}
            for h0 in pairs:
                h1 = h0 + 1
                lhs1 = jnp.concatenate([jnp.concatenate([wk[h0][r], wk[h1][r]], axis=1),
                                        jnp.concatenate([qd[h0][r], qd[h1][r]], axis=1)], axis=0)
                s0, s1 = state[h0].astype(BF16), state[h1].astype(BF16)
                sbd = jnp.concatenate([jnp.concatenate([s0, zs], axis=1), jnp.concatenate([zs, s1], axis=1)], axis=0)
                m1[h0] = _mmb(lhs1, sbd)
            yield
            new_state = list(state)
            for h0 in pairs:
                h1 = h0 + 1
                v0 = uu[h0][r] - m1[h0][:chunk, :HEAD_DIM]
                v1 = uu[h1][r] - m1[h0][:chunk, HEAD_DIM:]
                vbd = jnp.concatenate([jnp.concatenate([v0.astype(BF16), zv], axis=1),
                                       jnp.concatenate([zv, v1.astype(BF16)], axis=1)], axis=0)
                k0, k1 = kd_t[h0][:, cs], kd_t[h1][:, cs]
                if odd:
                    kpair = jnp.where(lane_t < half, pltpu.roll(k0, half, 1), k1)
                else:
                    kpair = jnp.where(lane_t < half, k0, pltpu.roll(k1, half, 1))
                qpair = jnp.where(lane_c < half, qk_fold[h0][r], qk_fold[h1][r])
                m2 = _mmb(jnp.concatenate([qpair, kpair], axis=0), vbd)
                o_rows[h0].append(m1[h0][chunk:, :HEAD_DIM] + m2[:chunk, :HEAD_DIM])
                o_rows[h1].append(m1[h0][chunk:, HEAD_DIM:] + m2[:chunk, HEAD_DIM:])
                new_state[h0] = state[h0] * jnp.exp(gl[h0][n * chunk:n * chunk + 1, :]) + m2[chunk:, :HEAD_DIM]
                new_state[h1] = state[h1] * jnp.exp(gl[h1][n * chunk:n * chunk + 1, :]) + m2[chunk:, HEAD_DIM:]
            state = new_state
            yield
        for h in hs:
            s_scr[h] = state[h]
        o_heads = [jnp.concatenate(o_rows[h], axis=0) for h in hs]
        y_ref[0] = yield from _finish(xb_ref[0], c3, o_heads, z3, pb_ref[0], w)

    def stage2():
        uu2, wk2, qk2, qd2, kd2, _, gtot2 = yield from _chunk_local(
            [_head(q2, h) for h in hs], [_head(k2, h) for h in hs], [_head(v2, h) for h in hs], bg2, chunk, SUB)
        folds = []
        for h in hs:
            f = qk2[h][:, 0:HEAD_DIM]
            for c in range(1, tile // HEAD_DIM):
                f = f + qk2[h][:, c * HEAD_DIM:(c + 1) * HEAD_DIM]
            folds.append(f + pltpu.roll(f, half, 1))
        s2_u[...] = jnp.concatenate(uu2, axis=1)
        s2_w[...] = jnp.concatenate(wk2, axis=1)
        s2_qd[...] = jnp.concatenate(qd2, axis=1)
        s2_qk[...] = jnp.concatenate(folds, axis=1)
        yield
        s2_kt[...] = jnp.concatenate([kd2[h].T for h in hs], axis=0)
        s2_gl[...] = gtot2
        s2_c[...] = c2
        s2_z[...] = z2

    def stage1():
        proj = _Projection(x_ref[0], w)
        u = proj.glu()
        for c in range(ubuf.shape[0]):
            ubuf[c, upad:upad + tile, :] = u[:, _lanes(c)]
        yield
        for part in range(3):
            qkv_p = proj.qkv(part)
            for c in range(N_HEADS):
                qbuf[part * N_HEADS + c, qpad:qpad + tile, :] = qkv_p[:, _lanes(c)]
            yield
        c_gate = None
        for i, (r0, p) in enumerate((r0, p) for r0 in range(0, tile, 2 * CONV_ROWS) for p in range(2)):
            for c in range(ubuf.shape[0]):
                acc = jnp.zeros((CONV_ROWS, LANES), F32)
                for j in range(kw):
                    win = ubuf[c, pl.ds(upad - hist + r0 + p + j, CONV_ROWS, stride=2), :]
                    acc = acc + win * w["dw_w"][j:j + 1, _lanes(c)]
                cbuf[c, pl.ds(r0 + p, CONV_ROWS, stride=2), :] = acc
            if i == 0:
                c_gate = proj.c_gate()
            elif i == 2:
                st_z[...] = proj.z()
            elif i == 4:
                st_bg[...] = proj.beta_g()
            yield
        c_raw = jnp.concatenate([cbuf[c] for c in range(cbuf.shape[0])], axis=1)
        st_c[...] = _conv_branch_tail(c_raw, c_gate, w)
        yield
        half_tile = tile // 2
        for c in range(qbuf.shape[0]):
            for p in range(2):
                acc = jnp.zeros((half_tile, LANES), F32)
                for j in range(ks):
                    win = qbuf[c, pl.ds(qpad - qhist + p + j, half_tile, stride=2), :]
                    acc = acc + win * w["dn_w"][j:j + 1, _lanes(c)]
                qcv[c, pl.ds(p, half_tile, stride=2), :] = acc
            if c % 3 == 2:
                yield
        qkv_c = jnp.concatenate([qcv[c] for c in range(qcv.shape[0])], axis=1)
        qa, ka, va = yield from _qkv_heads(qkv_c)
        st_q[...] = jnp.concatenate(qa, axis=1)
        st_k[...] = jnp.concatenate(ka, axis=1)
        st_v[...] = jnp.concatenate(va, axis=1)
        for c in range(ubuf.shape[0]):
            ubuf[c, 0:upad, :] = ubuf[c, tile:tile + upad, :]
        for c in range(qbuf.shape[0]):
            qbuf[c, 0:qpad, :] = qbuf[c, tile:tile + qpad, :]

    _interleave([stage3(), stage2(), stage1()], [11, 14, 21])

    @pl.when((t_a == nt - 1) & (s < pl.num_programs(0) - 2))
    def _():
        nconv_ref[0] = jnp.concatenate([ubuf[c, upad + tile - hist:upad + tile, :] for c in range(ubuf.shape[0])],
                                       axis=1)
        ndn_ref[0] = jnp.concatenate([qbuf[c, qpad + tile - qhist:qpad + tile, :] for c in range(qbuf.shape[0])],
                                     axis=1)

    @pl.when((t_c == nt - 1) & (s > 1))
    def _():
        ns_ref[0] = s_scr[...]


def _sample_kernel(*refs, seqs, steps):
    x_ref, p_ref, sc_ref, sdn_ref, s_ref = refs[:5]
    nw = len(_WEIGHT_NAMES)
    w = dict(zip(_WEIGHT_NAMES, refs[5:5 + nw]))
    y_ref, nconv_ref, ndn_ref, ns_ref = refs[5 + nw:9 + nw]
    ustage, cstage, qstage, qcs = refs[9 + nw:]
    rows = seqs * steps
    kw = w["dw_w"].shape[0]
    ks = w["dn_w"].shape[0]
    hist, qhist = sc_ref.shape[0], sdn_ref.shape[0]
    by_time = lambda ref, c, t: ref[c, pl.ds(t, seqs, stride=steps), :]

    x = x_ref[...]
    proj = _Projection(x, w)
    u, c_gate, z, bg = proj.glu(), proj.c_gate(), proj.z(), proj.beta_g()

    n_uc = ustage.shape[0]
    for c in range(n_uc):
        ustage[c] = u[:, _lanes(c)]
    u_tm = [[by_time(ustage, c, t) for c in range(n_uc)] for t in range(steps)]
    for t in range(steps):
        for c in range(n_uc):
            acc = jnp.zeros((seqs, LANES), F32)
            for j in range(kw):
                i = t + j
                src = sc_ref[i, :, _lanes(c)] if i < hist else u_tm[i - hist][c]
                acc = acc + src * w["dw_w"][j:j + 1, _lanes(c)]
            cstage[c, pl.ds(t, seqs, stride=steps), :] = acc
    for i in range(hist):
        nconv_ref[i] = sc_ref[i + steps] if i + steps < hist else jnp.concatenate(u_tm[i + steps - hist], axis=1)
    c_out = _conv_branch_tail(jnp.concatenate([cstage[c] for c in range(n_uc)], axis=1), c_gate, w)

    n_qc = qstage.shape[0]
    for part in range(3):
        qkv_p = proj.qkv(part)
        for c in range(N_HEADS):
            qstage[part * N_HEADS + c] = qkv_p[:, _lanes(c)]
    q_tm = [[by_time(qstage, c, t) for c in range(n_qc)] for t in range(steps)]
    for t in range(steps):
        for c in range(n_qc):
            acc = jnp.zeros((seqs, LANES), F32)
            for j in range(ks):
                i = t + j
                src = sdn_ref[i, :, _lanes(c)] if i < qhist else q_tm[i - qhist][c]
                acc = acc + src * w["dn_w"][j:j + 1, _lanes(c)]
            qcs[c, pl.ds(t, seqs, stride=steps), :] = acc
    for i in range(qhist):
        ndn_ref[i] = jnp.concatenate(q_tm[steps - qhist + i], axis=1)
    qs, kss, vs = _run(_qkv_heads(jnp.concatenate([qcs[c] for c in range(n_qc)], axis=1)))

    seq_of_col = lax.broadcasted_iota(jnp.int32, (seqs, 1, rows), 2) // steps
    seq_id = lax.broadcasted_iota(jnp.int32, (seqs, 1, rows), 0)
    col_mask = (seq_of_col == seq_id).astype(F32)
    uu, wk, qk, qd, kd, gl, _ = _run(_chunk_local(qs, kss, vs, bg, steps, steps))
    o_heads = []
    for h in range(N_HEADS):
        s_old = s_ref[:, h]
        lhs = jnp.concatenate([wk[h].reshape(seqs, steps, HEAD_DIM), qd[h].reshape(seqs, steps, HEAD_DIM)], axis=1)
        m1 = lax.dot_general(lhs.astype(BF16), s_old.astype(BF16), (((2,), (1,)), ((0,), (0,))),
                             preferred_element_type=F32)
        v_new = uu[h] - m1[:, :steps, :].reshape(rows, HEAD_DIM)
        o_heads.append(m1[:, steps:, :].reshape(rows, HEAD_DIM) + _mm(qk[h], v_new))
        kd_rows = (kd[h].T[None, :, :] * col_mask).reshape(seqs * HEAD_DIM, rows)
        ds = _mm(kd_rows, v_new).reshape(seqs, HEAD_DIM, HEAD_DIM)
        gl_seq = jnp.exp(gl[h].reshape(seqs, steps, 1)[:, 0:1, :])
        ns_ref[:, h] = s_old * gl_seq + ds

    y_ref[...] = _run(_finish(x, c_out, o_heads, z, p_ref[...], w))


def _full_spec(a):
    nd = a.ndim
    return pl.BlockSpec(a.shape, lambda *_: (0,) * nd)


def kernel(x_prompt, x_sample, state_conv, state_dn_conv, state_dn_S, p_prompt, p_sample, norm_mix_g, w_in, conv_dw_w, conv_dw_b, conv_ln_g, conv_ln_b, conv_pw_w, dn_conv_w, dn_a_log, dn_dt_bias, dn_norm_g, w_out, ple_norm_g, ple_gate_w, ple_proj_w, final_norm_g):
    depth = w_in.shape[0]
    assert depth == 1, "single trunk layer"
    bsz, seqlen, d_model = x_prompt.shape
    dec_b, dec_l, _ = x_sample.shape
    d_conv = conv_dw_w.shape[-1]
    d_dn = N_HEADS * HEAD_DIM
    kw, ks = conv_dw_w.shape[1], dn_conv_w.shape[1]
    d_main = 3 * d_conv + 4 * d_dn
    assert w_in.shape[-1] == d_main + 2 * N_HEADS
    assert dn_conv_w.shape[-1] == 3 * d_dn and dn_norm_g.shape[-1] == HEAD_DIM
    tile, chunk = min(PROMPT_TILE, seqlen), min(PROMPT_CHUNK, seqlen)
    assert seqlen % tile == 0 and tile % chunk == 0 and chunk % SUB == 0 and tile % (2 * CONV_ROWS) == 0
    assert tile >= 32 and dec_b % SAMPLE_SEQS == 0 and SAMPLE_SEQS % 8 == 0 and dec_l == 8
    assert 2 * chunk == HEAD_DIM and tile % HEAD_DIM == 0

    row = lambda v: v.reshape(1, -1).astype(F32)
    lanes = jnp.zeros((1, 128), F32)
    weights = dict(
        g_mix=row(norm_mix_g[0]),
        w_in=w_in[0].astype(BF16),
        w_tail=jnp.zeros((d_model, 128), BF16).at[:, :2 * N_HEADS].set(w_in[0, :, d_main:].astype(BF16)),
        dw_w=conv_dw_w[0].astype(F32), dw_b=row(conv_dw_b[0]), ln_g=row(conv_ln_g[0]), ln_b=row(conv_ln_b[0]),
        pw=conv_pw_w[0].astype(BF16),
        dn_w=dn_conv_w[0].astype(F32),
        a_log=lanes.at[0, N_HEADS:2 * N_HEADS].set(dn_a_log[0]),
        dt_bias=lanes.at[0, N_HEADS:2 * N_HEADS].set(dn_dt_bias[0]),
        dn_g=row(dn_norm_g[0]),
        w_out=w_out[0].astype(BF16),
        ple_g=row(ple_norm_g[0]), ple_gate=ple_gate_w[0].astype(BF16), ple_proj=ple_proj_w[0].astype(BF16),
        fin_g=row(final_norm_g),
    )
    wlist = [weights[n] for n in _WEIGHT_NAMES]
    wspecs = [_full_spec(a) for a in wlist]
    params = dict(vmem_limit_bytes=VMEM_LIMIT_BYTES)

    nt = seqlen // tile
    upad = -(-(kw - 1) // 8) * 8
    qpad = -(-(ks - 1) // 8) * 8
    assert nt > 1
    n_tiles = bsz * nt
    front = lambda s: (jnp.minimum(s, n_tiles - 1) // nt, jnp.minimum(s, n_tiles - 1) % nt)
    back = lambda s: (jnp.maximum(s - 2, 0) // nt, jnp.maximum(s - 2, 0) % nt)
    stage = lambda cols: pltpu.VMEM((tile, cols), F32)
    slabs = lambda cols, rows: pltpu.VMEM((cols // LANES, rows, LANES), F32)
    y_p, nconv_p, ndn_p, ns_p = pl.pallas_call(
        functools.partial(_prompt_kernel, tile=tile, chunk=chunk, nt=nt),
        grid=(n_tiles + 2,),
        in_specs=[pl.BlockSpec((1, tile, d_model), lambda s: (*front(s), 0)),
                  pl.BlockSpec((1, tile, d_model), lambda s: (*back(s), 0)),
                  pl.BlockSpec((1, tile, p_prompt.shape[-1]), lambda s: (*back(s), 0))] + wspecs,
        out_specs=[pl.BlockSpec((1, tile, d_model), lambda s: (*back(s), 0)),
                   pl.BlockSpec((1, kw - 1, d_conv), lambda s: (front(s)[0], 0, 0)),
                   pl.BlockSpec((1, ks - 1, 3 * d_dn), lambda s: (front(s)[0], 0, 0)),
                   pl.BlockSpec((1, N_HEADS, HEAD_DIM, HEAD_DIM), lambda s: (back(s)[0], 0, 0, 0))],
        out_shape=[jax.ShapeDtypeStruct((bsz, seqlen, d_model), F32),
                   jax.ShapeDtypeStruct((bsz, kw - 1, d_conv), F32),
                   jax.ShapeDtypeStruct((bsz, ks - 1, 3 * d_dn), F32),
                   jax.ShapeDtypeStruct((bsz, N_HEADS, HEAD_DIM, HEAD_DIM), F32)],
        scratch_shapes=[slabs(d_conv, upad + tile),
                        slabs(3 * d_dn, qpad + tile),
                        slabs(d_conv, tile),
                        pltpu.VMEM((N_HEADS, HEAD_DIM, HEAD_DIM), F32),
                        slabs(3 * d_dn, tile),
                        stage(d_conv), stage(d_dn), stage(d_dn), stage(d_dn), stage(d_dn), stage(128),
                        stage(d_conv), stage(d_dn), stage(d_dn), stage(d_dn), stage(d_dn), stage(d_dn),
                        pltpu.VMEM((d_dn, tile), F32), stage(128)],
        compiler_params=pltpu.CompilerParams(dimension_semantics=("arbitrary",), **params),
        name="prompt_layer",
    )(x_prompt, x_prompt, p_prompt[0], *wlist)

    seqs = SAMPLE_SEQS
    rows = seqs * dec_l
    xs = x_sample.reshape(dec_b * dec_l, d_model)
    ps = p_sample[0].reshape(dec_b * dec_l, -1)
    tm = lambda a: jnp.transpose(a, (0, 2, 1, 3))
    conv_spec = pl.BlockSpec((None, kw - 1, seqs, d_conv), lambda i: (0, 0, i, 0))
    dn_spec = pl.BlockSpec((None, ks - 1, seqs, 3 * d_dn), lambda i: (0, 0, i, 0))
    state_spec = pl.BlockSpec((None, seqs, N_HEADS, HEAD_DIM, HEAD_DIM), lambda i: (0, i, 0, 0, 0))
    y_s, nconv_s, ndn_s, ns_s = pl.pallas_call(
        functools.partial(_sample_kernel, seqs=seqs, steps=dec_l),
        grid=(dec_b // seqs,),
        in_specs=[pl.BlockSpec((rows, d_model), lambda i: (i, 0)),
                  pl.BlockSpec((rows, ps.shape[-1]), lambda i: (i, 0)),
                  conv_spec, dn_spec, state_spec] + wspecs,
        out_specs=[pl.BlockSpec((rows, d_model), lambda i: (i, 0)), conv_spec, dn_spec, state_spec],
        out_shape=[jax.ShapeDtypeStruct((dec_b * dec_l, d_model), F32),
                   jax.ShapeDtypeStruct((1, kw - 1, dec_b, d_conv), F32),
                   jax.ShapeDtypeStruct((1, ks - 1, dec_b, 3 * d_dn), F32),
                   jax.ShapeDtypeStruct((1, dec_b, N_HEADS, HEAD_DIM, HEAD_DIM), F32)],
        scratch_shapes=[slabs(d_conv, rows), slabs(d_conv, rows), slabs(3 * d_dn, rows), slabs(3 * d_dn, rows)],
        compiler_params=pltpu.CompilerParams(dimension_semantics=("arbitrary",), **params),
        name="sample_layer",
    )(xs, ps, tm(state_conv), tm(state_dn_conv), state_dn_S, *wlist)

    return (y_p, y_s.reshape(dec_b, dec_l, d_model), nconv_p[None], ndn_p[None], ns_p[None],
            tm(nconv_s), tm(ndn_s), ns_s)
```

```python
import functools

import jax
import jax.numpy as jnp
from jax import lax
from jax.experimental import pallas as pl
from jax.experimental.pallas import tpu as pltpu

EPS = 1e-6
N_HEADS = 4
HEAD_DIM = 128
LANES = 128
SUB = 16
PROMPT_TILE = 256
PROMPT_CHUNK = 64
SAMPLE_SEQS = 16
CONV_ROWS = 32
VMEM_LIMIT_BYTES = 56 * 1024 * 1024

F32 = jnp.float32
BF16 = jnp.bfloat16


def _run(gen):
    try:
        while True:
            next(gen)
    except StopIteration as stop:
        return stop.value


def _interleave(gens, weights):
    n = len(gens)
    done, alive, out = [0] * n, [True] * n, [None] * n
    while any(alive):
        k = min((i for i in range(n) if alive[i]), key=lambda i: (done[i] + 1) / weights[i])
        try:
            next(gens[k])
            done[k] += 1
        except StopIteration as stop:
            out[k], alive[k] = stop.value, False
    return out


def _mm(a, b):
    return jnp.dot(a.astype(BF16), b.astype(BF16), preferred_element_type=F32)


def _mm_nt(a, b):
    return lax.dot_general(a.astype(BF16), b.astype(BF16), (((1,), (1,)), ((), ())),
                           preferred_element_type=F32)


def _mmb(a, b_bf16):
    if a.shape[0] % 16:
        return jnp.dot(a.astype(F32), b_bf16.astype(F32), preferred_element_type=F32)
    return jnp.dot(a.astype(BF16), b_bf16, preferred_element_type=F32)


def _rms(x, g):
    return x * lax.rsqrt(jnp.mean(x * x, axis=-1, keepdims=True) + EPS) * g


def _silu(x):
    return x * jax.nn.sigmoid(x)


def _softplus(x):
    return jnp.maximum(x, 0.0) + jnp.log(1.0 + jnp.exp(-jnp.abs(x)))


def _head(a, h):
    return a[:, h * HEAD_DIM:(h + 1) * HEAD_DIM]


def _lanes(c):
    return slice(c * LANES, (c + 1) * LANES)


class _Projection:
    def __init__(self, x, w):
        self.w, self.d_conv, self.d_dn = w, w["dw_w"].shape[-1], N_HEADS * HEAD_DIM
        self.h = _rms(x, w["g_mix"][...]).astype(BF16)

    def _cols(self, lo, n):
        return jnp.dot(self.h, self.w["w_in"][:, lo:lo + n], preferred_element_type=F32)

    def glu(self):
        ab = self._cols(0, 2 * self.d_conv)
        return ab[:, :self.d_conv] * jax.nn.sigmoid(ab[:, self.d_conv:])

    def c_gate(self):
        return self._cols(2 * self.d_conv, self.d_conv)

    def qkv(self, part):
        return self._cols(3 * self.d_conv + part * self.d_dn, self.d_dn)

    def z(self):
        return self._cols(3 * self.d_conv + 3 * self.d_dn, self.d_dn)

    def beta_g(self):
        w = self.w
        tail = jnp.dot(self.h, w["w_tail"][...], preferred_element_type=F32)
        lane = lax.broadcasted_iota(jnp.int32, tail.shape, 1)
        beta = jax.nn.sigmoid(tail)
        g = -jnp.exp(w["a_log"][...]) * _softplus(tail + w["dt_bias"][...])
        return jnp.where(lane < N_HEADS, beta, jnp.where(lane < 2 * N_HEADS, g, 0.0))


def _conv_branch_tail(c, c_gate, w):
    c = c + w["dw_b"][...]
    cc = c - jnp.mean(c, axis=-1, keepdims=True)
    c = cc * lax.rsqrt(jnp.mean(cc * cc, axis=-1, keepdims=True) + EPS) * w["ln_g"][...] + w["ln_b"][...]
    c = _silu(c)
    return _mm(c, w["pw"][...]) * _silu(c_gate)


def _qkv_heads(qkv_c):
    d_dn = N_HEADS * HEAD_DIM
    qs, ks, vs = [], [], []
    for h in range(N_HEADS):
        q = _silu(_head(qkv_c, h))
        k = _silu(_head(qkv_c[:, d_dn:2 * d_dn], h))
        v = _silu(_head(qkv_c[:, 2 * d_dn:], h))
        q = q * (lax.rsqrt(jnp.sum(q * q, axis=-1, keepdims=True) + EPS) * (HEAD_DIM ** -0.5))
        k = k * lax.rsqrt(jnp.sum(k * k, axis=-1, keepdims=True) + EPS)
        qs.append(q); ks.append(k); vs.append(v)
        yield
    return qs, ks, vs


def _chunk_masks(rows, chunk, sub):
    ri = lax.broadcasted_iota(jnp.int32, (rows, rows), 0)
    ci = lax.broadcasted_iota(jnp.int32, (rows, rows), 1)
    same = (ri // chunk) == (ci // chunk)
    incl = same & (ci <= ri)
    strict = same & (ci < ri)
    offdiag = (ri // sub) != (ci // sub)
    return incl, strict, offdiag


def _tri_inverse(a_list, offdiag, chunk, sub):
    rows = a_list[0].shape[0]
    nc = rows // chunk
    hs = range(len(a_list))
    pi = lax.broadcasted_iota(jnp.int32, (chunk, rows), 0)
    pl_ = lax.broadcasted_iota(jnp.int32, (chunk, rows), 1)
    lane_in, lane_blk = pl_ % chunk, pl_ // chunk
    eye_pan = (lane_in == pi).astype(F32)
    diag_pan = (pi // sub) == (lane_in // sub)
    packed = chunk % 16 == 0
    blk_masks = [(lane_blk == c).astype(BF16 if packed else F32) for c in range(nc)]

    def fold(full):
        out = full[0:chunk]
        for c in range(1, nc):
            out = out + full[c * chunk:(c + 1) * chunk]
        return out

    def expand(pan):
        src = pan.astype(BF16) if packed else pan
        return jnp.concatenate([src * blk_masks[c] for c in range(nc)], axis=0).astype(BF16)

    a_pan = [fold(a_list[h]) for h in hs]
    d_pan = [jnp.where(diag_pan, a_pan[h], 0.0) for h in hs]
    p_pan = [eye_pan - d_pan[h] for h in hs]
    pw_pan, span = d_pan, 1
    pw_full = [expand(d_pan[h]) for h in hs]
    while 2 * span < sub:
        pw_pan = [_mmb(pw_pan[h], pw_full[h]) for h in hs]
        yield
        pw_full = [expand(pw_pan[h]) for h in hs]
        p_pan = [p_pan[h] + _mmb(p_pan[h], pw_full[h]) for h in hs]
        span *= 2
    nblk = chunk // sub
    if nblk == 1:
        return [expand(p_pan[h]) for h in hs]
    yield
    p_full = [expand(p_pan[h]) for h in hs]
    n_full = [jnp.where(offdiag, a_list[h], 0.0).astype(BF16) for h in hs]
    b_pan = [_mmb(p_pan[h], n_full[h]) for h in hs]
    yield
    t_pan = [eye_pan - b_pan[h] for h in hs]
    bp_pan, span = b_pan, 1
    bp_full = [expand(b_pan[h]) for h in hs] if nblk > 2 else None
    while 2 * span < nblk:
        bp_pan = [_mmb(bp_pan[h], bp_full[h]) for h in hs]
        yield
        bp_full = [expand(bp_pan[h]) for h in hs]
        t_pan = [t_pan[h] + _mmb(t_pan[h], bp_full[h]) for h in hs]
        yield
        span *= 2
    tinv = [expand(_mmb(t_pan[h], p_full[h])) for h in hs]
    yield
    return tinv


def _gate_scalars(bg, chunk):
    rows, lanes = bg.shape
    pos = lax.broadcasted_iota(jnp.int32, bg.shape, 0) % chunk
    gc, s = bg, 1
    while s < chunk:
        gc = gc + jnp.where(pos >= s, pltpu.roll(gc, s, 0), 0.0)
        s *= 2
    gtot = jnp.concatenate([jnp.broadcast_to(gc[e - 1:e, :], (chunk, lanes)) for e in range(chunk, rows + 1, chunk)],
                           axis=0)
    return gc, gc.T, gtot


def _chunk_local(qs, ks, vs, bg, chunk, sub):
    incl, strict, offdiag = _chunk_masks(bg.shape[0], chunk, sub)
    gc, gct, gtot = _gate_scalars(bg, chunk)
    hs = range(N_HEADS)
    beta = [bg[:, h:h + 1] for h in hs]
    gcc = [gc[:, N_HEADS + h:N_HEADS + h + 1] for h in hs]
    gl = [gtot[:, N_HEADS + h:N_HEADS + h + 1] for h in hs]
    yield
    decay = [jnp.where(incl, jnp.exp(jnp.where(incl, gcc[h] - gct[N_HEADS + h:N_HEADS + h + 1, :], 0.0)), 0.0)
             for h in hs]
    kb = [ks[h] * beta[h] for h in hs]
    yield
    a = [jnp.where(strict, _mm_nt(kb[h], ks[h]) * decay[h], 0.0) for h in hs]
    yield
    qk = [_mm_nt(qs[h], ks[h]) * decay[h] for h in hs]
    egc = [jnp.exp(gcc[h]) for h in hs]
    rhs = [jnp.concatenate([vs[h] * beta[h], kb[h] * egc[h]], axis=1).astype(BF16) for h in hs]
    qd = [qs[h] * egc[h] for h in hs]
    kd = [ks[h] * jnp.exp(gl[h] - gcc[h]) for h in hs]
    yield
    tinv = yield from _tri_inverse(a, offdiag, chunk, sub)
    y = [_mmb(tinv[h], rhs[h]) for h in hs]
    yield
    u = [y[h][:, :HEAD_DIM] for h in hs]
    wk = [y[h][:, HEAD_DIM:] for h in hs]
    return u, wk, qk, qd, kd, gl, gtot


def _finish(x, c_out, o_heads, z, p_emb, w):
    outs = []
    for h in range(N_HEADS):
        o = o_heads[h]
        o = o * lax.rsqrt(jnp.mean(o * o, axis=-1, keepdims=True) + EPS) * w["dn_g"][...]
        outs.append(o * _silu(_head(z, h)))
    mix_in = jnp.concatenate([c_out] + outs, axis=1)
    x = x + _mm(mix_in, w["w_out"][...])
    yield
    gate = jax.nn.sigmoid(_mm(_rms(x, w["ple_g"][...]), w["ple_gate"][...]))
    yield
    x = x + gate * _mm(p_emb, w["ple_proj"][...])
    return _rms(x, w["fin_g"][...])


_WEIGHT_NAMES = ("g_mix", "w_in", "w_tail", "dw_w", "dw_b", "ln_g", "ln_b", "pw", "dn_w", "a_log",
                 "dt_bias", "dn_g", "w_out", "ple_g", "ple_gate", "ple_proj", "fin_g")


def _prompt_kernel(*refs, tile, chunk, nt):
    x_ref, xb_ref, pb_ref = refs[0], refs[1], refs[2]
    nw = len(_WEIGHT_NAMES)
    w = dict(zip(_WEIGHT_NAMES, refs[3:3 + nw]))
    y_ref, nconv_ref, ndn_ref, ns_ref = refs[3 + nw:7 + nw]
    (ubuf, qbuf, cbuf, s_scr, qcv, cring, zring, st_q, st_k, st_v, st_bg,
     s2_u, s2_w, s2_qd, s2_qk, s2_kt, s2_gl) = refs[7 + nw:]
    s = pl.program_id(0)
    slot_new, slot_old = s % 3, (s + 1) % 3
    t_a = s % nt
    t_c = (s + nt - 2) % nt
    kw = w["dw_w"].shape[0]
    ks = w["dn_w"].shape[0]
    hist, qhist = kw - 1, ks - 1
    upad, qpad = ubuf.shape[1] - tile, qbuf.shape[1] - tile
    hs = range(N_HEADS)
    half = HEAD_DIM // 2

    @pl.when(s == 0)
    def _():
        for ref in (cring, zring, st_q, st_k, st_v, st_bg, s_scr, s2_u, s2_w, s2_qd, s2_qk, s2_kt, s2_gl):
            ref[...] = jnp.zeros(ref.shape, F32)

    @pl.when(t_a == 0)
    def _():
        ubuf[:, 0:upad, :] = jnp.zeros((ubuf.shape[0], upad, LANES), F32)
        qbuf[:, 0:qpad, :] = jnp.zeros((qbuf.shape[0], qpad, LANES), F32)

    def stage3():
        lane_c = lax.broadcasted_iota(jnp.int32, (chunk, HEAD_DIM), 1)
        lane_t = lax.broadcasted_iota(jnp.int32, (HEAD_DIM, HEAD_DIM), 1)
        hc = lambda h: slice(h * HEAD_DIM, (h + 1) * HEAD_DIM)
        zs = jnp.zeros((HEAD_DIM, HEAD_DIM), BF16)
        zv = jnp.zeros((chunk, HEAD_DIM), BF16)
        state = [jnp.where(t_c == 0, 0.0, s_scr[h]) for h in hs]
        o_rows = [[] for _ in hs]
        for n in range(tile // chunk):
            r = slice(n * chunk, (n + 1) * chunk)
            col, odd = divmod(n * chunk, HEAD_DIM)
            cs = slice(col * HEAD_DIM, (col + 1) * HEAD_DIM)
            pairs = range(0, N_HEADS, 2)
            pair_cols = lambda h0: slice(h0 * HEAD_DIM, (h0 + 2) * HEAD_DIM)
            m1 = {}
            for h0 in pairs:
                h1 = h0 + 1
                lhs1 = jnp.concatenate([s2_w[r, pair_cols(h0)], s2_qd[r, pair_cols(h0)]], axis=0)
                s0, s1 = state[h0].astype(BF16), state[h1].astype(BF16)
                sbd = jnp.concatenate([jnp.concatenate([s0, zs], axis=1), jnp.concatenate([zs, s1], axis=1)], axis=0)
                m1[h0] = _mmb(lhs1, sbd)
            yield
            new_state = list(state)
            decay_row = jnp.exp(s2_gl[n * chunk:n * chunk + 1, :])
            for h0 in pairs:
                h1 = h0 + 1
                v0 = s2_u[r, hc(h0)] - m1[h0][:chunk, :HEAD_DIM]
                v1 = s2_u[r, hc(h1)] - m1[h0][:chunk, HEAD_DIM:]
                vbd = jnp.concatenate([jnp.concatenate([v0.astype(BF16), zv], axis=1),
                                       jnp.concatenate([zv, v1.astype(BF16)], axis=1)], axis=0)
                k0, k1 = s2_kt[hc(h0), cs], s2_kt[hc(h1), cs]
                if odd:
                    kpair = jnp.where(lane_t < half, pltpu.roll(k0, half, 1), k1)
                else:
                    kpair = jnp.where(lane_t < half, k0, pltpu.roll(k1, half, 1))
                qpair = jnp.where(lane_c < half, s2_qk[r, hc(h0)], s2_qk[r, hc(h1)])
                m2 = _mmb(jnp.concatenate([qpair, kpair], axis=0), vbd)
                o_rows[h0].append(m1[h0][chunk:, :HEAD_DIM] + m2[:chunk, :HEAD_DIM])
                o_rows[h1].append(m1[h0][chunk:, HEAD_DIM:] + m2[:chunk, HEAD_DIM:])
                new_state[h0] = state[h0] * decay_row[:, N_HEADS + h0:N_HEADS + h0 + 1] + m2[chunk:, :HEAD_DIM]
                new_state[h1] = state[h1] * decay_row[:, N_HEADS + h1:N_HEADS + h1 + 1] + m2[chunk:, HEAD_DIM:]
            state = new_state
            yield
        for h in hs:
            s_scr[h] = state[h]
        return [jnp.concatenate(o_rows[h], axis=0) for h in hs]

    def stage3_out(o_heads):
        y_ref[0] = yield from _finish(xb_ref[0], cring[slot_old], o_heads, zring[slot_old], pb_ref[0], w)

    def stage2():
        heads_of = lambda ref: [ref[:, h * HEAD_DIM:(h + 1) * HEAD_DIM] for h in hs]
        uu2, wk2, qk2, qd2, kd2, _, gtot2 = yield from _chunk_local(
            heads_of(st_q), heads_of(st_k), heads_of(st_v), st_bg[...], chunk, SUB)
        folds = []
        for h in hs:
            f = qk2[h][:, 0:HEAD_DIM]
            for c in range(1, tile // HEAD_DIM):
                f = f + qk2[h][:, c * HEAD_DIM:(c + 1) * HEAD_DIM]
            folds.append(f + pltpu.roll(f, half, 1))
        yield
        return [(s2_u, jnp.concatenate(uu2, axis=1)), (s2_w, jnp.concatenate(wk2, axis=1)),
                (s2_qd, jnp.concatenate(qd2, axis=1)), (s2_qk, jnp.concatenate(folds, axis=1)),
                (s2_kt, jnp.concatenate([kd2[h].T for h in hs], axis=0)), (s2_gl, gtot2)]

    def stage1_project():
        proj = _Projection(x_ref[0], w)
        piece = 2 * LANES
        d_conv, d_dn = ubuf.shape[0] * LANES, N_HEADS * HEAD_DIM
        ab = []
        for lo in range(0, 2 * d_conv, piece):
            ab.append(proj._cols(lo, piece))
            yield
        n = len(ab) // 2
        for i in range(n):
            u = ab[i] * jax.nn.sigmoid(ab[n + i])
            for c in range(piece // LANES):
                ubuf[i * (piece // LANES) + c, upad:upad + tile, :] = u[:, _lanes(c)]
        gate = []
        for lo in range(2 * d_conv, 3 * d_conv, piece):
            gate.append(proj._cols(lo, piece))
            yield
        for i, lo in enumerate(range(3 * d_conv, 3 * d_conv + 3 * d_dn, piece)):
            cols = proj._cols(lo, piece)
            for c in range(piece // LANES):
                qbuf[i * (piece // LANES) + c, qpad:qpad + tile, :] = cols[:, _lanes(c)]
            yield
        for lo in range(0, d_dn, piece):
            zring[slot_new, :, lo:lo + piece] = proj._cols(3 * d_conv + 3 * d_dn + lo, piece)
            yield
        st_bg[...] = proj.beta_g()
        return jnp.concatenate(gate, axis=1)

    def stage1_conv(c_gate):
        starts = [(r0, p) for r0 in range(0, tile, 2 * CONV_ROWS) for p in range(2)]
        groups = [starts[:len(starts) // 2], starts[len(starts) // 2:]]
        for c in range(ubuf.shape[0]):
            for group in groups:
                accs = [jnp.zeros((CONV_ROWS, LANES), F32) for _ in group]
                for j in range(kw):
                    w_row = jnp.broadcast_to(w["dw_w"][j:j + 1, _lanes(c)], (CONV_ROWS, LANES))
                    for i, (r0, p) in enumerate(group):
                        win = ubuf[c, pl.ds(upad - hist + r0 + p + j, CONV_ROWS, stride=2), :]
                        accs[i] = accs[i] + win * w_row
                for (r0, p), acc in zip(group, accs):
                    cbuf[c, pl.ds(r0 + p, CONV_ROWS, stride=2), :] = acc
                yield
        c_raw = jnp.concatenate([cbuf[c] for c in range(cbuf.shape[0])], axis=1)
        cring[slot_new] = _conv_branch_tail(c_raw, c_gate, w)
        yield
        half_tile = tile // 2
        for c in range(qbuf.shape[0]):
            for p in range(2):
                acc = jnp.zeros((half_tile, LANES), F32)
                for j in range(ks):
                    win = qbuf[c, pl.ds(qpad - qhist + p + j, half_tile, stride=2), :]
                    acc = acc + win * w["dn_w"][j:j + 1, _lanes(c)]
                qcv[c, pl.ds(p, half_tile, stride=2), :] = acc
            if c % 3 == 2:
                yield
        qkv_c = jnp.concatenate([qcv[c] for c in range(qcv.shape[0])], axis=1)
        qa, ka, va = yield from _qkv_heads(qkv_c)
        st_q[...] = jnp.concatenate(qa, axis=1)
        st_k[...] = jnp.concatenate(ka, axis=1)
        st_v[...] = jnp.concatenate(va, axis=1)
        for c in range(ubuf.shape[0]):
            ubuf[c, 0:upad, :] = ubuf[c, tile:tile + upad, :]
        for c in range(qbuf.shape[0]):
            qbuf[c, 0:qpad, :] = qbuf[c, tile:tile + qpad, :]

    o_heads, delta_local, c_gate = _interleave([stage3(), stage2(), stage1_project()], [8, 14, 15])
    for ref, value in delta_local:
        ref[...] = value
    _interleave([stage3_out(o_heads), stage1_conv(c_gate)], [3, 17])

    @pl.when((t_a == nt - 1) & (s < pl.num_programs(0) - 2))
    def _():
        nconv_ref[0] = jnp.concatenate([ubuf[c, upad + tile - hist:upad + tile, :] for c in range(ubuf.shape[0])],
                                       axis=1)
        ndn_ref[0] = jnp.concatenate([qbuf[c, qpad + tile - qhist:qpad + tile, :] for c in range(qbuf.shape[0])],
                                     axis=1)

    @pl.when((t_c == nt - 1) & (s > 1))
    def _():
        ns_ref[0] = s_scr[...]


def _sample_kernel(*refs, seqs, steps):
    x_ref, p_ref, sc_ref, sdn_ref, s_ref = refs[:5]
    nw = len(_WEIGHT_NAMES)
    w = dict(zip(_WEIGHT_NAMES, refs[5:5 + nw]))
    y_ref, nconv_ref, ndn_ref, ns_ref = refs[5 + nw:9 + nw]
    ustage, cstage, qstage, qcs = refs[9 + nw:]
    rows = seqs * steps
    kw = w["dw_w"].shape[0]
    ks = w["dn_w"].shape[0]
    hist, qhist = sc_ref.shape[0], sdn_ref.shape[0]
    by_time = lambda ref, c, t: ref[c, pl.ds(t, seqs, stride=steps), :]

    x = x_ref[...]
    proj = _Projection(x, w)
    u, c_gate, z, bg = proj.glu(), proj.c_gate(), proj.z(), proj.beta_g()

    n_uc = ustage.shape[0]
    for c in range(n_uc):
        ustage[c] = u[:, _lanes(c)]
    u_tm = [[by_time(ustage, c, t) for c in range(n_uc)] for t in range(steps)]
    for t in range(steps):
        for c in range(n_uc):
            acc = jnp.zeros((seqs, LANES), F32)
            for j in range(kw):
                i = t + j
                src = sc_ref[i, :, _lanes(c)] if i < hist else u_tm[i - hist][c]
                acc = acc + src * w["dw_w"][j:j + 1, _lanes(c)]
            cstage[c, pl.ds(t, seqs, stride=steps), :] = acc
    for i in range(hist):
        nconv_ref[i] = sc_ref[i + steps] if i + steps < hist else jnp.concatenate(u_tm[i + steps - hist], axis=1)
    c_out = _conv_branch_tail(jnp.concatenate([cstage[c] for c in range(n_uc)], axis=1), c_gate, w)

    n_qc = qstage.shape[0]
    for part in range(3):
        qkv_p = proj.qkv(part)
        for c in range(N_HEADS):
            qstage[part * N_HEADS + c] = qkv_p[:, _lanes(c)]
    q_tm = [[by_time(qstage, c, t) for c in range(n_qc)] for t in range(steps)]
    for t in range(steps):
        for c in range(n_qc):
            acc = jnp.zeros((seqs, LANES), F32)
            for j in range(ks):
                i = t + j
                src = sdn_ref[i, :, _lanes(c)] if i < qhist else q_tm[i - qhist][c]
                acc = acc + src * w["dn_w"][j:j + 1, _lanes(c)]
            qcs[c, pl.ds(t, seqs, stride=steps), :] = acc
    for i in range(qhist):
        ndn_ref[i] = jnp.concatenate(q_tm[steps - qhist + i], axis=1)
    qs, kss, vs = _run(_qkv_heads(jnp.concatenate([qcs[c] for c in range(n_qc)], axis=1)))

    seq_of_col = lax.broadcasted_iota(jnp.int32, (seqs, 1, rows), 2) // steps
    seq_id = lax.broadcasted_iota(jnp.int32, (seqs, 1, rows), 0)
    col_mask = (seq_of_col == seq_id).astype(F32)
    uu, wk, qk, qd, kd, gl, _ = _run(_chunk_local(qs, kss, vs, bg, steps, steps))
    o_heads = []
    for h in range(N_HEADS):
        s_old = s_ref[:, h]
        lhs = jnp.concatenate([wk[h].reshape(seqs, steps, HEAD_DIM), qd[h].reshape(seqs, steps, HEAD_DIM)], axis=1)
        m1 = lax.dot_general(lhs.astype(BF16), s_old.astype(BF16), (((2,), (1,)), ((0,), (0,))),
                             preferred_element_type=F32)
        v_new = uu[h] - m1[:, :steps, :].reshape(rows, HEAD_DIM)
        o_heads.append(m1[:, steps:, :].reshape(rows, HEAD_DIM) + _mm(qk[h], v_new))
        kd_rows = (kd[h].T[None, :, :] * col_mask).reshape(seqs * HEAD_DIM, rows)
        ds = _mm(kd_rows, v_new).reshape(seqs, HEAD_DIM, HEAD_DIM)
        gl_seq = jnp.exp(gl[h].reshape(seqs, steps, 1)[:, 0:1, :])
        ns_ref[:, h] = s_old * gl_seq + ds

    y_ref[...] = _run(_finish(x, c_out, o_heads, z, p_ref[...], w))


def _full_spec(a):
    nd = a.ndim
    return pl.BlockSpec(a.shape, lambda *_: (0,) * nd)


def kernel(x_prompt, x_sample, state_conv, state_dn_conv, state_dn_S, p_prompt, p_sample, norm_mix_g, w_in, conv_dw_w, conv_dw_b, conv_ln_g, conv_ln_b, conv_pw_w, dn_conv_w, dn_a_log, dn_dt_bias, dn_norm_g, w_out, ple_norm_g, ple_gate_w, ple_proj_w, final_norm_g):
    depth = w_in.shape[0]
    assert depth == 1, "single trunk layer"
    bsz, seqlen, d_model = x_prompt.shape
    dec_b, dec_l, _ = x_sample.shape
    d_conv = conv_dw_w.shape[-1]
    d_dn = N_HEADS * HEAD_DIM
    kw, ks = conv_dw_w.shape[1], dn_conv_w.shape[1]
    d_main = 3 * d_conv + 4 * d_dn
    assert w_in.shape[-1] == d_main + 2 * N_HEADS
    assert dn_conv_w.shape[-1] == 3 * d_dn and dn_norm_g.shape[-1] == HEAD_DIM
    tile, chunk = min(PROMPT_TILE, seqlen), min(PROMPT_CHUNK, seqlen)
    assert seqlen % tile == 0 and tile % chunk == 0 and chunk % SUB == 0 and tile % (2 * CONV_ROWS) == 0
    assert tile >= 32 and dec_b % SAMPLE_SEQS == 0 and SAMPLE_SEQS % 8 == 0 and dec_l == 8
    assert 2 * chunk == HEAD_DIM and tile % HEAD_DIM == 0

    row = lambda v: v.reshape(1, -1).astype(F32)
    lanes = jnp.zeros((1, 128), F32)
    weights = dict(
        g_mix=row(norm_mix_g[0]),
        w_in=w_in[0].astype(BF16),
        w_tail=jnp.zeros((d_model, 128), BF16).at[:, :2 * N_HEADS].set(w_in[0, :, d_main:].astype(BF16)),
        dw_w=conv_dw_w[0].astype(F32), dw_b=row(conv_dw_b[0]), ln_g=row(conv_ln_g[0]), ln_b=row(conv_ln_b[0]),
        pw=conv_pw_w[0].astype(BF16),
        dn_w=dn_conv_w[0].astype(F32),
        a_log=lanes.at[0, N_HEADS:2 * N_HEADS].set(dn_a_log[0]),
        dt_bias=lanes.at[0, N_HEADS:2 * N_HEADS].set(dn_dt_bias[0]),
        dn_g=row(dn_norm_g[0]),
        w_out=w_out[0].astype(BF16),
        ple_g=row(ple_norm_g[0]), ple_gate=ple_gate_w[0].astype(BF16), ple_proj=ple_proj_w[0].astype(BF16),
        fin_g=row(final_norm_g),
    )
    wlist = [weights[n] for n in _WEIGHT_NAMES]
    wspecs = [_full_spec(a) for a in wlist]
    params = dict(vmem_limit_bytes=VMEM_LIMIT_BYTES)

    nt = seqlen // tile
    upad = -(-(kw - 1) // 8) * 8
    qpad = -(-(ks - 1) // 8) * 8
    assert nt > 1
    n_tiles = bsz * nt
    front = lambda s: (jnp.minimum(s, n_tiles - 1) // nt, jnp.minimum(s, n_tiles - 1) % nt)
    back = lambda s: (jnp.maximum(s - 2, 0) // nt, jnp.maximum(s - 2, 0) % nt)
    stage = lambda cols: pltpu.VMEM((tile, cols), F32)
    slabs = lambda cols, rows: pltpu.VMEM((cols // LANES, rows, LANES), F32)
    y_p, nconv_p, ndn_p, ns_p = pl.pallas_call(
        functools.partial(_prompt_kernel, tile=tile, chunk=chunk, nt=nt),
        grid=(n_tiles + 2,),
        in_specs=[pl.BlockSpec((1, tile, d_model), lambda s: (*front(s), 0)),
                  pl.BlockSpec((1, tile, d_model), lambda s: (*back(s), 0)),
                  pl.BlockSpec((1, tile, p_prompt.shape[-1]), lambda s: (*back(s), 0))] + wspecs,
        out_specs=[pl.BlockSpec((1, tile, d_model), lambda s: (*back(s), 0)),
                   pl.BlockSpec((1, kw - 1, d_conv), lambda s: (front(s)[0], 0, 0)),
                   pl.BlockSpec((1, ks - 1, 3 * d_dn), lambda s: (front(s)[0], 0, 0)),
                   pl.BlockSpec((1, N_HEADS, HEAD_DIM, HEAD_DIM), lambda s: (back(s)[0], 0, 0, 0))],
        out_shape=[jax.ShapeDtypeStruct((bsz, seqlen, d_model), F32),
                   jax.ShapeDtypeStruct((bsz, kw - 1, d_conv), F32),
                   jax.ShapeDtypeStruct((bsz, ks - 1, 3 * d_dn), F32),
                   jax.ShapeDtypeStruct((bsz, N_HEADS, HEAD_DIM, HEAD_DIM), F32)],
        scratch_shapes=[slabs(d_conv, upad + tile),
                        slabs(3 * d_dn, qpad + tile),
                        slabs(d_conv, tile),
                        pltpu.VMEM((N_HEADS, HEAD_DIM, HEAD_DIM), F32),
                        slabs(3 * d_dn, tile),
                        pltpu.VMEM((3, tile, d_conv), F32),
                        pltpu.VMEM((3, tile, d_dn), F32),
                        stage(d_dn), stage(d_dn), stage(d_dn), stage(128),
                        stage(d_dn), stage(d_dn), stage(d_dn), stage(d_dn),
                        pltpu.VMEM((d_dn, tile), F32), stage(128)],
        compiler_params=pltpu.CompilerParams(dimension_semantics=("arbitrary",), **params),
        name="prompt_layer",
    )(x_prompt, x_prompt, p_prompt[0], *wlist)

    seqs = SAMPLE_SEQS
    rows = seqs * dec_l
    xs = x_sample.reshape(dec_b * dec_l, d_model)
    ps = p_sample[0].reshape(dec_b * dec_l, -1)
    tm = lambda a: jnp.transpose(a, (0, 2, 1, 3))
    conv_spec = pl.BlockSpec((None, kw - 1, seqs, d_conv), lambda i: (0, 0, i, 0))
    dn_spec = pl.BlockSpec((None, ks - 1, seqs, 3 * d_dn), lambda i: (0, 0, i, 0))
    state_spec = pl.BlockSpec((None, seqs, N_HEADS, HEAD_DIM, HEAD_DIM), lambda i: (0, i, 0, 0, 0))
    y_s, nconv_s, ndn_s, ns_s = pl.pallas_call(
        functools.partial(_sample_kernel, seqs=seqs, steps=dec_l),
        grid=(dec_b // seqs,),
        in_specs=[pl.BlockSpec((rows, d_model), lambda i: (i, 0)),
                  pl.BlockSpec((rows, ps.shape[-1]), lambda i: (i, 0)),
                  conv_spec, dn_spec, state_spec] + wspecs,
        out_specs=[pl.BlockSpec((rows, d_model), lambda i: (i, 0)), conv_spec, dn_spec, state_spec],
        out_shape=[jax.ShapeDtypeStruct((dec_b * dec_l, d_model), F32),
                   jax.ShapeDtypeStruct((1, kw - 1, dec_b, d_conv), F32),
                   jax.ShapeDtypeStruct((1, ks - 1, dec_b, 3 * d_dn), F32),
                   jax.ShapeDtypeStruct((1, dec_b, N_HEADS, HEAD_DIM, HEAD_DIM), F32)],
        scratch_shapes=[slabs(d_conv, rows), slabs(d_conv, rows), slabs(3 * d_dn, rows), slabs(3 * d_dn, rows)],
        compiler_params=pltpu.CompilerParams(dimension_semantics=("arbitrary",), **params),
        name="sample_layer",
    )(xs, ps, tm(state_conv), tm(state_dn_conv), state_dn_S, *wlist)

    return (y_p, y_s.reshape(dec_b, dec_l, d_model), nconv_p[None], ndn_p[None], ns_p[None],
            tm(nconv_s), tm(ndn_s), ns_s)
```

```python
import functools

import jax
import jax.numpy as jnp
from jax import lax
from jax.experimental import pallas as pl
from jax.experimental.pallas import tpu as pltpu

EPS = 1e-6
N_HEADS = 4
HEAD_DIM = 128
LANES = 128
SUB = 16
PROMPT_TILE = 256
PROMPT_CHUNK = 64
SAMPLE_SEQS = 16
CONV_ROWS = 32
VMEM_LIMIT_BYTES = 56 * 1024 * 1024

F32 = jnp.float32
BF16 = jnp.bfloat16


def _run(gen):
    try:
        while True:
            next(gen)
    except StopIteration as stop:
        return stop.value


def _interleave(gens, weights):
    n = len(gens)
    done, alive, out = [0] * n, [True] * n, [None] * n
    while any(alive):
        k = min((i for i in range(n) if alive[i]), key=lambda i: (done[i] + 1) / weights[i])
        try:
            next(gens[k])
            done[k] += 1
        except StopIteration as stop:
            out[k], alive[k] = stop.value, False
    return out


def _mm(a, b):
    return jnp.dot(a.astype(BF16), b.astype(BF16), preferred_element_type=F32)


def _mm_nt(a, b):
    return lax.dot_general(a.astype(BF16), b.astype(BF16), (((1,), (1,)), ((), ())),
                           preferred_element_type=F32)


def _mmb(a, b_bf16):
    if a.shape[0] % 16:
        return jnp.dot(a.astype(F32), b_bf16.astype(F32), preferred_element_type=F32)
    return jnp.dot(a.astype(BF16), b_bf16, preferred_element_type=F32)


def _rms(x, g):
    return x * lax.rsqrt(jnp.mean(x * x, axis=-1, keepdims=True) + EPS) * g


def _silu(x):
    return x * jax.nn.sigmoid(x)


def _softplus(x):
    return jnp.maximum(x, 0.0) + jnp.log(1.0 + jnp.exp(-jnp.abs(x)))


def _head(a, h):
    return a[:, h * HEAD_DIM:(h + 1) * HEAD_DIM]


def _lanes(c):
    return slice(c * LANES, (c + 1) * LANES)


class _Projection:
    def __init__(self, x, w):
        self.w, self.d_conv, self.d_dn = w, w["dw_w"].shape[-1], N_HEADS * HEAD_DIM
        self.h = _rms(x, w["g_mix"][...]).astype(BF16)

    def _cols(self, lo, n):
        return jnp.dot(self.h, self.w["w_in"][:, lo:lo + n], preferred_element_type=F32)

    def glu(self):
        ab = self._cols(0, 2 * self.d_conv)
        return ab[:, :self.d_conv] * jax.nn.sigmoid(ab[:, self.d_conv:])

    def c_gate(self):
        return self._cols(2 * self.d_conv, self.d_conv)

    def qkv(self, part):
        return self._cols(3 * self.d_conv + part * self.d_dn, self.d_dn)

    def z(self):
        return self._cols(3 * self.d_conv + 3 * self.d_dn, self.d_dn)

    def beta_g(self):
        w = self.w
        tail = jnp.dot(self.h, w["w_tail"][...], preferred_element_type=F32)
        lane = lax.broadcasted_iota(jnp.int32, tail.shape, 1)
        beta = jax.nn.sigmoid(tail)
        g = -jnp.exp(w["a_log"][...]) * _softplus(tail + w["dt_bias"][...])
        return jnp.where(lane < N_HEADS, beta, jnp.where(lane < 2 * N_HEADS, g, 0.0))


def _conv_branch_tail(c, c_gate, w):
    c = c + w["dw_b"][...]
    cc = c - jnp.mean(c, axis=-1, keepdims=True)
    c = cc * lax.rsqrt(jnp.mean(cc * cc, axis=-1, keepdims=True) + EPS) * w["ln_g"][...] + w["ln_b"][...]
    c = _silu(c)
    return _mm(c, w["pw"][...]) * _silu(c_gate)


def _qkv_heads(qkv_c):
    d_dn = N_HEADS * HEAD_DIM
    qs, ks, vs = [], [], []
    for h in range(N_HEADS):
        q = _silu(_head(qkv_c, h))
        k = _silu(_head(qkv_c[:, d_dn:2 * d_dn], h))
        v = _silu(_head(qkv_c[:, 2 * d_dn:], h))
        q = q * (lax.rsqrt(jnp.sum(q * q, axis=-1, keepdims=True) + EPS) * (HEAD_DIM ** -0.5))
        k = k * lax.rsqrt(jnp.sum(k * k, axis=-1, keepdims=True) + EPS)
        qs.append(q); ks.append(k); vs.append(v)
        yield
    return qs, ks, vs


def _chunk_masks(rows, chunk, sub):
    ri = lax.broadcasted_iota(jnp.int32, (rows, rows), 0)
    ci = lax.broadcasted_iota(jnp.int32, (rows, rows), 1)
    same = (ri // chunk) == (ci // chunk)
    incl = same & (ci <= ri)
    strict = same & (ci < ri)
    offdiag = (ri // sub) != (ci // sub)
    return incl, strict, offdiag


def _tri_inverse(a_list, offdiag, chunk, sub):
    rows = a_list[0].shape[0]
    nc = rows // chunk
    hs = range(len(a_list))
    pi = lax.broadcasted_iota(jnp.int32, (chunk, rows), 0)
    pl_ = lax.broadcasted_iota(jnp.int32, (chunk, rows), 1)
    lane_in, lane_blk = pl_ % chunk, pl_ // chunk
    eye_pan = (lane_in == pi).astype(F32)
    diag_pan = (pi // sub) == (lane_in // sub)
    packed = chunk % 16 == 0
    blk_masks = [(lane_blk == c).astype(BF16 if packed else F32) for c in range(nc)]

    def fold(full):
        out = full[0:chunk]
        for c in range(1, nc):
            out = out + full[c * chunk:(c + 1) * chunk]
        return out

    def expand(pan):
        src = pan.astype(BF16) if packed else pan
        return jnp.concatenate([src * blk_masks[c] for c in range(nc)], axis=0).astype(BF16)

    a_pan = [fold(a_list[h]) for h in hs]
    d_pan = [jnp.where(diag_pan, a_pan[h], 0.0) for h in hs]
    p_pan = [eye_pan - d_pan[h] for h in hs]
    pw_pan, span = d_pan, 1
    pw_full = [expand(d_pan[h]) for h in hs]
    while 2 * span < sub:
        pw_pan = [_mmb(pw_pan[h], pw_full[h]) for h in hs]
        yield
        pw_full = [expand(pw_pan[h]) for h in hs]
        p_pan = [p_pan[h] + _mmb(p_pan[h], pw_full[h]) for h in hs]
        span *= 2
    nblk = chunk // sub
    if nblk == 1:
        return [expand(p_pan[h]) for h in hs]
    yield
    p_full = [expand(p_pan[h]) for h in hs]
    n_full = [jnp.where(offdiag, a_list[h], 0.0).astype(BF16) for h in hs]
    b_pan = [_mmb(p_pan[h], n_full[h]) for h in hs]
    yield
    t_pan = [eye_pan - b_pan[h] for h in hs]
    bp_pan, span = b_pan, 1
    bp_full = [expand(b_pan[h]) for h in hs] if nblk > 2 else None
    while 2 * span < nblk:
        bp_pan = [_mmb(bp_pan[h], bp_full[h]) for h in hs]
        yield
        bp_full = [expand(bp_pan[h]) for h in hs]
        t_pan = [t_pan[h] + _mmb(t_pan[h], bp_full[h]) for h in hs]
        yield
        span *= 2
    tinv = [expand(_mmb(t_pan[h], p_full[h])) for h in hs]
    yield
    return tinv


def _gate_scalars(bg, chunk):
    rows, lanes = bg.shape
    pos = lax.broadcasted_iota(jnp.int32, bg.shape, 0) % chunk
    gc, s = bg, 1
    while s < chunk:
        gc = gc + jnp.where(pos >= s, pltpu.roll(gc, s, 0), 0.0)
        s *= 2
    gtot = jnp.concatenate([jnp.broadcast_to(gc[e - 1:e, :], (chunk, lanes)) for e in range(chunk, rows + 1, chunk)],
                           axis=0)
    return gc, gc.T, gtot


def _chunk_local(qs, ks, vs, bg, chunk, sub):
    incl, strict, offdiag = _chunk_masks(bg.shape[0], chunk, sub)
    gc, gct, gtot = _gate_scalars(bg, chunk)
    hs = range(N_HEADS)
    beta = [bg[:, h:h + 1] for h in hs]
    gcc = [gc[:, N_HEADS + h:N_HEADS + h + 1] for h in hs]
    gl = [gtot[:, N_HEADS + h:N_HEADS + h + 1] for h in hs]
    yield
    decay = [jnp.where(incl, jnp.exp(jnp.where(incl, gcc[h] - gct[N_HEADS + h:N_HEADS + h + 1, :], 0.0)), 0.0)
             for h in hs]
    kb = [ks[h] * beta[h] for h in hs]
    yield
    a = [jnp.where(strict, _mm_nt(kb[h], ks[h]) * decay[h], 0.0) for h in hs]
    yield
    qk = [_mm_nt(qs[h], ks[h]) * decay[h] for h in hs]
    egc = [jnp.exp(gcc[h]) for h in hs]
    rhs = [jnp.concatenate([vs[h] * beta[h], kb[h] * egc[h]], axis=1).astype(BF16) for h in hs]
    qd = [qs[h] * egc[h] for h in hs]
    kd = [ks[h] * jnp.exp(gl[h] - gcc[h]) for h in hs]
    yield
    tinv = yield from _tri_inverse(a, offdiag, chunk, sub)
    y = [_mmb(tinv[h], rhs[h]) for h in hs]
    yield
    u = [y[h][:, :HEAD_DIM] for h in hs]
    wk = [y[h][:, HEAD_DIM:] for h in hs]
    return u, wk, qk, qd, kd, gl, gtot


def _finish(x, c_out, o_heads, z, p_emb, w):
    outs = []
    for h in range(N_HEADS):
        o = o_heads[h]
        o = o * lax.rsqrt(jnp.mean(o * o, axis=-1, keepdims=True) + EPS) * w["dn_g"][...]
        outs.append(o * _silu(_head(z, h)))
    mix_in = jnp.concatenate([c_out] + outs, axis=1)
    x = x + _mm(mix_in, w["w_out"][...])
    yield
    gate = jax.nn.sigmoid(_mm(_rms(x, w["ple_g"][...]), w["ple_gate"][...]))
    yield
    x = x + gate * _mm(p_emb, w["ple_proj"][...])
    return _rms(x, w["fin_g"][...])


_WEIGHT_NAMES = ("g_mix", "w_in", "w_tail", "dw_w", "dw_b", "ln_g", "ln_b", "pw", "dn_w", "a_log",
                 "dt_bias", "dn_g", "w_out", "ple_g", "ple_gate", "ple_proj", "fin_g")


def _prompt_kernel(*refs, tile, chunk, nt):
    x_ref, xb_ref, pb_ref = refs[0], refs[1], refs[2]
    nw = len(_WEIGHT_NAMES)
    w = dict(zip(_WEIGHT_NAMES, refs[3:3 + nw]))
    y_ref, nconv_ref, ndn_ref, ns_ref = refs[3 + nw:7 + nw]
    (ubuf, qbuf, cbuf, s_scr, qcv, cring, zring, st_q, st_k, st_v, st_bg,
     s2_u, s2_w, s2_qd, s2_qk, s2_kt, s2_gl) = refs[7 + nw:]
    s = pl.program_id(0)
    slot_new, slot_old = s % 3, (s + 1) % 3
    t_a = s % nt
    t_c = (s + nt - 2) % nt
    kw = w["dw_w"].shape[0]
    ks = w["dn_w"].shape[0]
    hist, qhist = kw - 1, ks - 1
    upad, qpad = ubuf.shape[1] - tile, qbuf.shape[1] - tile
    hs = range(N_HEADS)
    half = HEAD_DIM // 2

    @pl.when(s == 0)
    def _():
        for ref in (cring, zring, st_q, st_k, st_v, st_bg, s_scr, s2_u, s2_w, s2_qd, s2_qk, s2_kt, s2_gl):
            ref[...] = jnp.zeros(ref.shape, F32)

    @pl.when(t_a == 0)
    def _():
        ubuf[:, 0:upad, :] = jnp.zeros((ubuf.shape[0], upad, LANES), F32)
        qbuf[:, 0:qpad, :] = jnp.zeros((qbuf.shape[0], qpad, LANES), F32)

    def stage3():
        lane_c = lax.broadcasted_iota(jnp.int32, (chunk, HEAD_DIM), 1)
        lane_t = lax.broadcasted_iota(jnp.int32, (HEAD_DIM, HEAD_DIM), 1)
        hc = lambda h: slice(h * HEAD_DIM, (h + 1) * HEAD_DIM)
        zs = jnp.zeros((HEAD_DIM, HEAD_DIM), BF16)
        zv = jnp.zeros((chunk, HEAD_DIM), BF16)
        state = [jnp.where(t_c == 0, 0.0, s_scr[h]) for h in hs]
        o_rows = [[] for _ in hs]
        for n in range(tile // chunk):
            r = slice(n * chunk, (n + 1) * chunk)
            col, odd = divmod(n * chunk, HEAD_DIM)
            cs = slice(col * HEAD_DIM, (col + 1) * HEAD_DIM)
            pairs = range(0, N_HEADS, 2)
            pair_cols = lambda h0: slice(h0 * HEAD_DIM, (h0 + 2) * HEAD_DIM)
            m1 = {}
            for h0 in pairs:
                h1 = h0 + 1
                lhs1 = jnp.concatenate([s2_w[r, pair_cols(h0)], s2_qd[r, pair_cols(h0)]], axis=0)
                s0, s1 = state[h0].astype(BF16), state[h1].astype(BF16)
                sbd = jnp.concatenate([jnp.concatenate([s0, zs], axis=1), jnp.concatenate([zs, s1], axis=1)], axis=0)
                m1[h0] = _mmb(lhs1, sbd)
            yield
            new_state = list(state)
            decay_row = jnp.exp(s2_gl[n * chunk:n * chunk + 1, :])
            for h0 in pairs:
                h1 = h0 + 1
                v0 = s2_u[r, hc(h0)] - m1[h0][:chunk, :HEAD_DIM]
                v1 = s2_u[r, hc(h1)] - m1[h0][:chunk, HEAD_DIM:]
                vbd = jnp.concatenate([jnp.concatenate([v0.astype(BF16), zv], axis=1),
                                       jnp.concatenate([zv, v1.astype(BF16)], axis=1)], axis=0)
                k0, k1 = s2_kt[hc(h0), cs], s2_kt[hc(h1), cs]
                if odd:
                    kpair = jnp.where(lane_t < half, pltpu.roll(k0, half, 1), k1)
                else:
                    kpair = jnp.where(lane_t < half, k0, pltpu.roll(k1, half, 1))
                qpair = jnp.where(lane_c < half, s2_qk[r, hc(h0)], s2_qk[r, hc(h1)])
                m2 = _mmb(jnp.concatenate([qpair, kpair], axis=0), vbd)
                o_rows[h0].append(m1[h0][chunk:, :HEAD_DIM] + m2[:chunk, :HEAD_DIM])
                o_rows[h1].append(m1[h0][chunk:, HEAD_DIM:] + m2[:chunk, HEAD_DIM:])
                new_state[h0] = state[h0] * decay_row[:, N_HEADS + h0:N_HEADS + h0 + 1] + m2[chunk:, :HEAD_DIM]
                new_state[h1] = state[h1] * decay_row[:, N_HEADS + h1:N_HEADS + h1 + 1] + m2[chunk:, HEAD_DIM:]
            state = new_state
            yield
        for h in hs:
            s_scr[h] = state[h]
        return [jnp.concatenate(o_rows[h], axis=0) for h in hs]

    def stage3_out(o_heads):
        y_ref[0] = yield from _finish(xb_ref[0], cring[slot_old], o_heads, zring[slot_old], pb_ref[0], w)

    def stage2():
        heads_of = lambda ref: [ref[:, h * HEAD_DIM:(h + 1) * HEAD_DIM] for h in hs]
        uu2, wk2, qk2, qd2, kd2, _, gtot2 = yield from _chunk_local(
            heads_of(st_q), heads_of(st_k), heads_of(st_v), st_bg[...], chunk, SUB)
        folds = []
        for h in hs:
            f = qk2[h][:, 0:HEAD_DIM]
            for c in range(1, tile // HEAD_DIM):
                f = f + qk2[h][:, c * HEAD_DIM:(c + 1) * HEAD_DIM]
            folds.append(f + pltpu.roll(f, half, 1))
        yield
        return [(s2_u, jnp.concatenate(uu2, axis=1)), (s2_w, jnp.concatenate(wk2, axis=1)),
                (s2_qd, jnp.concatenate(qd2, axis=1)), (s2_qk, jnp.concatenate(folds, axis=1)),
                (s2_kt, jnp.concatenate([kd2[h].T for h in hs], axis=0)), (s2_gl, gtot2)]

    def stage1_project():
        proj = _Projection(x_ref[0], w)
        piece = 2 * LANES
        d_conv, d_dn = ubuf.shape[0] * LANES, N_HEADS * HEAD_DIM
        ab = []
        for lo in range(0, 2 * d_conv, piece):
            ab.append(proj._cols(lo, piece))
            yield
        n = len(ab) // 2
        for i in range(n):
            u = ab[i] * jax.nn.sigmoid(ab[n + i])
            for c in range(piece // LANES):
                ubuf[i * (piece // LANES) + c, upad:upad + tile, :] = u[:, _lanes(c)]
        gate = []
        for lo in range(2 * d_conv, 3 * d_conv, piece):
            gate.append(proj._cols(lo, piece))
            yield
        for i, lo in enumerate(range(3 * d_conv, 3 * d_conv + 3 * d_dn, piece)):
            cols = proj._cols(lo, piece)
            for c in range(piece // LANES):
                qbuf[i * (piece // LANES) + c, qpad:qpad + tile, :] = cols[:, _lanes(c)]
            yield
        for lo in range(0, d_dn, piece):
            zring[slot_new, :, lo:lo + piece] = proj._cols(3 * d_conv + 3 * d_dn + lo, piece)
            yield
        st_bg[...] = proj.beta_g()
        return jnp.concatenate(gate, axis=1)

    def stage1_conv(c_gate):
        starts = [(r0, p) for r0 in range(0, tile, 2 * CONV_ROWS) for p in range(2)]
        groups = [starts[:len(starts) // 2], starts[len(starts) // 2:]]
        for c in range(ubuf.shape[0]):
            for group in groups:
                accs = [jnp.zeros((CONV_ROWS, LANES), F32) for _ in group]
                for j in range(kw):
                    w_row = jnp.broadcast_to(w["dw_w"][j:j + 1, _lanes(c)], (CONV_ROWS, LANES))
                    for i, (r0, p) in enumerate(group):
                        win = ubuf[c, pl.ds(upad - hist + r0 + p + j, CONV_ROWS, stride=2), :]
                        accs[i] = accs[i] + win * w_row
                for (r0, p), acc in zip(group, accs):
                    cbuf[c, pl.ds(r0 + p, CONV_ROWS, stride=2), :] = acc
                yield
        c_raw = jnp.concatenate([cbuf[c] for c in range(cbuf.shape[0])], axis=1)
        cring[slot_new] = _conv_branch_tail(c_raw, c_gate, w)
        yield
        half_tile = tile // 2
        for c in range(qbuf.shape[0]):
            for p in range(2):
                acc = jnp.zeros((half_tile, LANES), F32)
                for j in range(ks):
                    win = qbuf[c, pl.ds(qpad - qhist + p + j, half_tile, stride=2), :]
                    acc = acc + win * w["dn_w"][j:j + 1, _lanes(c)]
                qcv[c, pl.ds(p, half_tile, stride=2), :] = acc
            if c % 3 == 2:
                yield
        qkv_c = jnp.concatenate([qcv[c] for c in range(qcv.shape[0])], axis=1)
        qa, ka, va = yield from _qkv_heads(qkv_c)
        st_q[...] = jnp.concatenate(qa, axis=1)
        st_k[...] = jnp.concatenate(ka, axis=1)
        st_v[...] = jnp.concatenate(va, axis=1)
        for c in range(ubuf.shape[0]):
            ubuf[c, 0:upad, :] = ubuf[c, tile:tile + upad, :]
        for c in range(qbuf.shape[0]):
            qbuf[c, 0:qpad, :] = qbuf[c, tile:tile + qpad, :]

    o_heads, delta_local, c_gate = _interleave([stage3(), stage2(), stage1_project()], [8, 14, 15])
    for ref, value in delta_local:
        ref[...] = value
    _interleave([stage3_out(o_heads), stage1_conv(c_gate)], [3, 17])

    @pl.when((t_a == nt - 1) & (s < pl.num_programs(0) - 2))
    def _():
        nconv_ref[0] = jnp.concatenate([ubuf[c, upad + tile - hist:upad + tile, :] for c in range(ubuf.shape[0])],
                                       axis=1)
        ndn_ref[0] = jnp.concatenate([qbuf[c, qpad + tile - qhist:qpad + tile, :] for c in range(qbuf.shape[0])],
                                     axis=1)

    @pl.when((t_c == nt - 1) & (s > 1))
    def _():
        ns_ref[0] = s_scr[...]


def _sample_kernel(*refs, seqs, steps):
    x_ref, p_ref, sc_ref, sdn_ref, s_ref = refs[:5]
    nw = len(_WEIGHT_NAMES)
    w = dict(zip(_WEIGHT_NAMES, refs[5:5 + nw]))
    y_ref, nconv_ref, ndn_ref, ns_ref = refs[5 + nw:9 + nw]
    ustage, cstage, qstage, qcs = refs[9 + nw:]
    rows = seqs * steps
    kw = w["dw_w"].shape[0]
    ks = w["dn_w"].shape[0]
    hist, qhist = sc_ref.shape[0], sdn_ref.shape[0]
    by_time = lambda ref, c, t: ref[c, pl.ds(t, seqs, stride=steps), :]

    x = x_ref[...]
    proj = _Projection(x, w)
    hs = range(N_HEADS)

    def conv_branch():
        u = proj.glu()
        n_uc = ustage.shape[0]
        for c in range(n_uc):
            ustage[c] = u[:, _lanes(c)]
        u_tm = [[by_time(ustage, c, t) for c in range(n_uc)] for t in range(steps)]
        yield
        for t in range(steps):
            for c in range(n_uc):
                acc = jnp.zeros((seqs, LANES), F32)
                for j in range(kw):
                    i = t + j
                    src = sc_ref[i, :, _lanes(c)] if i < hist else u_tm[i - hist][c]
                    acc = acc + src * w["dw_w"][j:j + 1, _lanes(c)]
                cstage[c, pl.ds(t, seqs, stride=steps), :] = acc
            yield
        for i in range(hist):
            nconv_ref[i] = sc_ref[i + steps] if i + steps < hist else jnp.concatenate(u_tm[i + steps - hist], axis=1)
        c_gate = proj.c_gate()
        yield
        return _conv_branch_tail(jnp.concatenate([cstage[c] for c in range(n_uc)], axis=1), c_gate, w)

    def delta_branch():
        n_qc = qstage.shape[0]
        for part in range(3):
            qkv_p = proj.qkv(part)
            for c in range(N_HEADS):
                qstage[part * N_HEADS + c] = qkv_p[:, _lanes(c)]
            yield
        q_tm = [[by_time(qstage, c, t) for c in range(n_qc)] for t in range(steps)]
        for t in range(steps):
            for c in range(n_qc):
                acc = jnp.zeros((seqs, LANES), F32)
                for j in range(ks):
                    i = t + j
                    src = sdn_ref[i, :, _lanes(c)] if i < qhist else q_tm[i - qhist][c]
                    acc = acc + src * w["dn_w"][j:j + 1, _lanes(c)]
                qcs[c, pl.ds(t, seqs, stride=steps), :] = acc
            if t % 2:
                yield
        for i in range(qhist):
            ndn_ref[i] = jnp.concatenate(q_tm[steps - qhist + i], axis=1)
        qs, kss, vs = yield from _qkv_heads(jnp.concatenate([qcs[c] for c in range(n_qc)], axis=1))
        bg = proj.beta_g()
        yield
        uu, wk, qk, qd, kd, gl, _ = yield from _chunk_local(qs, kss, vs, bg, steps, steps)
        seq_of_col = lax.broadcasted_iota(jnp.int32, (seqs, 1, rows), 2) // steps
        seq_id = lax.broadcasted_iota(jnp.int32, (seqs, 1, rows), 0)
        col_mask = (seq_of_col == seq_id).astype(F32)
        s_old = [s_ref[:, h] for h in hs]
        m1 = []
        for h in hs:
            lhs = jnp.concatenate([wk[h].reshape(seqs, steps, HEAD_DIM), qd[h].reshape(seqs, steps, HEAD_DIM)], axis=1)
            m1.append(lax.dot_general(lhs.astype(BF16), s_old[h].astype(BF16), (((2,), (1,)), ((0,), (0,))),
                                      preferred_element_type=F32))
        yield
        v_new = [uu[h] - m1[h][:, :steps, :].reshape(rows, HEAD_DIM) for h in hs]
        o_heads = [m1[h][:, steps:, :].reshape(rows, HEAD_DIM) + _mm(qk[h], v_new[h]) for h in hs]
        yield
        for h in hs:
            kd_rows = (kd[h].T[None, :, :] * col_mask).reshape(seqs * HEAD_DIM, rows)
            ds = _mm(kd_rows, v_new[h]).reshape(seqs, HEAD_DIM, HEAD_DIM)
            gl_seq = jnp.exp(gl[h].reshape(seqs, steps, 1)[:, 0:1, :])
            ns_ref[:, h] = s_old[h] * gl_seq + ds
        return o_heads

    c_out, o_heads = _interleave([conv_branch(), delta_branch()], [10, 24])
    z = proj.z()
    y_ref[...] = _run(_finish(x, c_out, o_heads, z, p_ref[...], w))


def _full_spec(a):
    nd = a.ndim
    return pl.BlockSpec(a.shape, lambda *_: (0,) * nd, pipeline_mode=pl.Buffered(1))


def kernel(x_prompt, x_sample, state_conv, state_dn_conv, state_dn_S, p_prompt, p_sample, norm_mix_g, w_in, conv_dw_w, conv_dw_b, conv_ln_g, conv_ln_b, conv_pw_w, dn_conv_w, dn_a_log, dn_dt_bias, dn_norm_g, w_out, ple_norm_g, ple_gate_w, ple_proj_w, final_norm_g):
    depth = w_in.shape[0]
    assert depth == 1, "single trunk layer"
    bsz, seqlen, d_model = x_prompt.shape
    dec_b, dec_l, _ = x_sample.shape
    d_conv = conv_dw_w.shape[-1]
    d_dn = N_HEADS * HEAD_DIM
    kw, ks = conv_dw_w.shape[1], dn_conv_w.shape[1]
    d_main = 3 * d_conv + 4 * d_dn
    assert w_in.shape[-1] == d_main + 2 * N_HEADS
    assert dn_conv_w.shape[-1] == 3 * d_dn and dn_norm_g.shape[-1] == HEAD_DIM
    tile, chunk = min(PROMPT_TILE, seqlen), min(PROMPT_CHUNK, seqlen)
    assert seqlen % tile == 0 and tile % chunk == 0 and chunk % SUB == 0 and tile % (2 * CONV_ROWS) == 0
    assert tile >= 32 and dec_b % SAMPLE_SEQS == 0 and SAMPLE_SEQS % 8 == 0 and dec_l == 8
    assert 2 * chunk == HEAD_DIM and tile % HEAD_DIM == 0

    row = lambda v: v.reshape(1, -1).astype(F32)
    lanes = jnp.zeros((1, 128), F32)
    weights = dict(
        g_mix=row(norm_mix_g[0]),
        w_in=w_in[0].astype(BF16),
        w_tail=jnp.zeros((d_model, 128), BF16).at[:, :2 * N_HEADS].set(w_in[0, :, d_main:].astype(BF16)),
        dw_w=conv_dw_w[0].astype(F32), dw_b=row(conv_dw_b[0]), ln_g=row(conv_ln_g[0]), ln_b=row(conv_ln_b[0]),
        pw=conv_pw_w[0].astype(BF16),
        dn_w=dn_conv_w[0].astype(F32),
        a_log=lanes.at[0, N_HEADS:2 * N_HEADS].set(dn_a_log[0]),
        dt_bias=lanes.at[0, N_HEADS:2 * N_HEADS].set(dn_dt_bias[0]),
        dn_g=row(dn_norm_g[0]),
        w_out=w_out[0].astype(BF16),
        ple_g=row(ple_norm_g[0]), ple_gate=ple_gate_w[0].astype(BF16), ple_proj=ple_proj_w[0].astype(BF16),
        fin_g=row(final_norm_g),
    )
    wlist = [weights[n] for n in _WEIGHT_NAMES]
    wspecs = [_full_spec(a) for a in wlist]
    params = dict(vmem_limit_bytes=VMEM_LIMIT_BYTES)

    nt = seqlen // tile
    upad = -(-(kw - 1) // 8) * 8
    qpad = -(-(ks - 1) // 8) * 8
    assert nt > 1
    n_tiles = bsz * nt
    front = lambda s: (jnp.minimum(s, n_tiles - 1) // nt, jnp.minimum(s, n_tiles - 1) % nt)
    back = lambda s: (jnp.maximum(s - 2, 0) // nt, jnp.maximum(s - 2, 0) % nt)
    stage = lambda cols: pltpu.VMEM((tile, cols), F32)
    slabs = lambda cols, rows: pltpu.VMEM((cols // LANES, rows, LANES), F32)
    y_p, nconv_p, ndn_p, ns_p = pl.pallas_call(
        functools.partial(_prompt_kernel, tile=tile, chunk=chunk, nt=nt),
        grid=(n_tiles + 2,),
        in_specs=[pl.BlockSpec((1, tile, d_model), lambda s: (*front(s), 0)),
                  pl.BlockSpec((1, tile, d_model), lambda s: (*back(s), 0)),
                  pl.BlockSpec((1, tile, p_prompt.shape[-1]), lambda s: (*back(s), 0))] + wspecs,
        out_specs=[pl.BlockSpec((1, tile, d_model), lambda s: (*back(s), 0)),
                   pl.BlockSpec((1, kw - 1, d_conv), lambda s: (front(s)[0], 0, 0)),
                   pl.BlockSpec((1, ks - 1, 3 * d_dn), lambda s: (front(s)[0], 0, 0)),
                   pl.BlockSpec((1, N_HEADS, HEAD_DIM, HEAD_DIM), lambda s: (back(s)[0], 0, 0, 0))],
        out_shape=[jax.ShapeDtypeStruct((bsz, seqlen, d_model), F32),
                   jax.ShapeDtypeStruct((bsz, kw - 1, d_conv), F32),
                   jax.ShapeDtypeStruct((bsz, ks - 1, 3 * d_dn), F32),
                   jax.ShapeDtypeStruct((bsz, N_HEADS, HEAD_DIM, HEAD_DIM), F32)],
        scratch_shapes=[slabs(d_conv, upad + tile),
                        slabs(3 * d_dn, qpad + tile),
                        slabs(d_conv, tile),
                        pltpu.VMEM((N_HEADS, HEAD_DIM, HEAD_DIM), F32),
                        slabs(3 * d_dn, tile),
                        pltpu.VMEM((3, tile, d_conv), F32),
                        pltpu.VMEM((3, tile, d_dn), F32),
                        stage(d_dn), stage(d_dn), stage(d_dn), stage(128),
                        stage(d_dn), stage(d_dn), stage(d_dn), stage(d_dn),
                        pltpu.VMEM((d_dn, tile), F32), stage(128)],
        compiler_params=pltpu.CompilerParams(dimension_semantics=("arbitrary",), **params),
        name="prompt_layer",
    )(x_prompt, x_prompt, p_prompt[0], *wlist)

    seqs = SAMPLE_SEQS
    rows = seqs * dec_l
    xs = x_sample.reshape(dec_b * dec_l, d_model)
    ps = p_sample[0].reshape(dec_b * dec_l, -1)
    tm = lambda a: jnp.transpose(a, (0, 2, 1, 3))
    conv_spec = pl.BlockSpec((None, kw - 1, seqs, d_conv), lambda i: (0, 0, i, 0))
    dn_spec = pl.BlockSpec((None, ks - 1, seqs, 3 * d_dn), lambda i: (0, 0, i, 0))
    state_spec = pl.BlockSpec((None, seqs, N_HEADS, HEAD_DIM, HEAD_DIM), lambda i: (0, i, 0, 0, 0))
    y_s, nconv_s, ndn_s, ns_s = pl.pallas_call(
        functools.partial(_sample_kernel, seqs=seqs, steps=dec_l),
        grid=(dec_b // seqs,),
        in_specs=[pl.BlockSpec((rows, d_model), lambda i: (i, 0)),
                  pl.BlockSpec((rows, ps.shape[-1]), lambda i: (i, 0)),
                  conv_spec, dn_spec, state_spec] + wspecs,
        out_specs=[pl.BlockSpec((rows, d_model), lambda i: (i, 0)), conv_spec, dn_spec, state_spec],
        out_shape=[jax.ShapeDtypeStruct((dec_b * dec_l, d_model), F32),
                   jax.ShapeDtypeStruct((1, kw - 1, dec_b, d_conv), F32),
                   jax.ShapeDtypeStruct((1, ks - 1, dec_b, 3 * d_dn), F32),
                   jax.ShapeDtypeStruct((1, dec_b, N_HEADS, HEAD_DIM, HEAD_DIM), F32)],
        scratch_shapes=[slabs(d_conv, rows), slabs(d_conv, rows), slabs(3 * d_dn, rows), slabs(3 * d_dn, rows)],
        compiler_params=pltpu.CompilerParams(dimension_semantics=("arbitrary",), **params),
        name="sample_layer",
    )(xs, ps, tm(state_conv), tm(state_dn_conv), state_dn_S, *wlist)

    return (y_p, y_s.reshape(dec_b, dec_l, d_model), nconv_p[None], ndn_p[None], ns_p[None],
            tm(nconv_s), tm(ndn_s), ns_s)
```

```python
import functools

import jax
import jax.numpy as jnp
from jax import lax
from jax.experimental import pallas as pl
from jax.experimental.pallas import tpu as pltpu

EPS = 1e-6
N_HEADS = 4
HEAD_DIM = 128
LANES = 128
SUB = 16
PROMPT_TILE = 256
PROMPT_CHUNK = 64
SAMPLE_SEQS = 16
CONV_ROWS = 32
VMEM_LIMIT_BYTES = 56 * 1024 * 1024

F32 = jnp.float32
BF16 = jnp.bfloat16


def _run(gen):
    try:
        while True:
            next(gen)
    except StopIteration as stop:
        return stop.value


def _interleave(gens, weights):
    n = len(gens)
    done, alive, out = [0] * n, [True] * n, [None] * n
    while any(alive):
        k = min((i for i in range(n) if alive[i]), key=lambda i: (done[i] + 1) / weights[i])
        try:
            next(gens[k])
            done[k] += 1
        except StopIteration as stop:
            out[k], alive[k] = stop.value, False
    return out


def _mm(a, b):
    return jnp.dot(a.astype(BF16), b.astype(BF16), preferred_element_type=F32)


def _mm_w(a, weight):
    return jnp.dot(a, weight, preferred_element_type=F32)


def _mm_nt(a, b):
    return lax.dot_general(a.astype(BF16), b.astype(BF16), (((1,), (1,)), ((), ())),
                           preferred_element_type=F32)


def _mmb(a, b_bf16):
    if a.shape[0] % 16:
        return jnp.dot(a.astype(F32), b_bf16.astype(F32), preferred_element_type=F32)
    return jnp.dot(a.astype(BF16), b_bf16, preferred_element_type=F32)


def _rms(x, g):
    return x * lax.rsqrt(jnp.mean(x * x, axis=-1, keepdims=True) + EPS) * g


def _silu(x):
    return x * jax.nn.sigmoid(x)


def _softplus(x):
    return jnp.maximum(x, 0.0) + jnp.log(1.0 + jnp.exp(-jnp.abs(x)))


def _head(a, h):
    return a[:, h * HEAD_DIM:(h + 1) * HEAD_DIM]


def _lanes(c):
    return slice(c * LANES, (c + 1) * LANES)


class _Projection:
    def __init__(self, x, w):
        self.w, self.d_conv, self.d_dn = w, w["dw_w"].shape[-1], N_HEADS * HEAD_DIM
        self.h = _rms(x, w["g_mix"][...]).astype(BF16)

    def _cols(self, lo, n):
        return jnp.dot(self.h, self.w["w_in"][:, lo:lo + n], preferred_element_type=F32)

    def glu(self):
        ab = self._cols(0, 2 * self.d_conv)
        return ab[:, :self.d_conv] * jax.nn.sigmoid(ab[:, self.d_conv:])

    def c_gate(self):
        return self._cols(2 * self.d_conv, self.d_conv)

    def qkv(self, part):
        return self._cols(3 * self.d_conv + part * self.d_dn, self.d_dn)

    def z(self):
        return self._cols(3 * self.d_conv + 3 * self.d_dn, self.d_dn)

    def beta_g(self):
        w = self.w
        tail = jnp.dot(self.h, w["w_tail"][...], preferred_element_type=F32)
        lane = lax.broadcasted_iota(jnp.int32, tail.shape, 1)
        beta = jax.nn.sigmoid(tail)
        g = -jnp.exp(w["a_log"][...]) * _softplus(tail + w["dt_bias"][...])
        return jnp.where(lane < N_HEADS, beta, jnp.where(lane < 2 * N_HEADS, g, 0.0))


def _conv_branch_tail(c, c_gate, w):
    c = c + w["dw_b"][...]
    cc = c - jnp.mean(c, axis=-1, keepdims=True)
    c = cc * lax.rsqrt(jnp.mean(cc * cc, axis=-1, keepdims=True) + EPS) * w["ln_g"][...] + w["ln_b"][...]
    c = _silu(c)
    return _mm_w(c, w["pw"][...]) * _silu(c_gate)


def _qkv_heads(qkv_c):
    d_dn = N_HEADS * HEAD_DIM
    qs, ks, vs = [], [], []
    for h in range(N_HEADS):
        q = _silu(_head(qkv_c, h))
        k = _silu(_head(qkv_c[:, d_dn:2 * d_dn], h))
        v = _silu(_head(qkv_c[:, 2 * d_dn:], h))
        q = q * (lax.rsqrt(jnp.sum(q * q, axis=-1, keepdims=True) + EPS) * (HEAD_DIM ** -0.5))
        k = k * lax.rsqrt(jnp.sum(k * k, axis=-1, keepdims=True) + EPS)
        qs.append(q); ks.append(k); vs.append(v)
        yield
    return qs, ks, vs


def _chunk_masks(rows, chunk, sub):
    ri = lax.broadcasted_iota(jnp.int32, (rows, rows), 0)
    ci = lax.broadcasted_iota(jnp.int32, (rows, rows), 1)
    same = (ri // chunk) == (ci // chunk)
    incl = same & (ci <= ri)
    strict = same & (ci < ri)
    offdiag = (ri // sub) != (ci // sub)
    return incl, strict, offdiag


def _tri_inverse(a_list, offdiag, chunk, sub):
    rows = a_list[0].shape[0]
    nc = rows // chunk
    hs = range(len(a_list))
    pi = lax.broadcasted_iota(jnp.int32, (chunk, rows), 0)
    pl_ = lax.broadcasted_iota(jnp.int32, (chunk, rows), 1)
    lane_in, lane_blk = pl_ % chunk, pl_ // chunk
    eye_pan = (lane_in == pi).astype(F32)
    diag_pan = (pi // sub) == (lane_in // sub)
    packed = chunk % 16 == 0
    blk_masks = [(lane_blk == c).astype(BF16 if packed else F32) for c in range(nc)]

    def fold(full):
        out = full[0:chunk]
        for c in range(1, nc):
            out = out + full[c * chunk:(c + 1) * chunk]
        return out

    def expand(pan):
        src = pan.astype(BF16) if packed else pan
        return jnp.concatenate([src * blk_masks[c] for c in range(nc)], axis=0).astype(BF16)

    a_pan = [fold(a_list[h]) for h in hs]
    d_pan = [jnp.where(diag_pan, a_pan[h], 0.0) for h in hs]
    p_pan = [eye_pan - d_pan[h] for h in hs]
    pw_pan, span = d_pan, 1
    pw_full = [expand(d_pan[h]) for h in hs]
    while 2 * span < sub:
        pw_pan = [_mmb(pw_pan[h], pw_full[h]) for h in hs]
        yield
        pw_full = [expand(pw_pan[h]) for h in hs]
        p_pan = [p_pan[h] + _mmb(p_pan[h], pw_full[h]) for h in hs]
        span *= 2
    nblk = chunk // sub
    if nblk == 1:
        return [expand(p_pan[h]) for h in hs]
    yield
    p_full = [expand(p_pan[h]) for h in hs]
    n_full = [jnp.where(offdiag, a_list[h], 0.0).astype(BF16) for h in hs]
    b_pan = [_mmb(p_pan[h], n_full[h]) for h in hs]
    yield
    t_pan = [eye_pan - b_pan[h] for h in hs]
    bp_pan, span = b_pan, 1
    bp_full = [expand(b_pan[h]) for h in hs] if nblk > 2 else None
    while 2 * span < nblk:
        bp_pan = [_mmb(bp_pan[h], bp_full[h]) for h in hs]
        yield
        bp_full = [expand(bp_pan[h]) for h in hs]
        t_pan = [t_pan[h] + _mmb(t_pan[h], bp_full[h]) for h in hs]
        yield
        span *= 2
    tinv = [expand(_mmb(t_pan[h], p_full[h])) for h in hs]
    yield
    return tinv


def _gate_scalars(bg, chunk):
    rows, lanes = bg.shape
    pos = lax.broadcasted_iota(jnp.int32, bg.shape, 0) % chunk
    gc, s = bg, 1
    while s < chunk:
        gc = gc + jnp.where(pos >= s, pltpu.roll(gc, s, 0), 0.0)
        s *= 2
    gtot = jnp.concatenate([jnp.broadcast_to(gc[e - 1:e, :], (chunk, lanes)) for e in range(chunk, rows + 1, chunk)],
                           axis=0)
    return gc, gc.T, gtot


def _chunk_local(qs, ks, vs, bg, chunk, sub):
    incl, strict, offdiag = _chunk_masks(bg.shape[0], chunk, sub)
    gc, gct, gtot = _gate_scalars(bg, chunk)
    hs = range(N_HEADS)
    beta = [bg[:, h:h + 1] for h in hs]
    gcc = [gc[:, N_HEADS + h:N_HEADS + h + 1] for h in hs]
    gl = [gtot[:, N_HEADS + h:N_HEADS + h + 1] for h in hs]
    yield
    decay = [jnp.where(incl, jnp.exp(jnp.where(incl, gcc[h] - gct[N_HEADS + h:N_HEADS + h + 1, :], 0.0)), 0.0)
             for h in hs]
    kb = [ks[h] * beta[h] for h in hs]
    yield
    a = [jnp.where(strict, _mm_nt(kb[h], ks[h]) * decay[h], 0.0) for h in hs]
    yield
    qk = [_mm_nt(qs[h], ks[h]) * decay[h] for h in hs]
    egc = [jnp.exp(gcc[h]) for h in hs]
    rhs = [jnp.concatenate([vs[h] * beta[h], kb[h] * egc[h]], axis=1).astype(BF16) for h in hs]
    qd = [qs[h] * egc[h] for h in hs]
    kd = [ks[h] * jnp.exp(gl[h] - gcc[h]) for h in hs]
    yield
    tinv = yield from _tri_inverse(a, offdiag, chunk, sub)
    y = [_mmb(tinv[h], rhs[h]) for h in hs]
    yield
    u = [y[h][:, :HEAD_DIM] for h in hs]
    wk = [y[h][:, HEAD_DIM:] for h in hs]
    return u, wk, qk, qd, kd, gl, gtot


def _finish(x, c_out, o_heads, z, p_emb, w):
    outs = []
    for h in range(N_HEADS):
        o = o_heads[h]
        o = o * lax.rsqrt(jnp.mean(o * o, axis=-1, keepdims=True) + EPS) * w["dn_g"][...]
        outs.append(o * _silu(_head(z, h)))
    mix_in = jnp.concatenate([c_out] + outs, axis=1)
    x = x + _mm_w(mix_in, w["w_out"][...])
    yield
    gate = jax.nn.sigmoid(_mm_w(_rms(x, w["ple_g"][...]), w["ple_gate"][...]))
    yield
    x = x + gate * _mm_w(p_emb, w["ple_proj"][...])
    return _rms(x, w["fin_g"][...])


_WEIGHT_NAMES = ("g_mix", "w_in", "w_tail", "dw_w", "dw_b", "ln_g", "ln_b", "pw", "dn_w", "a_log",
                 "dt_bias", "dn_g", "w_out", "ple_g", "ple_gate", "ple_proj", "fin_g")


def _prompt_kernel(*refs, tile, chunk, nt):
    x_ref, xb_ref, pb_ref = refs[0], refs[1], refs[2]
    nw = len(_WEIGHT_NAMES)
    w = dict(zip(_WEIGHT_NAMES, refs[3:3 + nw]))
    y_ref, nconv_ref, ndn_ref, ns_ref = refs[3 + nw:7 + nw]
    (ubuf, qbuf, cbuf, s_scr, qcv, cring, zring, st_q, st_k, st_v, st_bg,
     s2_u, s2_w, s2_qd, s2_qk, s2_kt, s2_gl) = refs[7 + nw:]
    s = pl.program_id(0)
    slot_new, slot_old = s % 3, (s + 1) % 3
    t_a = s % nt
    t_c = (s + nt - 2) % nt
    kw = w["dw_w"].shape[0]
    ks = w["dn_w"].shape[0]
    hist, qhist = kw - 1, ks - 1
    upad, qpad = ubuf.shape[1] - tile, qbuf.shape[1] - tile
    hs = range(N_HEADS)
    half = HEAD_DIM // 2

    @pl.when(s == 0)
    def _():
        for ref in (cring, zring, st_q, st_k, st_v, st_bg, s_scr, s2_u, s2_w, s2_qd, s2_qk, s2_kt, s2_gl):
            ref[...] = jnp.zeros(ref.shape, F32)

    @pl.when(t_a == 0)
    def _():
        ubuf[:, 0:upad, :] = jnp.zeros((ubuf.shape[0], upad, LANES), F32)
        qbuf[:, 0:qpad, :] = jnp.zeros((qbuf.shape[0], qpad, LANES), F32)

    def stage3():
        lane_c = lax.broadcasted_iota(jnp.int32, (chunk, HEAD_DIM), 1)
        lane_t = lax.broadcasted_iota(jnp.int32, (HEAD_DIM, HEAD_DIM), 1)
        hc = lambda h: slice(h * HEAD_DIM, (h + 1) * HEAD_DIM)
        zs = jnp.zeros((HEAD_DIM, HEAD_DIM), BF16)
        zv = jnp.zeros((chunk, HEAD_DIM), BF16)
        state = [jnp.where(t_c == 0, 0.0, s_scr[h]) for h in hs]
        o_rows = [[] for _ in hs]
        for n in range(tile // chunk):
            r = slice(n * chunk, (n + 1) * chunk)
            col, odd = divmod(n * chunk, HEAD_DIM)
            cs = slice(col * HEAD_DIM, (col + 1) * HEAD_DIM)
            pairs = range(0, N_HEADS, 2)
            pair_cols = lambda h0: slice(h0 * HEAD_DIM, (h0 + 2) * HEAD_DIM)
            m1 = {}
            for h0 in pairs:
                h1 = h0 + 1
                lhs1 = jnp.concatenate([s2_w[r, pair_cols(h0)], s2_qd[r, pair_cols(h0)]], axis=0)
                s0, s1 = state[h0].astype(BF16), state[h1].astype(BF16)
                sbd = jnp.concatenate([jnp.concatenate([s0, zs], axis=1), jnp.concatenate([zs, s1], axis=1)], axis=0)
                m1[h0] = _mmb(lhs1, sbd)
            yield
            new_state = list(state)
            decay_row = jnp.exp(s2_gl[n * chunk:n * chunk + 1, :])
            for h0 in pairs:
                h1 = h0 + 1
                v0 = s2_u[r, hc(h0)] - m1[h0][:chunk, :HEAD_DIM]
                v1 = s2_u[r, hc(h1)] - m1[h0][:chunk, HEAD_DIM:]
                vbd = jnp.concatenate([jnp.concatenate([v0.astype(BF16), zv], axis=1),
                                       jnp.concatenate([zv, v1.astype(BF16)], axis=1)], axis=0)
                k0, k1 = s2_kt[hc(h0), cs], s2_kt[hc(h1), cs]
                if odd:
                    kpair = jnp.where(lane_t < half, pltpu.roll(k0, half, 1), k1)
                else:
                    kpair = jnp.where(lane_t < half, k0, pltpu.roll(k1, half, 1))
                qpair = jnp.where(lane_c < half, s2_qk[r, hc(h0)], s2_qk[r, hc(h1)])
                m2 = _mmb(jnp.concatenate([qpair, kpair], axis=0), vbd)
                o_rows[h0].append(m1[h0][chunk:, :HEAD_DIM] + m2[:chunk, :HEAD_DIM])
                o_rows[h1].append(m1[h0][chunk:, HEAD_DIM:] + m2[:chunk, HEAD_DIM:])
                new_state[h0] = state[h0] * decay_row[:, N_HEADS + h0:N_HEADS + h0 + 1] + m2[chunk:, :HEAD_DIM]
                new_state[h1] = state[h1] * decay_row[:, N_HEADS + h1:N_HEADS + h1 + 1] + m2[chunk:, HEAD_DIM:]
            state = new_state
            yield
        for h in hs:
            s_scr[h] = state[h]
        return [jnp.concatenate(o_rows[h], axis=0) for h in hs]

    def stage3_out(o_heads):
        y_ref[0] = yield from _finish(xb_ref[0], cring[slot_old], o_heads, zring[slot_old], pb_ref[0], w)

    def stage2():
        heads_of = lambda ref: [ref[:, h * HEAD_DIM:(h + 1) * HEAD_DIM] for h in hs]
        uu2, wk2, qk2, qd2, kd2, _, gtot2 = yield from _chunk_local(
            heads_of(st_q), heads_of(st_k), heads_of(st_v), st_bg[...], chunk, SUB)
        folds = []
        for h in hs:
            f = qk2[h][:, 0:HEAD_DIM]
            for c in range(1, tile // HEAD_DIM):
                f = f + qk2[h][:, c * HEAD_DIM:(c + 1) * HEAD_DIM]
            folds.append(f + pltpu.roll(f, half, 1))
        yield
        return [(s2_u, jnp.concatenate(uu2, axis=1)), (s2_w, jnp.concatenate(wk2, axis=1)),
                (s2_qd, jnp.concatenate(qd2, axis=1)), (s2_qk, jnp.concatenate(folds, axis=1)),
                (s2_kt, jnp.concatenate([kd2[h].T for h in hs], axis=0)), (s2_gl, gtot2)]

    def stage1_project():
        proj = _Projection(x_ref[0], w)
        piece = 2 * LANES
        d_conv, d_dn = ubuf.shape[0] * LANES, N_HEADS * HEAD_DIM
        ab = []
        for lo in range(0, 2 * d_conv, piece):
            ab.append(proj._cols(lo, piece))
            yield
        n = len(ab) // 2
        for i in range(n):
            u = ab[i] * jax.nn.sigmoid(ab[n + i])
            for c in range(piece // LANES):
                ubuf[i * (piece // LANES) + c, upad:upad + tile, :] = u[:, _lanes(c)]
        gate = []
        for lo in range(2 * d_conv, 3 * d_conv, piece):
            gate.append(proj._cols(lo, piece))
            yield
        for i, lo in enumerate(range(3 * d_conv, 3 * d_conv + 3 * d_dn, piece)):
            cols = proj._cols(lo, piece)
            for c in range(piece // LANES):
                qbuf[i * (piece // LANES) + c, qpad:qpad + tile, :] = cols[:, _lanes(c)]
            yield
        for lo in range(0, d_dn, piece):
            zring[slot_new, :, lo:lo + piece] = proj._cols(3 * d_conv + 3 * d_dn + lo, piece)
            yield
        st_bg[...] = proj.beta_g()
        return jnp.concatenate(gate, axis=1)

    def stage1_conv(c_gate):
        starts = [(r0, p) for r0 in range(0, tile, 2 * CONV_ROWS) for p in range(2)]
        groups = [starts[:len(starts) // 2], starts[len(starts) // 2:]]
        for c in range(ubuf.shape[0]):
            for group in groups:
                accs = [jnp.zeros((CONV_ROWS, LANES), F32) for _ in group]
                for j in range(kw):
                    w_row = jnp.broadcast_to(w["dw_w"][j:j + 1, _lanes(c)], (CONV_ROWS, LANES))
                    for i, (r0, p) in enumerate(group):
                        win = ubuf[c, pl.ds(upad - hist + r0 + p + j, CONV_ROWS, stride=2), :]
                        accs[i] = accs[i] + win * w_row
                for (r0, p), acc in zip(group, accs):
                    cbuf[c, pl.ds(r0 + p, CONV_ROWS, stride=2), :] = acc
                yield
        c_raw = jnp.concatenate([cbuf[c] for c in range(cbuf.shape[0])], axis=1)
        cring[slot_new] = _conv_branch_tail(c_raw, c_gate, w)
        yield
        half_tile = tile // 2
        for c in range(qbuf.shape[0]):
            for p in range(2):
                acc = jnp.zeros((half_tile, LANES), F32)
                for j in range(ks):
                    win = qbuf[c, pl.ds(qpad - qhist + p + j, half_tile, stride=2), :]
                    acc = acc + win * w["dn_w"][j:j + 1, _lanes(c)]
                qcv[c, pl.ds(p, half_tile, stride=2), :] = acc
            if c % 3 == 2:
                yield
        qkv_c = jnp.concatenate([qcv[c] for c in range(qcv.shape[0])], axis=1)
        qa, ka, va = yield from _qkv_heads(qkv_c)
        st_q[...] = jnp.concatenate(qa, axis=1)
        st_k[...] = jnp.concatenate(ka, axis=1)
        st_v[...] = jnp.concatenate(va, axis=1)
        for c in range(ubuf.shape[0]):
            ubuf[c, 0:upad, :] = ubuf[c, tile:tile + upad, :]
        for c in range(qbuf.shape[0]):
            qbuf[c, 0:qpad, :] = qbuf[c, tile:tile + qpad, :]

    o_heads, delta_local, c_gate = _interleave([stage3(), stage2(), stage1_project()], [8, 14, 15])
    for ref, value in delta_local:
        ref[...] = value
    _interleave([stage3_out(o_heads), stage1_conv(c_gate)], [3, 17])

    @pl.when((t_a == nt - 1) & (s < pl.num_programs(0) - 2))
    def _():
        nconv_ref[0] = jnp.concatenate([ubuf[c, upad + tile - hist:upad + tile, :] for c in range(ubuf.shape[0])],
                                       axis=1)
        ndn_ref[0] = jnp.concatenate([qbuf[c, qpad + tile - qhist:qpad + tile, :] for c in range(qbuf.shape[0])],
                                     axis=1)

    @pl.when((t_c == nt - 1) & (s > 1))
    def _():
        ns_ref[0] = s_scr[...]


def _sample_kernel(*refs, seqs, steps):
    x_ref, p_ref, sc_ref, sdn_ref, s_ref = refs[:5]
    nw = len(_WEIGHT_NAMES)
    w = dict(zip(_WEIGHT_NAMES, refs[5:5 + nw]))
    y_ref, nconv_ref, ndn_ref, ns_ref = refs[5 + nw:9 + nw]
    ustage, cstage, qstage, qcs = refs[9 + nw:]
    rows = seqs * steps
    kw = w["dw_w"].shape[0]
    ks = w["dn_w"].shape[0]
    hist, qhist = sc_ref.shape[0], sdn_ref.shape[0]
    by_time = lambda ref, c, t: ref[c, pl.ds(t, seqs, stride=steps), :]

    x = x_ref[...]
    proj = _Projection(x, w)
    hs = range(N_HEADS)

    def conv_branch():
        u = proj.glu()
        n_uc = ustage.shape[0]
        for c in range(n_uc):
            ustage[c] = u[:, _lanes(c)]
        u_tm = [[by_time(ustage, c, t) for c in range(n_uc)] for t in range(steps)]
        yield
        for t in range(steps):
            for c in range(n_uc):
                acc = jnp.zeros((seqs, LANES), F32)
                for j in range(kw):
                    i = t + j
                    src = sc_ref[i, :, _lanes(c)] if i < hist else u_tm[i - hist][c]
                    acc = acc + src * w["dw_w"][j:j + 1, _lanes(c)]
                cstage[c, pl.ds(t, seqs, stride=steps), :] = acc
            yield
        for i in range(hist):
            nconv_ref[i] = sc_ref[i + steps] if i + steps < hist else jnp.concatenate(u_tm[i + steps - hist], axis=1)
        c_gate = proj.c_gate()
        yield
        return _conv_branch_tail(jnp.concatenate([cstage[c] for c in range(n_uc)], axis=1), c_gate, w)

    def delta_branch():
        n_qc = qstage.shape[0]
        for part in range(3):
            qkv_p = proj.qkv(part)
            for c in range(N_HEADS):
                qstage[part * N_HEADS + c] = qkv_p[:, _lanes(c)]
            yield
        q_tm = [[by_time(qstage, c, t) for c in range(n_qc)] for t in range(steps)]
        for t in range(steps):
            for c in range(n_qc):
                acc = jnp.zeros((seqs, LANES), F32)
                for j in range(ks):
                    i = t + j
                    src = sdn_ref[i, :, _lanes(c)] if i < qhist else q_tm[i - qhist][c]
                    acc = acc + src * w["dn_w"][j:j + 1, _lanes(c)]
                qcs[c, pl.ds(t, seqs, stride=steps), :] = acc
            if t % 2:
                yield
        for i in range(qhist):
            ndn_ref[i] = jnp.concatenate(q_tm[steps - qhist + i], axis=1)
        qs, kss, vs = yield from _qkv_heads(jnp.concatenate([qcs[c] for c in range(n_qc)], axis=1))
        bg = proj.beta_g()
        yield
        uu, wk, qk, qd, kd, gl, _ = yield from _chunk_local(qs, kss, vs, bg, steps, steps)
        seq_of_col = lax.broadcasted_iota(jnp.int32, (seqs, 1, rows), 2) // steps
        seq_id = lax.broadcasted_iota(jnp.int32, (seqs, 1, rows), 0)
        col_mask = (seq_of_col == seq_id).astype(F32)
        s_old = [s_ref[:, h] for h in hs]
        m1 = []
        for h in hs:
            lhs = jnp.concatenate([wk[h].reshape(seqs, steps, HEAD_DIM), qd[h].reshape(seqs, steps, HEAD_DIM)], axis=1)
            m1.append(lax.dot_general(lhs.astype(BF16), s_old[h].astype(BF16), (((2,), (1,)), ((0,), (0,))),
                                      preferred_element_type=F32))
        yield
        v_new = [uu[h] - m1[h][:, :steps, :].reshape(rows, HEAD_DIM) for h in hs]
        o_heads = [m1[h][:, steps:, :].reshape(rows, HEAD_DIM) + _mm(qk[h], v_new[h]) for h in hs]
        yield
        for h in hs:
            kd_rows = (kd[h].T[None, :, :] * col_mask).reshape(seqs * HEAD_DIM, rows)
            ds = _mm(kd_rows, v_new[h]).reshape(seqs, HEAD_DIM, HEAD_DIM)
            gl_seq = jnp.exp(gl[h].reshape(seqs, steps, 1)[:, 0:1, :])
            ns_ref[:, h] = s_old[h] * gl_seq + ds
        return o_heads

    c_out, o_heads = _interleave([conv_branch(), delta_branch()], [10, 24])
    z = proj.z()
    y_ref[...] = _run(_finish(x, c_out, o_heads, z, p_ref[...], w))


def _full_spec(a):
    nd = a.ndim
    return pl.BlockSpec(a.shape, lambda *_: (0,) * nd, pipeline_mode=pl.Buffered(1))


def kernel(x_prompt, x_sample, state_conv, state_dn_conv, state_dn_S, p_prompt, p_sample, norm_mix_g, w_in, conv_dw_w, conv_dw_b, conv_ln_g, conv_ln_b, conv_pw_w, dn_conv_w, dn_a_log, dn_dt_bias, dn_norm_g, w_out, ple_norm_g, ple_gate_w, ple_proj_w, final_norm_g):
    depth = w_in.shape[0]
    assert depth == 1, "single trunk layer"
    bsz, seqlen, d_model = x_prompt.shape
    dec_b, dec_l, _ = x_sample.shape
    d_conv = conv_dw_w.shape[-1]
    d_dn = N_HEADS * HEAD_DIM
    kw, ks = conv_dw_w.shape[1], dn_conv_w.shape[1]
    d_main = 3 * d_conv + 4 * d_dn
    assert w_in.shape[-1] == d_main + 2 * N_HEADS
    assert dn_conv_w.shape[-1] == 3 * d_dn and dn_norm_g.shape[-1] == HEAD_DIM
    tile, chunk = min(PROMPT_TILE, seqlen), min(PROMPT_CHUNK, seqlen)
    assert seqlen % tile == 0 and tile % chunk == 0 and chunk % SUB == 0 and tile % (2 * CONV_ROWS) == 0
    assert tile >= 32 and dec_b % SAMPLE_SEQS == 0 and SAMPLE_SEQS % 8 == 0 and dec_l == 8
    assert 2 * chunk == HEAD_DIM and tile % HEAD_DIM == 0

    row = lambda v: v.reshape(1, -1).astype(F32)
    head_lanes = lambda v: jnp.pad(v.astype(F32), (N_HEADS, LANES - 2 * N_HEADS)).reshape(1, LANES)
    weights = dict(
        g_mix=row(norm_mix_g[0]),
        w_in=w_in[0].astype(BF16),
        w_tail=jnp.pad(w_in[0, :, d_main:].astype(BF16), ((0, 0), (0, LANES - 2 * N_HEADS))),
        dw_w=conv_dw_w[0].astype(F32), dw_b=row(conv_dw_b[0]), ln_g=row(conv_ln_g[0]), ln_b=row(conv_ln_b[0]),
        pw=conv_pw_w[0].astype(F32),
        dn_w=dn_conv_w[0].astype(F32),
        a_log=head_lanes(dn_a_log[0]),
        dt_bias=head_lanes(dn_dt_bias[0]),
        dn_g=row(dn_norm_g[0]),
        w_out=w_out[0].astype(F32),
        ple_g=row(ple_norm_g[0]), ple_gate=ple_gate_w[0].astype(F32), ple_proj=ple_proj_w[0].astype(F32),
        fin_g=row(final_norm_g),
    )
    wlist = [weights[n] for n in _WEIGHT_NAMES]
    wspecs = [_full_spec(a) for a in wlist]
    params = dict(vmem_limit_bytes=VMEM_LIMIT_BYTES)

    nt = seqlen // tile
    upad = -(-(kw - 1) // 8) * 8
    qpad = -(-(ks - 1) // 8) * 8
    assert nt > 1
    n_tiles = bsz * nt
    front = lambda s: (jnp.minimum(s, n_tiles - 1) // nt, jnp.minimum(s, n_tiles - 1) % nt)
    back = lambda s: (jnp.maximum(s - 2, 0) // nt, jnp.maximum(s - 2, 0) % nt)
    stage = lambda cols: pltpu.VMEM((tile, cols), F32)
    slabs = lambda cols, rows: pltpu.VMEM((cols // LANES, rows, LANES), F32)
    y_p, nconv_p, ndn_p, ns_p = pl.pallas_call(
        functools.partial(_prompt_kernel, tile=tile, chunk=chunk, nt=nt),
        grid=(n_tiles + 2,),
        in_specs=[pl.BlockSpec((1, tile, d_model), lambda s: (*front(s), 0)),
                  pl.BlockSpec((1, tile, d_model), lambda s: (*back(s), 0)),
                  pl.BlockSpec((1, tile, p_prompt.shape[-1]), lambda s: (*back(s), 0))] + wspecs,
        out_specs=[pl.BlockSpec((1, tile, d_model), lambda s: (*back(s), 0)),
                   pl.BlockSpec((1, kw - 1, d_conv), lambda s: (front(s)[0], 0, 0)),
                   pl.BlockSpec((1, ks - 1, 3 * d_dn), lambda s: (front(s)[0], 0, 0)),
                   pl.BlockSpec((1, N_HEADS, HEAD_DIM, HEAD_DIM), lambda s: (back(s)[0], 0, 0, 0))],
        out_shape=[jax.ShapeDtypeStruct((bsz, seqlen, d_model), F32),
                   jax.ShapeDtypeStruct((bsz, kw - 1, d_conv), F32),
                   jax.ShapeDtypeStruct((bsz, ks - 1, 3 * d_dn), F32),
                   jax.ShapeDtypeStruct((bsz, N_HEADS, HEAD_DIM, HEAD_DIM), F32)],
        scratch_shapes=[slabs(d_conv, upad + tile),
                        slabs(3 * d_dn, qpad + tile),
                        slabs(d_conv, tile),
                        pltpu.VMEM((N_HEADS, HEAD_DIM, HEAD_DIM), F32),
                        slabs(3 * d_dn, tile),
                        pltpu.VMEM((3, tile, d_conv), F32),
                        pltpu.VMEM((3, tile, d_dn), F32),
                        stage(d_dn), stage(d_dn), stage(d_dn), stage(128),
                        stage(d_dn), stage(d_dn), stage(d_dn), stage(d_dn),
                        pltpu.VMEM((d_dn, tile), F32), stage(128)],
        compiler_params=pltpu.CompilerParams(dimension_semantics=("arbitrary",), **params),
        name="prompt_layer",
    )(x_prompt, x_prompt, p_prompt[0], *wlist)

    seqs = SAMPLE_SEQS
    rows = seqs * dec_l
    xs = x_sample.reshape(dec_b * dec_l, d_model)
    ps = p_sample[0].reshape(dec_b * dec_l, -1)
    tm = lambda a: jnp.transpose(a, (0, 2, 1, 3))
    conv_spec = pl.BlockSpec((None, kw - 1, seqs, d_conv), lambda i: (0, 0, i, 0))
    dn_spec = pl.BlockSpec((None, ks - 1, seqs, 3 * d_dn), lambda i: (0, 0, i, 0))
    state_spec = pl.BlockSpec((None, seqs, N_HEADS, HEAD_DIM, HEAD_DIM), lambda i: (0, i, 0, 0, 0))
    y_s, nconv_s, ndn_s, ns_s = pl.pallas_call(
        functools.partial(_sample_kernel, seqs=seqs, steps=dec_l),
        grid=(dec_b // seqs,),
        in_specs=[pl.BlockSpec((rows, d_model), lambda i: (i, 0)),
                  pl.BlockSpec((rows, ps.shape[-1]), lambda i: (i, 0)),
                  conv_spec, dn_spec, state_spec] + wspecs,
        out_specs=[pl.BlockSpec((rows, d_model), lambda i: (i, 0)), conv_spec, dn_spec, state_spec],
        out_shape=[jax.ShapeDtypeStruct((dec_b * dec_l, d_model), F32),
                   jax.ShapeDtypeStruct((1, kw - 1, dec_b, d_conv), F32),
                   jax.ShapeDtypeStruct((1, ks - 1, dec_b, 3 * d_dn), F32),
                   jax.ShapeDtypeStruct((1, dec_b, N_HEADS, HEAD_DIM, HEAD_DIM), F32)],
        scratch_shapes=[slabs(d_conv, rows), slabs(d_conv, rows), slabs(3 * d_dn, rows), slabs(3 * d_dn, rows)],
        compiler_params=pltpu.CompilerParams(dimension_semantics=("arbitrary",), **params),
        name="sample_layer",
    )(xs, ps, tm(state_conv), tm(state_dn_conv), state_dn_S, *wlist)

    return (y_p, y_s.reshape(dec_b, dec_l, d_model), nconv_p[None], ndn_p[None], ns_p[None],
            tm(nconv_s), tm(ndn_s), ns_s)
```

```python
import functools

import jax
import jax.numpy as jnp
from jax import lax
from jax.experimental import pallas as pl
from jax.experimental.pallas import tpu as pltpu

EPS = 1e-6
N_HEADS = 4
HEAD_DIM = 128
LANES = 128
SUBLANES = 8
PROMPT_TILE = 256
PROMPT_CHUNK = 64
SAMPLE_SEQS = 16
CONV_ROWS = 32
VMEM_LIMIT_BYTES = 56 * 1024 * 1024

F32 = jnp.float32
BF16 = jnp.bfloat16


def _run(gen):
    try:
        while True:
            next(gen)
    except StopIteration as stop:
        return stop.value


def _interleave(gens, weights):
    n = len(gens)
    done, alive, out = [0] * n, [True] * n, [None] * n
    while any(alive):
        k = min((i for i in range(n) if alive[i]), key=lambda i: (done[i] + 1) / weights[i])
        try:
            next(gens[k])
            done[k] += 1
        except StopIteration as stop:
            out[k], alive[k] = stop.value, False
    return out


def _mm(a, b):
    return jnp.dot(a.astype(BF16), b.astype(BF16), preferred_element_type=F32)


def _mm_w(a, weight):
    return jnp.dot(a, weight, preferred_element_type=F32)


def _mm_nt(a, b):
    return lax.dot_general(a.astype(BF16), b.astype(BF16), (((1,), (1,)), ((), ())),
                           preferred_element_type=F32)


def _mmb(a, b_bf16):
    if a.shape[0] % 16:
        return jnp.dot(a.astype(F32), b_bf16.astype(F32), preferred_element_type=F32)
    return jnp.dot(a.astype(BF16), b_bf16, preferred_element_type=F32)


def _rms(x, g):
    return x * lax.rsqrt(jnp.mean(x * x, axis=-1, keepdims=True) + EPS) * g


def _silu(x):
    return x * jax.nn.sigmoid(x)


def _softplus(x):
    return jnp.maximum(x, 0.0) + jnp.log(1.0 + jnp.exp(-jnp.abs(x)))


def _head(a, h):
    return a[:, h * HEAD_DIM:(h + 1) * HEAD_DIM]


def _lanes(c):
    return slice(c * LANES, (c + 1) * LANES)


class _Projection:
    def __init__(self, x, w):
        self.w, self.d_conv, self.d_dn = w, w["dw_w"].shape[-1], N_HEADS * HEAD_DIM
        self.h = _rms(x, w["g_mix"][...]).astype(BF16)

    def _cols(self, lo, n):
        return jnp.dot(self.h, self.w["w_in"][:, lo:lo + n], preferred_element_type=F32)

    def glu(self):
        ab = self._cols(0, 2 * self.d_conv)
        return ab[:, :self.d_conv] * jax.nn.sigmoid(ab[:, self.d_conv:])

    def c_gate(self):
        return self._cols(2 * self.d_conv, self.d_conv)

    def qkv(self, part):
        return self._cols(3 * self.d_conv + part * self.d_dn, self.d_dn)

    def z(self):
        return self._cols(3 * self.d_conv + 3 * self.d_dn, self.d_dn)

    def beta_g(self):
        w = self.w
        tail = jnp.dot(self.h, w["w_tail"][...], preferred_element_type=F32)
        lane = lax.broadcasted_iota(jnp.int32, tail.shape, 1)
        beta = jax.nn.sigmoid(tail)
        g = -jnp.exp(w["a_log"][...]) * _softplus(tail + w["dt_bias"][...])
        return jnp.where(lane < N_HEADS, beta, jnp.where(lane < 2 * N_HEADS, g, 0.0))


def _conv_branch_tail(c, c_gate, w):
    c = c + w["dw_b"][...]
    cc = c - jnp.mean(c, axis=-1, keepdims=True)
    c = cc * lax.rsqrt(jnp.mean(cc * cc, axis=-1, keepdims=True) + EPS) * w["ln_g"][...] + w["ln_b"][...]
    c = _silu(c)
    return _mm_w(c, w["pw"][...]) * _silu(c_gate)


def _qkv_heads(qkv_c):
    d_dn = N_HEADS * HEAD_DIM
    qs, ks, vs = [], [], []
    for h in range(N_HEADS):
        q = _silu(_head(qkv_c, h))
        k = _silu(_head(qkv_c[:, d_dn:2 * d_dn], h))
        v = _silu(_head(qkv_c[:, 2 * d_dn:], h))
        q = q * (lax.rsqrt(jnp.sum(q * q, axis=-1, keepdims=True) + EPS) * (HEAD_DIM ** -0.5))
        k = k * lax.rsqrt(jnp.sum(k * k, axis=-1, keepdims=True) + EPS)
        qs.append(q); ks.append(k); vs.append(v)
        yield
    return qs, ks, vs


def _chunk_masks(rows, chunk):
    ri = lax.broadcasted_iota(jnp.int32, (rows, rows), 0)
    ci = lax.broadcasted_iota(jnp.int32, (rows, rows), 1)
    same = (ri // chunk) == (ci // chunk)
    incl = same & (ci <= ri)
    strict = same & (ci < ri)
    return incl, strict


def _tri_inverse(a_list, chunk):
    rows = a_list[0].shape[0]
    nc = rows // chunk
    hs = range(len(a_list))
    pi = lax.broadcasted_iota(jnp.int32, (chunk, rows), 0)
    pl_ = lax.broadcasted_iota(jnp.int32, (chunk, rows), 1)
    lane_in, lane_blk = pl_ % chunk, pl_ // chunk
    eye_pan = (lane_in == pi).astype(F32)
    packed = chunk % 16 == 0
    blk_masks = [(lane_blk == c).astype(BF16 if packed else F32) for c in range(nc)]

    def fold(full):
        out = full[0:chunk]
        for c in range(1, nc):
            out = out + full[c * chunk:(c + 1) * chunk]
        return out

    def expand(pan):
        src = pan.astype(BF16) if packed else pan
        return jnp.concatenate([src * blk_masks[c] for c in range(nc)], axis=0).astype(BF16)

    def off_halves(size):
        return ((pi // size) == (lane_in // size)) & ((pi // (size // 2)) != (lane_in // (size // 2)))

    a_pan = [fold(a_list[h]) for h in hs]
    inv_pan = [eye_pan - jnp.where(off_halves(2), a_pan[h], 0.0) for h in hs]
    size = 4
    while size <= chunk:
        inv_full = [expand(inv_pan[h]) for h in hs]
        am_full = [expand(jnp.where(off_halves(size), a_pan[h], 0.0)) for h in hs]
        x_pan = [_mmb(inv_pan[h], am_full[h]) for h in hs]
        yield
        inv_pan = [inv_pan[h] - _mmb(x_pan[h], inv_full[h]) for h in hs]
        yield
        size *= 2
    return [expand(inv_pan[h]) for h in hs]


def _gate_scalars(bg, chunk):
    rows, lanes = bg.shape
    pos = lax.broadcasted_iota(jnp.int32, bg.shape, 0) % chunk
    gc, s = bg, 1
    while s < chunk:
        gc = gc + jnp.where(pos >= s, pltpu.roll(gc, s, 0), 0.0)
        s *= 2
    gtot = jnp.concatenate([jnp.broadcast_to(gc[e - 1:e, :], (chunk, lanes)) for e in range(chunk, rows + 1, chunk)],
                           axis=0)
    return gc, gc.T, gtot


def _chunk_local(qs, ks, vs, bg, chunk):
    incl, strict = _chunk_masks(bg.shape[0], chunk)
    gc, gct, gtot = _gate_scalars(bg, chunk)
    hs = range(N_HEADS)
    beta = [bg[:, h:h + 1] for h in hs]
    gcc = [gc[:, N_HEADS + h:N_HEADS + h + 1] for h in hs]
    gl = [gtot[:, N_HEADS + h:N_HEADS + h + 1] for h in hs]
    yield
    decay = [jnp.where(incl, jnp.exp(jnp.where(incl, gcc[h] - gct[N_HEADS + h:N_HEADS + h + 1, :], 0.0)), 0.0)
             for h in hs]
    kb = [ks[h] * beta[h] for h in hs]
    yield
    a = [jnp.where(strict, _mm_nt(kb[h], ks[h]) * decay[h], 0.0) for h in hs]
    yield
    qk = [_mm_nt(qs[h], ks[h]) * decay[h] for h in hs]
    egc = [jnp.exp(gcc[h]) for h in hs]
    rhs = [jnp.concatenate([vs[h] * beta[h], kb[h] * egc[h]], axis=1).astype(BF16) for h in hs]
    qd = [qs[h] * egc[h] for h in hs]
    kd = [ks[h] * jnp.exp(gl[h] - gcc[h]) for h in hs]
    yield
    tinv = yield from _tri_inverse(a, chunk)
    y = [_mmb(tinv[h], rhs[h]) for h in hs]
    yield
    u = [y[h][:, :HEAD_DIM] for h in hs]
    wk = [y[h][:, HEAD_DIM:] for h in hs]
    return u, wk, qk, qd, kd, gl, gtot


def _finish(x, c_out, o_heads, z, p_emb, w):
    outs = []
    for h in range(N_HEADS):
        o = o_heads[h]
        o = o * lax.rsqrt(jnp.mean(o * o, axis=-1, keepdims=True) + EPS) * w["dn_g"][...]
        outs.append(o * _silu(_head(z, h)))
    mix_in = jnp.concatenate([c_out] + outs, axis=1)
    x = x + _mm_w(mix_in, w["w_out"][...])
    yield
    gate = jax.nn.sigmoid(_mm_w(_rms(x, w["ple_g"][...]), w["ple_gate"][...]))
    yield
    x = x + gate * _mm_w(p_emb, w["ple_proj"][...])
    return _rms(x, w["fin_g"][...])


_WEIGHT_NAMES = ("g_mix", "w_in", "w_tail", "dw_w", "dw_b", "ln_g", "ln_b", "pw", "dn_w", "a_log",
                 "dt_bias", "dn_g", "w_out", "ple_g", "ple_gate", "ple_proj", "fin_g")


def _prompt_kernel(*refs, tile, chunk, nt):
    x_ref, xb_ref, pb_ref = refs[0], refs[1], refs[2]
    nw = len(_WEIGHT_NAMES)
    w = dict(zip(_WEIGHT_NAMES, refs[3:3 + nw]))
    y_ref, nconv_ref, ndn_ref, ns_ref = refs[3 + nw:7 + nw]
    (ubuf, qbuf, cbuf, s_scr, qcv, cring, zring, st_q, st_k, st_v, st_bg,
     s2_u, s2_w, s2_qd, s2_qk, s2_kt, s2_gl) = refs[7 + nw:]
    s = pl.program_id(0)
    slot_new, slot_old = s % 3, (s + 1) % 3
    t_a = s % nt
    t_c = (s + nt - 2) % nt
    kw = w["dw_w"].shape[0]
    ks = w["dn_w"].shape[0]
    hist, qhist = kw - 1, ks - 1
    upad, qpad = ubuf.shape[1] - tile, qbuf.shape[1] - tile
    hs = range(N_HEADS)
    half = HEAD_DIM // 2

    @pl.when(s == 0)
    def _():
        for ref in (cring, zring, st_q, st_k, st_v, st_bg, s_scr, s2_u, s2_w, s2_qd, s2_qk, s2_kt, s2_gl):
            ref[...] = jnp.zeros(ref.shape, F32)

    @pl.when(t_a == 0)
    def _():
        ubuf[:, 0:upad, :] = jnp.zeros((ubuf.shape[0], upad, LANES), F32)
        qbuf[:, 0:qpad, :] = jnp.zeros((qbuf.shape[0], qpad, LANES), F32)

    def stage3():
        lane_c = lax.broadcasted_iota(jnp.int32, (chunk, HEAD_DIM), 1)
        lane_t = lax.broadcasted_iota(jnp.int32, (HEAD_DIM, HEAD_DIM), 1)
        hc = lambda h: slice(h * HEAD_DIM, (h + 1) * HEAD_DIM)
        zs = jnp.zeros((HEAD_DIM, HEAD_DIM), BF16)
        zv = jnp.zeros((chunk, HEAD_DIM), BF16)
        state = [jnp.where(t_c == 0, 0.0, s_scr[h]) for h in hs]
        o_rows = [[] for _ in hs]
        for n in range(tile // chunk):
            r = slice(n * chunk, (n + 1) * chunk)
            col, odd = divmod(n * chunk, HEAD_DIM)
            cs = slice(col * HEAD_DIM, (col + 1) * HEAD_DIM)
            pairs = range(0, N_HEADS, 2)
            pair_cols = lambda h0: slice(h0 * HEAD_DIM, (h0 + 2) * HEAD_DIM)
            m1 = {}
            for h0 in pairs:
                h1 = h0 + 1
                lhs1 = jnp.concatenate([s2_w[r, pair_cols(h0)], s2_qd[r, pair_cols(h0)]], axis=0)
                s0, s1 = state[h0].astype(BF16), state[h1].astype(BF16)
                sbd = jnp.concatenate([jnp.concatenate([s0, zs], axis=1), jnp.concatenate([zs, s1], axis=1)], axis=0)
                m1[h0] = _mmb(lhs1, sbd)
            yield
            new_state = list(state)
            decay_row = jnp.exp(s2_gl[n * chunk:n * chunk + 1, :])
            for h0 in pairs:
                h1 = h0 + 1
                v0 = s2_u[r, hc(h0)] - m1[h0][:chunk, :HEAD_DIM]
                v1 = s2_u[r, hc(h1)] - m1[h0][:chunk, HEAD_DIM:]
                vbd = jnp.concatenate([jnp.concatenate([v0.astype(BF16), zv], axis=1),
                                       jnp.concatenate([zv, v1.astype(BF16)], axis=1)], axis=0)
                k0, k1 = s2_kt[hc(h0), cs], s2_kt[hc(h1), cs]
                if odd:
                    kpair = jnp.where(lane_t < half, pltpu.roll(k0, half, 1), k1)
                else:
                    kpair = jnp.where(lane_t < half, k0, pltpu.roll(k1, half, 1))
                qpair = jnp.where(lane_c < half, s2_qk[r, hc(h0)], s2_qk[r, hc(h1)])
                m2 = _mmb(jnp.concatenate([qpair, kpair], axis=0), vbd)
                o_rows[h0].append(m1[h0][chunk:, :HEAD_DIM] + m2[:chunk, :HEAD_DIM])
                o_rows[h1].append(m1[h0][chunk:, HEAD_DIM:] + m2[:chunk, HEAD_DIM:])
                new_state[h0] = state[h0] * decay_row[:, N_HEADS + h0:N_HEADS + h0 + 1] + m2[chunk:, :HEAD_DIM]
                new_state[h1] = state[h1] * decay_row[:, N_HEADS + h1:N_HEADS + h1 + 1] + m2[chunk:, HEAD_DIM:]
            state = new_state
            yield
        for h in hs:
            s_scr[h] = state[h]
        return [jnp.concatenate(o_rows[h], axis=0) for h in hs]

    def stage3_out(o_heads):
        y_ref[0] = yield from _finish(xb_ref[0], cring[slot_old], o_heads, zring[slot_old], pb_ref[0], w)

    def stage2():
        heads_of = lambda ref: [ref[:, h * HEAD_DIM:(h + 1) * HEAD_DIM] for h in hs]
        uu2, wk2, qk2, qd2, kd2, _, gtot2 = yield from _chunk_local(
            heads_of(st_q), heads_of(st_k), heads_of(st_v), st_bg[...], chunk)
        folds = []
        for h in hs:
            f = qk2[h][:, 0:HEAD_DIM]
            for c in range(1, tile // HEAD_DIM):
                f = f + qk2[h][:, c * HEAD_DIM:(c + 1) * HEAD_DIM]
            folds.append(f + pltpu.roll(f, half, 1))
        yield
        return [(s2_u, jnp.concatenate(uu2, axis=1)), (s2_w, jnp.concatenate(wk2, axis=1)),
                (s2_qd, jnp.concatenate(qd2, axis=1)), (s2_qk, jnp.concatenate(folds, axis=1)),
                (s2_kt, jnp.concatenate([kd2[h].T for h in hs], axis=0)), (s2_gl, gtot2)]

    def stage1_project():
        proj = _Projection(x_ref[0], w)
        piece = 2 * LANES
        d_conv, d_dn = ubuf.shape[0] * LANES, N_HEADS * HEAD_DIM
        ab = []
        for lo in range(0, 2 * d_conv, piece):
            ab.append(proj._cols(lo, piece))
            yield
        n = len(ab) // 2
        for i in range(n):
            u = ab[i] * jax.nn.sigmoid(ab[n + i])
            for c in range(piece // LANES):
                ubuf[i * (piece // LANES) + c, upad:upad + tile, :] = u[:, _lanes(c)]
        gate = []
        for lo in range(2 * d_conv, 3 * d_conv, piece):
            gate.append(proj._cols(lo, piece))
            yield
        for i, lo in enumerate(range(3 * d_conv, 3 * d_conv + 3 * d_dn, piece)):
            cols = proj._cols(lo, piece)
            for c in range(piece // LANES):
                qbuf[i * (piece // LANES) + c, qpad:qpad + tile, :] = cols[:, _lanes(c)]
            yield
        for lo in range(0, d_dn, piece):
            zring[slot_new, :, lo:lo + piece] = proj._cols(3 * d_conv + 3 * d_dn + lo, piece)
            yield
        st_bg[...] = proj.beta_g()
        return jnp.concatenate(gate, axis=1)

    def stage1_conv(c_gate):
        starts = [(r0, p) for r0 in range(0, tile, 2 * CONV_ROWS) for p in range(2)]
        groups = [starts[:len(starts) // 2], starts[len(starts) // 2:]]
        for c in range(ubuf.shape[0]):
            for group in groups:
                accs = [jnp.zeros((CONV_ROWS, LANES), F32) for _ in group]
                for j in range(kw):
                    w_row = jnp.broadcast_to(w["dw_w"][j:j + 1, _lanes(c)], (CONV_ROWS, LANES))
                    for i, (r0, p) in enumerate(group):
                        win = ubuf[c, pl.ds(upad - hist + r0 + p + j, CONV_ROWS, stride=2), :]
                        accs[i] = accs[i] + win * w_row
                for (r0, p), acc in zip(group, accs):
                    cbuf[c, pl.ds(r0 + p, CONV_ROWS, stride=2), :] = acc
                yield
        c_raw = jnp.concatenate([cbuf[c] for c in range(cbuf.shape[0])], axis=1)
        cring[slot_new] = _conv_branch_tail(c_raw, c_gate, w)
        yield
        half_tile = tile // 2
        for c in range(qbuf.shape[0]):
            for p in range(2):
                acc = jnp.zeros((half_tile, LANES), F32)
                for j in range(ks):
                    win = qbuf[c, pl.ds(qpad - qhist + p + j, half_tile, stride=2), :]
                    acc = acc + win * w["dn_w"][j:j + 1, _lanes(c)]
                qcv[c, pl.ds(p, half_tile, stride=2), :] = acc
            if c % 3 == 2:
                yield
        qkv_c = jnp.concatenate([qcv[c] for c in range(qcv.shape[0])], axis=1)
        qa, ka, va = yield from _qkv_heads(qkv_c)
        st_q[...] = jnp.concatenate(qa, axis=1)
        st_k[...] = jnp.concatenate(ka, axis=1)
        st_v[...] = jnp.concatenate(va, axis=1)
        for c in range(ubuf.shape[0]):
            ubuf[c, 0:upad, :] = ubuf[c, tile:tile + upad, :]
        for c in range(qbuf.shape[0]):
            qbuf[c, 0:qpad, :] = qbuf[c, tile:tile + qpad, :]

    o_heads, delta_local, c_gate = _interleave([stage3(), stage2(), stage1_project()], [8, 16, 15])
    for ref, value in delta_local:
        ref[...] = value
    _interleave([stage3_out(o_heads), stage1_conv(c_gate)], [3, 17])

    @pl.when((t_a == nt - 1) & (s < pl.num_programs(0) - 2))
    def _():
        nconv_ref[0] = jnp.concatenate([ubuf[c, upad + tile - hist:upad + tile, :] for c in range(ubuf.shape[0])],
                                       axis=1)
        ndn_ref[0] = jnp.concatenate([qbuf[c, qpad + tile - qhist:qpad + tile, :] for c in range(qbuf.shape[0])],
                                     axis=1)

    @pl.when((t_c == nt - 1) & (s > 1))
    def _():
        ns_ref[0] = s_scr[...]


def _sample_kernel(*refs, seqs, steps):
    x_ref, p_ref, sc_ref, sdn_ref, s_ref = refs[:5]
    nw = len(_WEIGHT_NAMES)
    w = dict(zip(_WEIGHT_NAMES, refs[5:5 + nw]))
    y_ref, nconv_ref, ndn_ref, ns_ref = refs[5 + nw:9 + nw]
    ustage, cstage, qstage, qcs = refs[9 + nw:]
    rows = seqs * steps
    kw = w["dw_w"].shape[0]
    ks = w["dn_w"].shape[0]
    hist, qhist = sc_ref.shape[0], sdn_ref.shape[0]
    by_time = lambda ref, c, t: ref[c, pl.ds(t, seqs, stride=steps), :]

    x = x_ref[...]
    proj = _Projection(x, w)
    hs = range(N_HEADS)

    def conv_branch():
        u = proj.glu()
        n_uc = ustage.shape[0]
        for c in range(n_uc):
            ustage[c] = u[:, _lanes(c)]
        u_tm = [[by_time(ustage, c, t) for c in range(n_uc)] for t in range(steps)]
        yield
        for t in range(steps):
            for c in range(n_uc):
                acc = jnp.zeros((seqs, LANES), F32)
                for j in range(kw):
                    i = t + j
                    src = sc_ref[i, :, _lanes(c)] if i < hist else u_tm[i - hist][c]
                    acc = acc + src * w["dw_w"][j:j + 1, _lanes(c)]
                cstage[c, pl.ds(t, seqs, stride=steps), :] = acc
            yield
        for i in range(hist):
            nconv_ref[i] = sc_ref[i + steps] if i + steps < hist else jnp.concatenate(u_tm[i + steps - hist], axis=1)
        c_gate = proj.c_gate()
        yield
        return _conv_branch_tail(jnp.concatenate([cstage[c] for c in range(n_uc)], axis=1), c_gate, w)

    def delta_branch():
        n_qc = qstage.shape[0]
        for part in range(3):
            qkv_p = proj.qkv(part)
            for c in range(N_HEADS):
                qstage[part * N_HEADS + c] = qkv_p[:, _lanes(c)]
            yield
        q_tm = [[by_time(qstage, c, t) for c in range(n_qc)] for t in range(steps)]
        for t in range(steps):
            for c in range(n_qc):
                acc = jnp.zeros((seqs, LANES), F32)
                for j in range(ks):
                    i = t + j
                    src = sdn_ref[i, :, _lanes(c)] if i < qhist else q_tm[i - qhist][c]
                    acc = acc + src * w["dn_w"][j:j + 1, _lanes(c)]
                qcs[c, pl.ds(t, seqs, stride=steps), :] = acc
            if t % 2:
                yield
        for i in range(qhist):
            ndn_ref[i] = jnp.concatenate(q_tm[steps - qhist + i], axis=1)
        qs, kss, vs = yield from _qkv_heads(jnp.concatenate([qcs[c] for c in range(n_qc)], axis=1))
        bg = proj.beta_g()
        yield
        uu, wk, qk, qd, kd, gl, _ = yield from _chunk_local(qs, kss, vs, bg, steps)
        seq_of_col = lax.broadcasted_iota(jnp.int32, (seqs, 1, rows), 2) // steps
        seq_id = lax.broadcasted_iota(jnp.int32, (seqs, 1, rows), 0)
        col_mask = (seq_of_col == seq_id).astype(F32)
        s_old = [s_ref[:, h] for h in hs]
        m1 = []
        for h in hs:
            lhs = jnp.concatenate([wk[h].reshape(seqs, steps, HEAD_DIM), qd[h].reshape(seqs, steps, HEAD_DIM)], axis=1)
            m1.append(lax.dot_general(lhs.astype(BF16), s_old[h].astype(BF16), (((2,), (1,)), ((0,), (0,))),
                                      preferred_element_type=F32))
        yield
        v_new = [uu[h] - m1[h][:, :steps, :].reshape(rows, HEAD_DIM) for h in hs]
        o_heads = [m1[h][:, steps:, :].reshape(rows, HEAD_DIM) + _mm(qk[h], v_new[h]) for h in hs]
        yield
        for h in hs:
            kd_rows = (kd[h].T[None, :, :] * col_mask).reshape(seqs * HEAD_DIM, rows)
            ds = _mm(kd_rows, v_new[h]).reshape(seqs, HEAD_DIM, HEAD_DIM)
            gl_seq = jnp.exp(gl[h].reshape(seqs, steps, 1)[:, 0:1, :])
            ns_ref[:, h] = s_old[h] * gl_seq + ds
        return o_heads

    c_out, o_heads = _interleave([conv_branch(), delta_branch()], [10, 24])
    z = proj.z()
    y_ref[...] = _run(_finish(x, c_out, o_heads, z, p_ref[...], w))


def _full_spec(a):
    nd = a.ndim
    return pl.BlockSpec(a.shape, lambda *_: (0,) * nd, pipeline_mode=pl.Buffered(1))


def kernel(x_prompt, x_sample, state_conv, state_dn_conv, state_dn_S, p_prompt, p_sample, norm_mix_g, w_in, conv_dw_w, conv_dw_b, conv_ln_g, conv_ln_b, conv_pw_w, dn_conv_w, dn_a_log, dn_dt_bias, dn_norm_g, w_out, ple_norm_g, ple_gate_w, ple_proj_w, final_norm_g):
    depth = w_in.shape[0]
    assert depth == 1, "single trunk layer"
    bsz, seqlen, d_model = x_prompt.shape
    dec_b, dec_l, _ = x_sample.shape
    d_conv = conv_dw_w.shape[-1]
    d_dn = N_HEADS * HEAD_DIM
    kw, ks = conv_dw_w.shape[1], dn_conv_w.shape[1]
    d_main = 3 * d_conv + 4 * d_dn
    assert w_in.shape[-1] == d_main + 2 * N_HEADS
    assert dn_conv_w.shape[-1] == 3 * d_dn and dn_norm_g.shape[-1] == HEAD_DIM
    tile, chunk = min(PROMPT_TILE, seqlen), min(PROMPT_CHUNK, seqlen)
    assert seqlen % tile == 0 and tile % chunk == 0 and tile % (2 * CONV_ROWS) == 0
    assert dec_b % SAMPLE_SEQS == 0 and SAMPLE_SEQS % SUBLANES == 0 and dec_l == SUBLANES
    assert 2 * chunk == HEAD_DIM and tile % HEAD_DIM == 0

    row = lambda v: v.reshape(1, -1)
    head_lanes = lambda v: jnp.pad(v, (N_HEADS, LANES - 2 * N_HEADS)).reshape(1, LANES)
    weights = dict(
        g_mix=row(norm_mix_g[0]),
        w_in=w_in[0].astype(BF16),
        w_tail=jnp.pad(w_in[0, :, d_main:].astype(BF16), ((0, 0), (0, LANES - 2 * N_HEADS))),
        dw_w=conv_dw_w[0], dw_b=row(conv_dw_b[0]), ln_g=row(conv_ln_g[0]), ln_b=row(conv_ln_b[0]),
        pw=conv_pw_w[0],
        dn_w=dn_conv_w[0],
        a_log=head_lanes(dn_a_log[0]),
        dt_bias=head_lanes(dn_dt_bias[0]),
        dn_g=row(dn_norm_g[0]),
        w_out=w_out[0],
        ple_g=row(ple_norm_g[0]), ple_gate=ple_gate_w[0], ple_proj=ple_proj_w[0],
        fin_g=row(final_norm_g),
    )
    wlist = [weights[n] for n in _WEIGHT_NAMES]
    wspecs = [_full_spec(a) for a in wlist]
    params = dict(vmem_limit_bytes=VMEM_LIMIT_BYTES)

    nt = seqlen // tile
    upad = pl.cdiv(kw - 1, SUBLANES) * SUBLANES
    qpad = pl.cdiv(ks - 1, SUBLANES) * SUBLANES
    assert nt > 1
    n_tiles = bsz * nt
    front = lambda s: (jnp.minimum(s, n_tiles - 1) // nt, jnp.minimum(s, n_tiles - 1) % nt)
    back = lambda s: (jnp.maximum(s - 2, 0) // nt, jnp.maximum(s - 2, 0) % nt)
    stage = lambda cols: pltpu.VMEM((tile, cols), F32)
    slabs = lambda cols, rows: pltpu.VMEM((cols // LANES, rows, LANES), F32)
    y_p, nconv_p, ndn_p, ns_p = pl.pallas_call(
        functools.partial(_prompt_kernel, tile=tile, chunk=chunk, nt=nt),
        grid=(n_tiles + 2,),
        in_specs=[pl.BlockSpec((1, tile, d_model), lambda s: (*front(s), 0)),
                  pl.BlockSpec((1, tile, d_model), lambda s: (*back(s), 0)),
                  pl.BlockSpec((1, tile, p_prompt.shape[-1]), lambda s: (*back(s), 0))] + wspecs,
        out_specs=[pl.BlockSpec((1, tile, d_model), lambda s: (*back(s), 0)),
                   pl.BlockSpec((1, kw - 1, d_conv), lambda s: (front(s)[0], 0, 0)),
                   pl.BlockSpec((1, ks - 1, 3 * d_dn), lambda s: (front(s)[0], 0, 0)),
                   pl.BlockSpec((1, N_HEADS, HEAD_DIM, HEAD_DIM), lambda s: (back(s)[0], 0, 0, 0))],
        out_shape=[jax.ShapeDtypeStruct((bsz, seqlen, d_model), F32),
                   jax.ShapeDtypeStruct((bsz, kw - 1, d_conv), F32),
                   jax.ShapeDtypeStruct((bsz, ks - 1, 3 * d_dn), F32),
                   jax.ShapeDtypeStruct((bsz, N_HEADS, HEAD_DIM, HEAD_DIM), F32)],
        scratch_shapes=[slabs(d_conv, upad + tile),
                        slabs(3 * d_dn, qpad + tile),
                        slabs(d_conv, tile),
                        pltpu.VMEM((N_HEADS, HEAD_DIM, HEAD_DIM), F32),
                        slabs(3 * d_dn, tile),
                        pltpu.VMEM((3, tile, d_conv), F32),
                        pltpu.VMEM((3, tile, d_dn), F32),
                        stage(d_dn), stage(d_dn), stage(d_dn), stage(LANES),
                        stage(d_dn), stage(d_dn), stage(d_dn), stage(d_dn),
                        pltpu.VMEM((d_dn, tile), F32), stage(LANES)],
        compiler_params=pltpu.CompilerParams(dimension_semantics=("arbitrary",), **params),
        name="prompt_layer",
    )(x_prompt, x_prompt, p_prompt[0], *wlist)

    seqs = SAMPLE_SEQS
    rows = seqs * dec_l
    xs = x_sample.reshape(dec_b * dec_l, d_model)
    ps = p_sample[0].reshape(dec_b * dec_l, -1)
    tm = lambda a: jnp.transpose(a, (0, 2, 1, 3))
    conv_spec = pl.BlockSpec((None, kw - 1, seqs, d_conv), lambda i: (0, 0, i, 0))
    dn_spec = pl.BlockSpec((None, ks - 1, seqs, 3 * d_dn), lambda i: (0, 0, i, 0))
    state_spec = pl.BlockSpec((None, seqs, N_HEADS, HEAD_DIM, HEAD_DIM), lambda i: (0, i, 0, 0, 0))
    y_s, nconv_s, ndn_s, ns_s = pl.pallas_call(
        functools.partial(_sample_kernel, seqs=seqs, steps=dec_l),
        grid=(dec_b // seqs,),
        in_specs=[pl.BlockSpec((rows, d_model), lambda i: (i, 0)),
                  pl.BlockSpec((rows, ps.shape[-1]), lambda i: (i, 0)),
                  conv_spec, dn_spec, state_spec] + wspecs,
        out_specs=[pl.BlockSpec((rows, d_model), lambda i: (i, 0)), conv_spec, dn_spec, state_spec],
        out_shape=[jax.ShapeDtypeStruct((dec_b * dec_l, d_model), F32),
                   jax.ShapeDtypeStruct((1, kw - 1, dec_b, d_conv), F32),
                   jax.ShapeDtypeStruct((1, ks - 1, dec_b, 3 * d_dn), F32),
                   jax.ShapeDtypeStruct((1, dec_b, N_HEADS, HEAD_DIM, HEAD_DIM), F32)],
        scratch_shapes=[slabs(d_conv, rows), slabs(d_conv, rows), slabs(3 * d_dn, rows), slabs(3 * d_dn, rows)],
        compiler_params=pltpu.CompilerParams(dimension_semantics=("arbitrary",), **params),
        name="sample_layer",
    )(xs, ps, tm(state_conv), tm(state_dn_conv), state_dn_S, *wlist)

    return (y_p, y_s.reshape(dec_b, dec_l, d_model), nconv_p[None], ndn_p[None], ns_p[None],
            tm(nconv_s), tm(ndn_s), ns_s)
```

```python
import functools

import jax
import jax.numpy as jnp
from jax import lax
from jax.experimental import pallas as pl
from jax.experimental.pallas import tpu as pltpu

EPS = 1e-6
N_HEADS = 4
HEAD_DIM = 128
LANES = 128
SUBLANES = 8
PROMPT_TILE = 256
PROMPT_CHUNK = 64
SAMPLE_SEQS = 16
CONV_ROWS = 32
VMEM_LIMIT_BYTES = 56 * 1024 * 1024

F32 = jnp.float32
BF16 = jnp.bfloat16


def _run(gen):
    try:
        while True:
            next(gen)
    except StopIteration as stop:
        return stop.value


def _interleave(gens, weights):
    n = len(gens)
    done, alive, out = [0] * n, [True] * n, [None] * n
    while any(alive):
        k = min((i for i in range(n) if alive[i]), key=lambda i: (done[i] + 1) / weights[i])
        try:
            next(gens[k])
            done[k] += 1
        except StopIteration as stop:
            out[k], alive[k] = stop.value, False
    return out


def _mm(a, b):
    return jnp.dot(a.astype(BF16), b.astype(BF16), preferred_element_type=F32)


def _mm_w(a, weight):
    return jnp.dot(a, weight, preferred_element_type=F32)


def _mm_nt(a, b):
    return lax.dot_general(a.astype(BF16), b.astype(BF16), (((1,), (1,)), ((), ())),
                           preferred_element_type=F32)


def _mmb(a, b_bf16):
    if a.shape[0] % 16:
        return jnp.dot(a.astype(F32), b_bf16.astype(F32), preferred_element_type=F32)
    return jnp.dot(a.astype(BF16), b_bf16, preferred_element_type=F32)


def _rms(x, g):
    return x * lax.rsqrt(jnp.mean(x * x, axis=-1, keepdims=True) + EPS) * g


def _silu(x):
    return x * jax.nn.sigmoid(x)


def _softplus(x):
    return jnp.maximum(x, 0.0) + jnp.log(1.0 + jnp.exp(-jnp.abs(x)))


def _head(a, h):
    return a[:, h * HEAD_DIM:(h + 1) * HEAD_DIM]


def _lanes(c):
    return slice(c * LANES, (c + 1) * LANES)


class _Projection:
    def __init__(self, x, w):
        self.w, self.d_conv, self.d_dn = w, w["dw_w"].shape[-1], N_HEADS * HEAD_DIM
        self.h = _rms(x, w["g_mix"][...]).astype(BF16)

    def _cols(self, lo, n):
        return jnp.dot(self.h, self.w["w_in"][:, lo:lo + n], preferred_element_type=F32)

    def glu(self):
        ab = self._cols(0, 2 * self.d_conv)
        return ab[:, :self.d_conv] * jax.nn.sigmoid(ab[:, self.d_conv:])

    def c_gate(self):
        return self._cols(2 * self.d_conv, self.d_conv)

    def qkv(self, part):
        return self._cols(3 * self.d_conv + part * self.d_dn, self.d_dn)

    def z(self):
        return self._cols(3 * self.d_conv + 3 * self.d_dn, self.d_dn)

    def beta_g(self):
        w = self.w
        tail = jnp.dot(self.h, w["w_tail"][...], preferred_element_type=F32)
        lane = lax.broadcasted_iota(jnp.int32, tail.shape, 1)
        beta = jax.nn.sigmoid(tail)
        g = -jnp.exp(w["a_log"][...]) * _softplus(tail + w["dt_bias"][...])
        return jnp.where(lane < N_HEADS, beta, jnp.where(lane < 2 * N_HEADS, g, 0.0))


def _conv_branch_tail(c, c_gate, w):
    c = c + w["dw_b"][...]
    cc = c - jnp.mean(c, axis=-1, keepdims=True)
    c = cc * lax.rsqrt(jnp.mean(cc * cc, axis=-1, keepdims=True) + EPS) * w["ln_g"][...] + w["ln_b"][...]
    c = _silu(c)
    return _mm_w(c, w["pw"][...]) * _silu(c_gate)


def _qkv_heads(qkv_c):
    d_dn = N_HEADS * HEAD_DIM
    qs, ks, vs = [], [], []
    for h in range(N_HEADS):
        q = _silu(_head(qkv_c, h))
        k = _silu(_head(qkv_c[:, d_dn:2 * d_dn], h))
        v = _silu(_head(qkv_c[:, 2 * d_dn:], h))
        q = q * (lax.rsqrt(jnp.sum(q * q, axis=-1, keepdims=True) + EPS) * (HEAD_DIM ** -0.5))
        k = k * lax.rsqrt(jnp.sum(k * k, axis=-1, keepdims=True) + EPS)
        qs.append(q); ks.append(k); vs.append(v)
        yield
    return qs, ks, vs


def _chunk_masks(rows, chunk):
    ri = lax.broadcasted_iota(jnp.int32, (rows, rows), 0)
    ci = lax.broadcasted_iota(jnp.int32, (rows, rows), 1)
    same = (ri // chunk) == (ci // chunk)
    incl = same & (ci <= ri)
    strict = same & (ci < ri)
    return incl, strict


def _tri_inverse(a_list, chunk):
    rows = a_list[0].shape[0]
    nc = rows // chunk
    hs = range(len(a_list))
    pi = lax.broadcasted_iota(jnp.int32, (chunk, rows), 0)
    pl_ = lax.broadcasted_iota(jnp.int32, (chunk, rows), 1)
    lane_in, lane_blk = pl_ % chunk, pl_ // chunk
    eye_pan = (lane_in == pi).astype(F32)
    packed = chunk % 16 == 0
    blk_masks = [(lane_blk == c).astype(BF16 if packed else F32) for c in range(nc)]

    def fold(full):
        out = full[0:chunk]
        for c in range(1, nc):
            out = out + full[c * chunk:(c + 1) * chunk]
        return out

    def expand(pan):
        src = pan.astype(BF16) if packed else pan
        return jnp.concatenate([src * blk_masks[c] for c in range(nc)], axis=0).astype(BF16)

    def off_halves(size):
        return ((pi // size) == (lane_in // size)) & ((pi // (size // 2)) != (lane_in // (size // 2)))

    a_pan = [fold(a_list[h]) for h in hs]
    inv_pan = [eye_pan - jnp.where(off_halves(2), a_pan[h], 0.0) for h in hs]
    size = 4
    while size <= chunk:
        inv_full = [expand(inv_pan[h]) for h in hs]
        am_full = [expand(jnp.where(off_halves(size), a_pan[h], 0.0)) for h in hs]
        x_pan = [_mmb(inv_pan[h], am_full[h]) for h in hs]
        yield
        inv_pan = [inv_pan[h] - _mmb(x_pan[h], inv_full[h]) for h in hs]
        yield
        size *= 2
    return [expand(inv_pan[h]) for h in hs]


def _gate_scalars(bg, chunk):
    rows, lanes = bg.shape
    pos = lax.broadcasted_iota(jnp.int32, bg.shape, 0) % chunk
    gc, s = bg, 1
    while s < chunk:
        gc = gc + jnp.where(pos >= s, pltpu.roll(gc, s, 0), 0.0)
        s *= 2
    gtot = jnp.concatenate([jnp.broadcast_to(gc[e - 1:e, :], (chunk, lanes)) for e in range(chunk, rows + 1, chunk)],
                           axis=0)
    return gc, gc.T, gtot


def _chunk_local(qs, ks, vs, bg, chunk):
    incl, strict = _chunk_masks(bg.shape[0], chunk)
    gc, gct, gtot = _gate_scalars(bg, chunk)
    hs = range(N_HEADS)
    beta = [bg[:, h:h + 1] for h in hs]
    gcc = [gc[:, N_HEADS + h:N_HEADS + h + 1] for h in hs]
    gl = [gtot[:, N_HEADS + h:N_HEADS + h + 1] for h in hs]
    yield
    decay = [jnp.where(incl, jnp.exp(jnp.where(incl, gcc[h] - gct[N_HEADS + h:N_HEADS + h + 1, :], 0.0)), 0.0)
             for h in hs]
    kb = [ks[h] * beta[h] for h in hs]
    yield
    a = [jnp.where(strict, _mm_nt(kb[h], ks[h]) * decay[h], 0.0) for h in hs]
    yield
    qk = [_mm_nt(qs[h], ks[h]) * decay[h] for h in hs]
    egc = [jnp.exp(gcc[h]) for h in hs]
    rhs = [jnp.concatenate([vs[h] * beta[h], kb[h] * egc[h]], axis=1).astype(BF16) for h in hs]
    qd = [qs[h] * egc[h] for h in hs]
    kd = [ks[h] * jnp.exp(gl[h] - gcc[h]) for h in hs]
    yield
    tinv = yield from _tri_inverse(a, chunk)
    y = [_mmb(tinv[h], rhs[h]) for h in hs]
    yield
    u = [y[h][:, :HEAD_DIM] for h in hs]
    wk = [y[h][:, HEAD_DIM:] for h in hs]
    return u, wk, qk, qd, kd, gl, gtot


def _finish(x, c_out, o_heads, z, p_emb, w):
    outs = []
    for h in range(N_HEADS):
        o = o_heads[h]
        o = o * lax.rsqrt(jnp.mean(o * o, axis=-1, keepdims=True) + EPS) * w["dn_g"][...]
        outs.append(o * _silu(_head(z, h)))
    mix_in = jnp.concatenate([c_out] + outs, axis=1)
    x = x + _mm_w(mix_in, w["w_out"][...])
    yield
    gate = jax.nn.sigmoid(_mm_w(_rms(x, w["ple_g"][...]), w["ple_gate"][...]))
    yield
    x = x + gate * _mm_w(p_emb, w["ple_proj"][...])
    return _rms(x, w["fin_g"][...])


_WEIGHT_NAMES = ("g_mix", "w_in", "w_tail", "dw_w", "dw_b", "ln_g", "ln_b", "pw", "dn_w", "a_log",
                 "dt_bias", "dn_g", "w_out", "ple_g", "ple_gate", "ple_proj", "fin_g")


def _prompt_kernel(*refs, tile, chunk, nt):
    x_ref, xb_ref, pb_ref = refs[0], refs[1], refs[2]
    nw = len(_WEIGHT_NAMES)
    w = dict(zip(_WEIGHT_NAMES, refs[3:3 + nw]))
    y_ref, nconv_ref, ndn_ref, ns_ref = refs[3 + nw:7 + nw]
    (ubuf, qbuf, cbuf, s_scr, qcv, cring, zring, st_q, st_k, st_v, st_bg,
     s2_u, s2_w, s2_qd, s2_qk, s2_kt, s2_gl) = refs[7 + nw:]
    s = pl.program_id(0)
    slot_new, slot_old = s % 3, (s + 1) % 3
    t_a = s % nt
    t_c = (s + nt - 2) % nt
    kw = w["dw_w"].shape[0]
    ks = w["dn_w"].shape[0]
    hist, qhist = kw - 1, ks - 1
    upad, qpad = ubuf.shape[1] - tile, qbuf.shape[1] - tile
    hs = range(N_HEADS)
    half = HEAD_DIM // 2

    @pl.when(s == 0)
    def _():
        for ref in (cring, zring, st_q, st_k, st_v, st_bg, s_scr, s2_u, s2_w, s2_qd, s2_qk, s2_kt, s2_gl,
                    nconv_ref, ndn_ref):
            ref[...] = jnp.zeros(ref.shape, F32)

    @pl.when(t_a == 0)
    def _():
        ubuf[:, 0:upad, :] = jnp.zeros((ubuf.shape[0], upad, LANES), F32)
        qbuf[:, 0:qpad, :] = jnp.zeros((qbuf.shape[0], qpad, LANES), F32)

    def stage3():
        lane_c = lax.broadcasted_iota(jnp.int32, (chunk, HEAD_DIM), 1)
        lane_t = lax.broadcasted_iota(jnp.int32, (HEAD_DIM, HEAD_DIM), 1)
        hc = lambda h: slice(h * HEAD_DIM, (h + 1) * HEAD_DIM)
        zs = jnp.zeros((HEAD_DIM, HEAD_DIM), BF16)
        zv = jnp.zeros((chunk, HEAD_DIM), BF16)
        state = [jnp.where(t_c == 0, 0.0, s_scr[h]) for h in hs]
        o_rows = [[] for _ in hs]
        for n in range(tile // chunk):
            r = slice(n * chunk, (n + 1) * chunk)
            col, odd = divmod(n * chunk, HEAD_DIM)
            cs = slice(col * HEAD_DIM, (col + 1) * HEAD_DIM)
            pairs = range(0, N_HEADS, 2)
            pair_cols = lambda h0: slice(h0 * HEAD_DIM, (h0 + 2) * HEAD_DIM)
            m1 = {}
            for h0 in pairs:
                h1 = h0 + 1
                lhs1 = jnp.concatenate([s2_w[r, pair_cols(h0)], s2_qd[r, pair_cols(h0)]], axis=0)
                s0, s1 = state[h0].astype(BF16), state[h1].astype(BF16)
                sbd = jnp.concatenate([jnp.concatenate([s0, zs], axis=1), jnp.concatenate([zs, s1], axis=1)], axis=0)
                m1[h0] = _mmb(lhs1, sbd)
            yield
            new_state = list(state)
            decay_row = jnp.exp(s2_gl[n * chunk:n * chunk + 1, :])
            for h0 in pairs:
                h1 = h0 + 1
                v0 = s2_u[r, hc(h0)] - m1[h0][:chunk, :HEAD_DIM]
                v1 = s2_u[r, hc(h1)] - m1[h0][:chunk, HEAD_DIM:]
                vbd = jnp.concatenate([jnp.concatenate([v0.astype(BF16), zv], axis=1),
                                       jnp.concatenate([zv, v1.astype(BF16)], axis=1)], axis=0)
                k0, k1 = s2_kt[hc(h0), cs], s2_kt[hc(h1), cs]
                if odd:
                    kpair = jnp.where(lane_t < half, pltpu.roll(k0, half, 1), k1)
                else:
                    kpair = jnp.where(lane_t < half, k0, pltpu.roll(k1, half, 1))
                qpair = jnp.where(lane_c < half, s2_qk[r, hc(h0)], s2_qk[r, hc(h1)])
                m2 = _mmb(jnp.concatenate([qpair, kpair], axis=0), vbd)
                o_rows[h0].append(m1[h0][chunk:, :HEAD_DIM] + m2[:chunk, :HEAD_DIM])
                o_rows[h1].append(m1[h0][chunk:, HEAD_DIM:] + m2[:chunk, HEAD_DIM:])
                new_state[h0] = state[h0] * decay_row[:, N_HEADS + h0:N_HEADS + h0 + 1] + m2[chunk:, :HEAD_DIM]
                new_state[h1] = state[h1] * decay_row[:, N_HEADS + h1:N_HEADS + h1 + 1] + m2[chunk:, HEAD_DIM:]
            state = new_state
            yield
        for h in hs:
            s_scr[h] = state[h]
        return [jnp.concatenate(o_rows[h], axis=0) for h in hs]

    def stage3_out(o_heads):
        y_ref[0] = yield from _finish(xb_ref[0], cring[slot_old], o_heads, zring[slot_old], pb_ref[0], w)

    def stage2():
        heads_of = lambda ref: [ref[:, h * HEAD_DIM:(h + 1) * HEAD_DIM] for h in hs]
        uu2, wk2, qk2, qd2, kd2, _, gtot2 = yield from _chunk_local(
            heads_of(st_q), heads_of(st_k), heads_of(st_v), st_bg[...], chunk)
        folds = []
        for h in hs:
            f = qk2[h][:, 0:HEAD_DIM]
            for c in range(1, tile // HEAD_DIM):
                f = f + qk2[h][:, c * HEAD_DIM:(c + 1) * HEAD_DIM]
            folds.append(f + pltpu.roll(f, half, 1))
        yield
        return [(s2_u, jnp.concatenate(uu2, axis=1)), (s2_w, jnp.concatenate(wk2, axis=1)),
                (s2_qd, jnp.concatenate(qd2, axis=1)), (s2_qk, jnp.concatenate(folds, axis=1)),
                (s2_kt, jnp.concatenate([kd2[h].T for h in hs], axis=0)), (s2_gl, gtot2)]

    def stage1_project():
        proj = _Projection(x_ref[0], w)
        piece = 2 * LANES
        d_conv, d_dn = ubuf.shape[0] * LANES, N_HEADS * HEAD_DIM
        ab = []
        for lo in range(0, 2 * d_conv, piece):
            ab.append(proj._cols(lo, piece))
            yield
        n = len(ab) // 2
        for i in range(n):
            u = ab[i] * jax.nn.sigmoid(ab[n + i])
            for c in range(piece // LANES):
                ubuf[i * (piece // LANES) + c, upad:upad + tile, :] = u[:, _lanes(c)]
        gate = []
        for lo in range(2 * d_conv, 3 * d_conv, piece):
            gate.append(proj._cols(lo, piece))
            yield
        for i, lo in enumerate(range(3 * d_conv, 3 * d_conv + 3 * d_dn, piece)):
            cols = proj._cols(lo, piece)
            for c in range(piece // LANES):
                qbuf[i * (piece // LANES) + c, qpad:qpad + tile, :] = cols[:, _lanes(c)]
            yield
        for lo in range(0, d_dn, piece):
            zring[slot_new, :, lo:lo + piece] = proj._cols(3 * d_conv + 3 * d_dn + lo, piece)
            yield
        st_bg[...] = proj.beta_g()
        return jnp.concatenate(gate, axis=1)

    def stage1_conv(c_gate):
        starts = [(r0, p) for r0 in range(0, tile, 2 * CONV_ROWS) for p in range(2)]
        groups = [starts[:len(starts) // 2], starts[len(starts) // 2:]]
        for c in range(ubuf.shape[0]):
            for group in groups:
                accs = [jnp.zeros((CONV_ROWS, LANES), F32) for _ in group]
                for j in range(kw):
                    w_row = jnp.broadcast_to(w["dw_w"][j:j + 1, _lanes(c)], (CONV_ROWS, LANES))
                    for i, (r0, p) in enumerate(group):
                        win = ubuf[c, pl.ds(upad - hist + r0 + p + j, CONV_ROWS, stride=2), :]
                        accs[i] = accs[i] + win * w_row
                for (r0, p), acc in zip(group, accs):
                    cbuf[c, pl.ds(r0 + p, CONV_ROWS, stride=2), :] = acc
                yield
        c_raw = jnp.concatenate([cbuf[c] for c in range(cbuf.shape[0])], axis=1)
        cring[slot_new] = _conv_branch_tail(c_raw, c_gate, w)
        yield
        half_tile = tile // 2
        for c in range(qbuf.shape[0]):
            for p in range(2):
                acc = jnp.zeros((half_tile, LANES), F32)
                for j in range(ks):
                    win = qbuf[c, pl.ds(qpad - qhist + p + j, half_tile, stride=2), :]
                    acc = acc + win * w["dn_w"][j:j + 1, _lanes(c)]
                qcv[c, pl.ds(p, half_tile, stride=2), :] = acc
            if c % 3 == 2:
                yield
        qkv_c = jnp.concatenate([qcv[c] for c in range(qcv.shape[0])], axis=1)
        qa, ka, va = yield from _qkv_heads(qkv_c)
        st_q[...] = jnp.concatenate(qa, axis=1)
        st_k[...] = jnp.concatenate(ka, axis=1)
        st_v[...] = jnp.concatenate(va, axis=1)
        for c in range(ubuf.shape[0]):
            ubuf[c, 0:upad, :] = ubuf[c, tile:tile + upad, :]
        for c in range(qbuf.shape[0]):
            qbuf[c, 0:qpad, :] = qbuf[c, tile:tile + qpad, :]

    o_heads, delta_local, c_gate = _interleave([stage3(), stage2(), stage1_project()], [8, 16, 15])
    for ref, value in delta_local:
        ref[...] = value
    _interleave([stage3_out(o_heads), stage1_conv(c_gate)], [3, 17])

    @pl.when((t_a == nt - 1) & (s < pl.num_programs(0) - 2))
    def _():
        n_seq = nconv_ref.shape[1]
        mine = jax.lax.broadcasted_iota(jnp.int32, (n_seq, LANES), 0) == s // nt
        for out_ref, buf, first, n in ((nconv_ref, ubuf, upad + tile - hist, hist),
                                       (ndn_ref, qbuf, qpad + tile - qhist, qhist)):
            for c in range(buf.shape[0]):
                for i in range(n):
                    new = jnp.broadcast_to(buf[c, first + i:first + i + 1, :], (n_seq, LANES))
                    out_ref[i, :, _lanes(c)] = jnp.where(mine, new, out_ref[i, :, _lanes(c)])

    @pl.when((t_c == nt - 1) & (s > 1))
    def _():
        ns_ref[0] = s_scr[...]


def _sample_kernel(*refs, seqs, steps):
    x_ref, p_ref, sc_ref, sdn_ref, s_ref = refs[:5]
    nw = len(_WEIGHT_NAMES)
    w = dict(zip(_WEIGHT_NAMES, refs[5:5 + nw]))
    y_ref, nconv_ref, ndn_ref, ns_ref = refs[5 + nw:9 + nw]
    ustage, cstage, qstage, qcs = refs[9 + nw:]
    rows = seqs * steps
    kw = w["dw_w"].shape[0]
    ks = w["dn_w"].shape[0]
    hist, qhist = sc_ref.shape[0], sdn_ref.shape[0]
    by_time = lambda ref, c, t: ref[c, pl.ds(t, seqs, stride=steps), :]

    x = x_ref[...]
    proj = _Projection(x, w)
    hs = range(N_HEADS)

    def conv_branch():
        u = proj.glu()
        n_uc = ustage.shape[0]
        for c in range(n_uc):
            ustage[c] = u[:, _lanes(c)]
        u_tm = [[by_time(ustage, c, t) for c in range(n_uc)] for t in range(steps)]
        yield
        for t in range(steps):
            for c in range(n_uc):
                acc = jnp.zeros((seqs, LANES), F32)
                for j in range(kw):
                    i = t + j
                    src = sc_ref[i, :, _lanes(c)] if i < hist else u_tm[i - hist][c]
                    acc = acc + src * w["dw_w"][j:j + 1, _lanes(c)]
                cstage[c, pl.ds(t, seqs, stride=steps), :] = acc
            yield
        for i in range(hist):
            nconv_ref[i] = sc_ref[i + steps] if i + steps < hist else jnp.concatenate(u_tm[i + steps - hist], axis=1)
        c_gate = proj.c_gate()
        yield
        return _conv_branch_tail(jnp.concatenate([cstage[c] for c in range(n_uc)], axis=1), c_gate, w)

    def delta_branch():
        n_qc = qstage.shape[0]
        for part in range(3):
            qkv_p = proj.qkv(part)
            for c in range(N_HEADS):
                qstage[part * N_HEADS + c] = qkv_p[:, _lanes(c)]
            yield
        q_tm = [[by_time(qstage, c, t) for c in range(n_qc)] for t in range(steps)]
        for t in range(steps):
            for c in range(n_qc):
                acc = jnp.zeros((seqs, LANES), F32)
                for j in range(ks):
                    i = t + j
                    src = sdn_ref[i, :, _lanes(c)] if i < qhist else q_tm[i - qhist][c]
                    acc = acc + src * w["dn_w"][j:j + 1, _lanes(c)]
                qcs[c, pl.ds(t, seqs, stride=steps), :] = acc
            if t % 2:
                yield
        for i in range(qhist):
            ndn_ref[i] = jnp.concatenate(q_tm[steps - qhist + i], axis=1)
        qs, kss, vs = yield from _qkv_heads(jnp.concatenate([qcs[c] for c in range(n_qc)], axis=1))
        bg = proj.beta_g()
        yield
        uu, wk, qk, qd, kd, gl, _ = yield from _chunk_local(qs, kss, vs, bg, steps)
        seq_of_col = lax.broadcasted_iota(jnp.int32, (seqs, 1, rows), 2) // steps
        seq_id = lax.broadcasted_iota(jnp.int32, (seqs, 1, rows), 0)
        col_mask = (seq_of_col == seq_id).astype(F32)
        s_old = [s_ref[:, h] for h in hs]
        m1 = []
        for h in hs:
            lhs = jnp.concatenate([wk[h].reshape(seqs, steps, HEAD_DIM), qd[h].reshape(seqs, steps, HEAD_DIM)], axis=1)
            m1.append(lax.dot_general(lhs.astype(BF16), s_old[h].astype(BF16), (((2,), (1,)), ((0,), (0,))),
                                      preferred_element_type=F32))
        yield
        v_new = [uu[h] - m1[h][:, :steps, :].reshape(rows, HEAD_DIM) for h in hs]
        o_heads = [m1[h][:, steps:, :].reshape(rows, HEAD_DIM) + _mm(qk[h], v_new[h]) for h in hs]
        yield
        for h in hs:
            kd_rows = (kd[h].T[None, :, :] * col_mask).reshape(seqs * HEAD_DIM, rows)
            ds = _mm(kd_rows, v_new[h]).reshape(seqs, HEAD_DIM, HEAD_DIM)
            gl_seq = jnp.exp(gl[h].reshape(seqs, steps, 1)[:, 0:1, :])
            ns_ref[:, h] = s_old[h] * gl_seq + ds
        return o_heads

    c_out, o_heads = _interleave([conv_branch(), delta_branch()], [10, 24])
    z = proj.z()
    y_ref[...] = _run(_finish(x, c_out, o_heads, z, p_ref[...], w))


def _full_spec(a):
    nd = a.ndim
    return pl.BlockSpec(a.shape, lambda *_: (0,) * nd, pipeline_mode=pl.Buffered(1))


def kernel(x_prompt, x_sample, state_conv, state_dn_conv, state_dn_S, p_prompt, p_sample, norm_mix_g, w_in, conv_dw_w, conv_dw_b, conv_ln_g, conv_ln_b, conv_pw_w, dn_conv_w, dn_a_log, dn_dt_bias, dn_norm_g, w_out, ple_norm_g, ple_gate_w, ple_proj_w, final_norm_g):
    depth = w_in.shape[0]
    assert depth == 1, "single trunk layer"
    bsz, seqlen, d_model = x_prompt.shape
    dec_b, dec_l, _ = x_sample.shape
    d_conv = conv_dw_w.shape[-1]
    d_dn = N_HEADS * HEAD_DIM
    kw, ks = conv_dw_w.shape[1], dn_conv_w.shape[1]
    d_main = 3 * d_conv + 4 * d_dn
    assert w_in.shape[-1] == d_main + 2 * N_HEADS
    assert dn_conv_w.shape[-1] == 3 * d_dn and dn_norm_g.shape[-1] == HEAD_DIM
    tile, chunk = min(PROMPT_TILE, seqlen), min(PROMPT_CHUNK, seqlen)
    assert seqlen % tile == 0 and tile % chunk == 0 and tile % (2 * CONV_ROWS) == 0
    assert dec_b % SAMPLE_SEQS == 0 and SAMPLE_SEQS % SUBLANES == 0 and dec_l == SUBLANES
    assert 2 * chunk == HEAD_DIM and tile % HEAD_DIM == 0

    row = lambda v: v.reshape(1, -1)
    head_lanes = lambda v: jnp.pad(v, (N_HEADS, LANES - 2 * N_HEADS)).reshape(1, LANES)
    weights = dict(
        g_mix=row(norm_mix_g[0]),
        w_in=w_in[0].astype(BF16),
        w_tail=jnp.pad(w_in[0, :, d_main:].astype(BF16), ((0, 0), (0, LANES - 2 * N_HEADS))),
        dw_w=conv_dw_w[0], dw_b=row(conv_dw_b[0]), ln_g=row(conv_ln_g[0]), ln_b=row(conv_ln_b[0]),
        pw=conv_pw_w[0],
        dn_w=dn_conv_w[0],
        a_log=head_lanes(dn_a_log[0]),
        dt_bias=head_lanes(dn_dt_bias[0]),
        dn_g=row(dn_norm_g[0]),
        w_out=w_out[0],
        ple_g=row(ple_norm_g[0]), ple_gate=ple_gate_w[0], ple_proj=ple_proj_w[0],
        fin_g=row(final_norm_g),
    )
    wlist = [weights[n] for n in _WEIGHT_NAMES]
    wspecs = [_full_spec(a) for a in wlist]
    params = dict(vmem_limit_bytes=VMEM_LIMIT_BYTES)

    nt = seqlen // tile
    upad = pl.cdiv(kw - 1, SUBLANES) * SUBLANES
    qpad = pl.cdiv(ks - 1, SUBLANES) * SUBLANES
    assert nt > 1
    n_tiles = bsz * nt
    front = lambda s: (jnp.minimum(s, n_tiles - 1) // nt, jnp.minimum(s, n_tiles - 1) % nt)
    back = lambda s: (jnp.maximum(s - 2, 0) // nt, jnp.maximum(s - 2, 0) % nt)
    stage = lambda cols: pltpu.VMEM((tile, cols), F32)
    slabs = lambda cols, rows: pltpu.VMEM((cols // LANES, rows, LANES), F32)
    y_p, nconv_p, ndn_p, ns_p = pl.pallas_call(
        functools.partial(_prompt_kernel, tile=tile, chunk=chunk, nt=nt),
        grid=(n_tiles + 2,),
        in_specs=[pl.BlockSpec((1, tile, d_model), lambda s: (*front(s), 0)),
                  pl.BlockSpec((1, tile, d_model), lambda s: (*back(s), 0)),
                  pl.BlockSpec((1, tile, p_prompt.shape[-1]), lambda s: (*back(s), 0))] + wspecs,
        out_specs=[pl.BlockSpec((1, tile, d_model), lambda s: (*back(s), 0)),
                   pl.BlockSpec((None, kw - 1, bsz, d_conv), lambda s: (0, 0, 0, 0)),
                   pl.BlockSpec((None, ks - 1, bsz, 3 * d_dn), lambda s: (0, 0, 0, 0)),
                   pl.BlockSpec((1, N_HEADS, HEAD_DIM, HEAD_DIM), lambda s: (back(s)[0], 0, 0, 0))],
        out_shape=[jax.ShapeDtypeStruct((bsz, seqlen, d_model), F32),
                   jax.ShapeDtypeStruct((1, kw - 1, bsz, d_conv), F32),
                   jax.ShapeDtypeStruct((1, ks - 1, bsz, 3 * d_dn), F32),
                   jax.ShapeDtypeStruct((bsz, N_HEADS, HEAD_DIM, HEAD_DIM), F32)],
        scratch_shapes=[slabs(d_conv, upad + tile),
                        slabs(3 * d_dn, qpad + tile),
                        slabs(d_conv, tile),
                        pltpu.VMEM((N_HEADS, HEAD_DIM, HEAD_DIM), F32),
                        slabs(3 * d_dn, tile),
                        pltpu.VMEM((3, tile, d_conv), F32),
                        pltpu.VMEM((3, tile, d_dn), F32),
                        stage(d_dn), stage(d_dn), stage(d_dn), stage(LANES),
                        stage(d_dn), stage(d_dn), stage(d_dn), stage(d_dn),
                        pltpu.VMEM((d_dn, tile), F32), stage(LANES)],
        compiler_params=pltpu.CompilerParams(dimension_semantics=("arbitrary",), **params),
        name="prompt_layer",
    )(x_prompt, x_prompt, p_prompt[0], *wlist)

    seqs = SAMPLE_SEQS
    rows = seqs * dec_l
    xs = x_sample.reshape(dec_b * dec_l, d_model)
    ps = p_sample[0].reshape(dec_b * dec_l, -1)
    tm = lambda a: jnp.transpose(a, (0, 2, 1, 3))
    conv_spec = pl.BlockSpec((None, kw - 1, seqs, d_conv), lambda i: (0, 0, i, 0))
    dn_spec = pl.BlockSpec((None, ks - 1, seqs, 3 * d_dn), lambda i: (0, 0, i, 0))
    state_spec = pl.BlockSpec((None, seqs, N_HEADS, HEAD_DIM, HEAD_DIM), lambda i: (0, i, 0, 0, 0))
    y_s, nconv_s, ndn_s, ns_s = pl.pallas_call(
        functools.partial(_sample_kernel, seqs=seqs, steps=dec_l),
        grid=(dec_b // seqs,),
        in_specs=[pl.BlockSpec((rows, d_model), lambda i: (i, 0)),
                  pl.BlockSpec((rows, ps.shape[-1]), lambda i: (i, 0)),
                  conv_spec, dn_spec, state_spec] + wspecs,
        out_specs=[pl.BlockSpec((rows, d_model), lambda i: (i, 0)), conv_spec, dn_spec, state_spec],
        out_shape=[jax.ShapeDtypeStruct((dec_b * dec_l, d_model), F32),
                   jax.ShapeDtypeStruct((1, kw - 1, dec_b, d_conv), F32),
                   jax.ShapeDtypeStruct((1, ks - 1, dec_b, 3 * d_dn), F32),
                   jax.ShapeDtypeStruct((1, dec_b, N_HEADS, HEAD_DIM, HEAD_DIM), F32)],
        scratch_shapes=[slabs(d_conv, rows), slabs(d_conv, rows), slabs(3 * d_dn, rows), slabs(3 * d_dn, rows)],
        compiler_params=pltpu.CompilerParams(dimension_semantics=("arbitrary",), **params),
        name="sample_layer",
    )(xs, ps, tm(state_conv), tm(state_dn_conv), state_dn_S, *wlist)

    return (y_p, y_s.reshape(dec_b, dec_l, d_model), tm(nconv_p), tm(ndn_p), ns_p[None],
            tm(nconv_s), tm(ndn_s), ns_s)
```

```python
import functools

import jax
import jax.numpy as jnp
from jax import lax
from jax.experimental import pallas as pl
from jax.experimental.pallas import tpu as pltpu

EPS = 1e-6
N_HEADS = 4
HEAD_DIM = 128
LANES = 128
SUBLANES = 8
PROMPT_TILE = 256
PROMPT_CHUNK = 64
SAMPLE_SEQS = 16
CONV_ROWS = 32
VMEM_LIMIT_BYTES = 56 * 1024 * 1024

F32 = jnp.float32
BF16 = jnp.bfloat16


def _run(gen):
    try:
        while True:
            next(gen)
    except StopIteration as stop:
        return stop.value


def _interleave(gens, weights):
    n = len(gens)
    done, alive, out = [0] * n, [True] * n, [None] * n
    while any(alive):
        k = min((i for i in range(n) if alive[i]), key=lambda i: (done[i] + 1) / weights[i])
        try:
            next(gens[k])
            done[k] += 1
        except StopIteration as stop:
            out[k], alive[k] = stop.value, False
    return out


def _mm(a, b):
    return jnp.dot(a.astype(BF16), b.astype(BF16), preferred_element_type=F32)


def _mm_w(a, weight):
    return jnp.dot(a, weight, preferred_element_type=F32)


def _mm_nt(a, b):
    return lax.dot_general(a.astype(BF16), b.astype(BF16), (((1,), (1,)), ((), ())),
                           preferred_element_type=F32)


def _mmb(a, b_bf16):
    if a.shape[0] % 16:
        return jnp.dot(a.astype(F32), b_bf16.astype(F32), preferred_element_type=F32)
    return jnp.dot(a.astype(BF16), b_bf16, preferred_element_type=F32)


def _rms(x, g):
    return x * lax.rsqrt(jnp.mean(x * x, axis=-1, keepdims=True) + EPS) * g


def _silu(x):
    return x * jax.nn.sigmoid(x)


def _softplus(x):
    return jnp.maximum(x, 0.0) + jnp.log(1.0 + jnp.exp(-jnp.abs(x)))


def _head(a, h):
    return a[:, h * HEAD_DIM:(h + 1) * HEAD_DIM]


def _lanes(c):
    return slice(c * LANES, (c + 1) * LANES)


class _Projection:
    def __init__(self, x, w):
        self.w, self.d_conv, self.d_dn = w, w["dw_w"].shape[-1], N_HEADS * HEAD_DIM
        self.h = _rms(x, w["g_mix"][...]).astype(BF16)

    def _cols(self, lo, n):
        return jnp.dot(self.h, self.w["w_in"][:, lo:lo + n], preferred_element_type=F32)

    def glu(self):
        ab = self._cols(0, 2 * self.d_conv)
        return ab[:, :self.d_conv] * jax.nn.sigmoid(ab[:, self.d_conv:])

    def c_gate(self):
        return self._cols(2 * self.d_conv, self.d_conv)

    def qkv(self, part):
        return self._cols(3 * self.d_conv + part * self.d_dn, self.d_dn)

    def z(self):
        return self._cols(3 * self.d_conv + 3 * self.d_dn, self.d_dn)

    def beta_g(self):
        w = self.w
        tail = jnp.dot(self.h, w["w_tail"][...], preferred_element_type=F32)
        lane = lax.broadcasted_iota(jnp.int32, tail.shape, 1)
        beta = jax.nn.sigmoid(tail)
        g = -jnp.exp(w["a_log"][...]) * _softplus(tail + w["dt_bias"][...])
        return jnp.where(lane < N_HEADS, beta, jnp.where(lane < 2 * N_HEADS, g, 0.0))


def _conv_branch_tail(c, c_gate, w):
    c = c + w["dw_b"][...]
    cc = c - jnp.mean(c, axis=-1, keepdims=True)
    c = cc * lax.rsqrt(jnp.mean(cc * cc, axis=-1, keepdims=True) + EPS) * w["ln_g"][...] + w["ln_b"][...]
    c = _silu(c)
    return _mm_w(c, w["pw"][...]) * _silu(c_gate)


def _qkv_heads(qkv_c):
    d_dn = N_HEADS * HEAD_DIM
    qs, ks, vs = [], [], []
    for h in range(N_HEADS):
        q = _silu(_head(qkv_c, h))
        k = _silu(_head(qkv_c[:, d_dn:2 * d_dn], h))
        v = _silu(_head(qkv_c[:, 2 * d_dn:], h))
        q = q * (lax.rsqrt(jnp.sum(q * q, axis=-1, keepdims=True) + EPS) * (HEAD_DIM ** -0.5))
        k = k * lax.rsqrt(jnp.sum(k * k, axis=-1, keepdims=True) + EPS)
        qs.append(q); ks.append(k); vs.append(v)
        yield
    return qs, ks, vs


def _chunk_masks(rows, chunk):
    ri = lax.broadcasted_iota(jnp.int32, (rows, rows), 0)
    ci = lax.broadcasted_iota(jnp.int32, (rows, rows), 1)
    same = (ri // chunk) == (ci // chunk)
    incl = same & (ci <= ri)
    strict = same & (ci < ri)
    return incl, strict


def _tri_inverse(a_list, chunk):
    rows = a_list[0].shape[0]
    nc = rows // chunk
    hs = range(len(a_list))
    pi = lax.broadcasted_iota(jnp.int32, (chunk, rows), 0)
    pl_ = lax.broadcasted_iota(jnp.int32, (chunk, rows), 1)
    lane_in, lane_blk = pl_ % chunk, pl_ // chunk
    eye_pan = (lane_in == pi).astype(F32)
    packed = chunk % 16 == 0
    blk_masks = [(lane_blk == c).astype(BF16 if packed else F32) for c in range(nc)]

    def fold(full):
        out = full[0:chunk]
        for c in range(1, nc):
            out = out + full[c * chunk:(c + 1) * chunk]
        return out

    def expand(pan):
        src = pan.astype(BF16) if packed else pan
        return jnp.concatenate([src * blk_masks[c] for c in range(nc)], axis=0).astype(BF16)

    def off_halves(size):
        return ((pi // size) == (lane_in // size)) & ((pi // (size // 2)) != (lane_in // (size // 2)))

    a_pan = [fold(a_list[h]) for h in hs]
    inv_pan = [eye_pan - jnp.where(off_halves(2), a_pan[h], 0.0) for h in hs]
    size = 4
    while size <= chunk:
        inv_full = [expand(inv_pan[h]) for h in hs]
        am_full = [expand(jnp.where(off_halves(size), a_pan[h], 0.0)) for h in hs]
        x_pan = [_mmb(inv_pan[h], am_full[h]) for h in hs]
        yield
        inv_pan = [inv_pan[h] - _mmb(x_pan[h], inv_full[h]) for h in hs]
        yield
        size *= 2
    return [expand(inv_pan[h]) for h in hs]


def _gate_scalars(bg, chunk):
    rows, lanes = bg.shape
    pos = lax.broadcasted_iota(jnp.int32, bg.shape, 0) % chunk
    gc, s = bg, 1
    while s < chunk:
        gc = gc + jnp.where(pos >= s, pltpu.roll(gc, s, 0), 0.0)
        s *= 2
    gtot = jnp.concatenate([jnp.broadcast_to(gc[e - 1:e, :], (chunk, lanes)) for e in range(chunk, rows + 1, chunk)],
                           axis=0)
    return gc, gc.T, gtot


def _chunk_local(qs, ks, vs, bg, chunk):
    incl, strict = _chunk_masks(bg.shape[0], chunk)
    gc, gct, gtot = _gate_scalars(bg, chunk)
    hs = range(N_HEADS)
    beta = [bg[:, h:h + 1] for h in hs]
    gcc = [gc[:, N_HEADS + h:N_HEADS + h + 1] for h in hs]
    gl = [gtot[:, N_HEADS + h:N_HEADS + h + 1] for h in hs]
    yield
    decay = [jnp.where(incl, jnp.exp(jnp.where(incl, gcc[h] - gct[N_HEADS + h:N_HEADS + h + 1, :], 0.0)), 0.0)
             for h in hs]
    kb = [ks[h] * beta[h] for h in hs]
    yield
    a = [jnp.where(strict, _mm_nt(kb[h], ks[h]) * decay[h], 0.0) for h in hs]
    yield
    qk = [_mm_nt(qs[h], ks[h]) * decay[h] for h in hs]
    egc = [jnp.exp(gcc[h]) for h in hs]
    rhs = [jnp.concatenate([vs[h] * beta[h], kb[h] * egc[h]], axis=1).astype(BF16) for h in hs]
    qd = [qs[h] * egc[h] for h in hs]
    kd = [ks[h] * jnp.exp(gl[h] - gcc[h]) for h in hs]
    yield
    tinv = yield from _tri_inverse(a, chunk)
    y = [_mmb(tinv[h], rhs[h]) for h in hs]
    yield
    u = [y[h][:, :HEAD_DIM] for h in hs]
    wk = [y[h][:, HEAD_DIM:] for h in hs]
    return u, wk, qk, qd, kd, gl, gtot


def _finish(x, c_out, o_heads, z, p_emb, w):
    outs = []
    for h in range(N_HEADS):
        o = o_heads[h]
        o = o * lax.rsqrt(jnp.mean(o * o, axis=-1, keepdims=True) + EPS) * w["dn_g"][...]
        outs.append(o * _silu(_head(z, h)))
    mix_in = jnp.concatenate([c_out] + outs, axis=1)
    x = x + _mm_w(mix_in, w["w_out"][...])
    yield
    gate = jax.nn.sigmoid(_mm_w(_rms(x, w["ple_g"][...]), w["ple_gate"][...]))
    yield
    x = x + gate * _mm_w(p_emb, w["ple_proj"][...])
    return _rms(x, w["fin_g"][...])


_WEIGHT_NAMES = ("g_mix", "w_in", "w_tail", "dw_w", "dw_b", "ln_g", "ln_b", "pw", "dn_w", "a_log",
                 "dt_bias", "dn_g", "w_out", "ple_g", "ple_gate", "ple_proj", "fin_g")


def _prompt_kernel(*refs, tile, chunk, nt):
    x_ref, xb_ref, pb_ref = refs[0], refs[1], refs[2]
    nw = len(_WEIGHT_NAMES)
    w = dict(zip(_WEIGHT_NAMES, refs[3:3 + nw]))
    y_ref, nconv_ref, ndn_ref, ns_ref = refs[3 + nw:7 + nw]
    (ubuf, qbuf, cbuf, s_scr, qcv, cring, zring, st_q, st_k, st_v, st_bg,
     s2_u, s2_w, s2_qd, s2_qk, s2_kt, s2_gl) = refs[7 + nw:]
    s = pl.program_id(0)
    slot_new, slot_old = s % 3, (s + 1) % 3
    t_a = s % nt
    t_c = (s + nt - 2) % nt
    kw = w["dw_w"].shape[0]
    ks = w["dn_w"].shape[0]
    hist, qhist = kw - 1, ks - 1
    upad, qpad = ubuf.shape[1] - tile, qbuf.shape[1] - tile
    hs = range(N_HEADS)
    half = HEAD_DIM // 2

    @pl.when(s == 0)
    def _():
        for ref in (s_scr, nconv_ref, ndn_ref):
            ref[...] = jnp.zeros(ref.shape, F32)

    @pl.when(t_a == 0)
    def _():
        ubuf[:, 0:upad, :] = jnp.zeros((ubuf.shape[0], upad, LANES), F32)
        qbuf[:, 0:qpad, :] = jnp.zeros((qbuf.shape[0], qpad, LANES), F32)

    def stage3():
        lane_c = lax.broadcasted_iota(jnp.int32, (chunk, HEAD_DIM), 1)
        lane_t = lax.broadcasted_iota(jnp.int32, (HEAD_DIM, HEAD_DIM), 1)
        hc = lambda h: slice(h * HEAD_DIM, (h + 1) * HEAD_DIM)
        zs = jnp.zeros((HEAD_DIM, HEAD_DIM), BF16)
        zv = jnp.zeros((chunk, HEAD_DIM), BF16)
        state = [jnp.where(t_c == 0, 0.0, s_scr[h]) for h in hs]
        o_rows = [[] for _ in hs]
        for n in range(tile // chunk):
            r = slice(n * chunk, (n + 1) * chunk)
            col, odd = divmod(n * chunk, HEAD_DIM)
            cs = slice(col * HEAD_DIM, (col + 1) * HEAD_DIM)
            pairs = range(0, N_HEADS, 2)
            pair_cols = lambda h0: slice(h0 * HEAD_DIM, (h0 + 2) * HEAD_DIM)
            m1 = {}
            for h0 in pairs:
                h1 = h0 + 1
                lhs1 = jnp.concatenate([s2_w[r, pair_cols(h0)], s2_qd[r, pair_cols(h0)]], axis=0)
                s0, s1 = state[h0].astype(BF16), state[h1].astype(BF16)
                sbd = jnp.concatenate([jnp.concatenate([s0, zs], axis=1), jnp.concatenate([zs, s1], axis=1)], axis=0)
                m1[h0] = _mmb(lhs1, sbd)
            yield
            new_state = list(state)
            decay_row = jnp.exp(s2_gl[n * chunk:n * chunk + 1, :])
            for h0 in pairs:
                h1 = h0 + 1
                v0 = s2_u[r, hc(h0)] - m1[h0][:chunk, :HEAD_DIM]
                v1 = s2_u[r, hc(h1)] - m1[h0][:chunk, HEAD_DIM:]
                vbd = jnp.concatenate([jnp.concatenate([v0.astype(BF16), zv], axis=1),
                                       jnp.concatenate([zv, v1.astype(BF16)], axis=1)], axis=0)
                k0, k1 = s2_kt[hc(h0), cs], s2_kt[hc(h1), cs]
                if odd:
                    kpair = jnp.where(lane_t < half, pltpu.roll(k0, half, 1), k1)
                else:
                    kpair = jnp.where(lane_t < half, k0, pltpu.roll(k1, half, 1))
                qpair = jnp.where(lane_c < half, s2_qk[r, hc(h0)], s2_qk[r, hc(h1)])
                m2 = _mmb(jnp.concatenate([qpair, kpair], axis=0), vbd)
                o_rows[h0].append(m1[h0][chunk:, :HEAD_DIM] + m2[:chunk, :HEAD_DIM])
                o_rows[h1].append(m1[h0][chunk:, HEAD_DIM:] + m2[:chunk, HEAD_DIM:])
                new_state[h0] = state[h0] * decay_row[:, N_HEADS + h0:N_HEADS + h0 + 1] + m2[chunk:, :HEAD_DIM]
                new_state[h1] = state[h1] * decay_row[:, N_HEADS + h1:N_HEADS + h1 + 1] + m2[chunk:, HEAD_DIM:]
            state = new_state
            yield
        for h in hs:
            s_scr[h] = state[h]
        return [jnp.concatenate(o_rows[h], axis=0) for h in hs]

    def stage3_out(o_heads):
        y_ref[0] = yield from _finish(xb_ref[0], cring[slot_old], o_heads, zring[slot_old], pb_ref[0], w)

    def stage2():
        heads_of = lambda ref: [ref[:, h * HEAD_DIM:(h + 1) * HEAD_DIM] for h in hs]
        uu2, wk2, qk2, qd2, kd2, _, gtot2 = yield from _chunk_local(
            heads_of(st_q), heads_of(st_k), heads_of(st_v), st_bg[...], chunk)
        folds = []
        for h in hs:
            f = qk2[h][:, 0:HEAD_DIM]
            for c in range(1, tile // HEAD_DIM):
                f = f + qk2[h][:, c * HEAD_DIM:(c + 1) * HEAD_DIM]
            folds.append(f + pltpu.roll(f, half, 1))
        yield
        return [(s2_u, jnp.concatenate(uu2, axis=1)), (s2_w, jnp.concatenate(wk2, axis=1)),
                (s2_qd, jnp.concatenate(qd2, axis=1)), (s2_qk, jnp.concatenate(folds, axis=1)),
                (s2_kt, jnp.concatenate([kd2[h].T for h in hs], axis=0)), (s2_gl, gtot2)]

    def stage1_project():
        proj = _Projection(x_ref[0], w)
        piece = 2 * LANES
        d_conv, d_dn = ubuf.shape[0] * LANES, N_HEADS * HEAD_DIM
        ab = []
        for lo in range(0, 2 * d_conv, piece):
            ab.append(proj._cols(lo, piece))
            yield
        n = len(ab) // 2
        for i in range(n):
            u = ab[i] * jax.nn.sigmoid(ab[n + i])
            for c in range(piece // LANES):
                ubuf[i * (piece // LANES) + c, upad:upad + tile, :] = u[:, _lanes(c)]
        gate = []
        for lo in range(2 * d_conv, 3 * d_conv, piece):
            gate.append(proj._cols(lo, piece))
            yield
        for i, lo in enumerate(range(3 * d_conv, 3 * d_conv + 3 * d_dn, piece)):
            cols = proj._cols(lo, piece)
            for c in range(piece // LANES):
                qbuf[i * (piece // LANES) + c, qpad:qpad + tile, :] = cols[:, _lanes(c)]
            yield
        for lo in range(0, d_dn, piece):
            zring[slot_new, :, lo:lo + piece] = proj._cols(3 * d_conv + 3 * d_dn + lo, piece)
            yield
        st_bg[...] = proj.beta_g()
        return jnp.concatenate(gate, axis=1)

    def stage1_conv(c_gate):
        starts = [(r0, p) for r0 in range(0, tile, 2 * CONV_ROWS) for p in range(2)]
        groups = [starts[:len(starts) // 2], starts[len(starts) // 2:]]
        for c in range(ubuf.shape[0]):
            for group in groups:
                accs = [jnp.zeros((CONV_ROWS, LANES), F32) for _ in group]
                for j in range(kw):
                    w_row = jnp.broadcast_to(w["dw_w"][j:j + 1, _lanes(c)], (CONV_ROWS, LANES))
                    for i, (r0, p) in enumerate(group):
                        win = ubuf[c, pl.ds(upad - hist + r0 + p + j, CONV_ROWS, stride=2), :]
                        accs[i] = accs[i] + win * w_row
                for (r0, p), acc in zip(group, accs):
                    cbuf[c, pl.ds(r0 + p, CONV_ROWS, stride=2), :] = acc
                yield
        c_raw = jnp.concatenate([cbuf[c] for c in range(cbuf.shape[0])], axis=1)
        cring[slot_new] = _conv_branch_tail(c_raw, c_gate, w)
        yield
        half_tile = tile // 2
        for c in range(qbuf.shape[0]):
            for p in range(2):
                acc = jnp.zeros((half_tile, LANES), F32)
                for j in range(ks):
                    win = qbuf[c, pl.ds(qpad - qhist + p + j, half_tile, stride=2), :]
                    acc = acc + win * w["dn_w"][j:j + 1, _lanes(c)]
                qcv[c, pl.ds(p, half_tile, stride=2), :] = acc
            if c % 3 == 2:
                yield
        qkv_c = jnp.concatenate([qcv[c] for c in range(qcv.shape[0])], axis=1)
        qa, ka, va = yield from _qkv_heads(qkv_c)
        st_q[...] = jnp.concatenate(qa, axis=1)
        st_k[...] = jnp.concatenate(ka, axis=1)
        st_v[...] = jnp.concatenate(va, axis=1)
        for c in range(ubuf.shape[0]):
            ubuf[c, 0:upad, :] = ubuf[c, tile:tile + upad, :]
        for c in range(qbuf.shape[0]):
            qbuf[c, 0:qpad, :] = qbuf[c, tile:tile + qpad, :]

    def step(first, second, third):
        chains = [(stage3, 8)] * third + [(stage2, 16)] * second + [(stage1_project, 15)] * first
        out = _interleave([g() for g, _ in chains], [n for _, n in chains])
        c_gate = out.pop() if first else None
        delta_local = out.pop() if second else []
        for ref, value in delta_local:
            ref[...] = value
        tails = ([(stage3_out(out[0]), 3)] if third else []) + ([(stage1_conv(c_gate), 17)] if first else [])
        _interleave([g for g, _ in tails], [n for _, n in tails])

    tiles = pl.num_programs(0) - 2
    pl.when(s == 0)(lambda: step(True, False, False))
    pl.when(s == 1)(lambda: step(True, True, False))
    pl.when((s >= 2) & (s < tiles))(lambda: step(True, True, True))
    pl.when(s == tiles)(lambda: step(False, True, True))
    pl.when(s == tiles + 1)(lambda: step(False, False, True))

    @pl.when((t_a == nt - 1) & (s < pl.num_programs(0) - 2))
    def _():
        n_seq = nconv_ref.shape[1]
        mine = jax.lax.broadcasted_iota(jnp.int32, (n_seq, LANES), 0) == s // nt
        for out_ref, buf, first, n in ((nconv_ref, ubuf, upad + tile - hist, hist),
                                       (ndn_ref, qbuf, qpad + tile - qhist, qhist)):
            for c in range(buf.shape[0]):
                for i in range(n):
                    new = jnp.broadcast_to(buf[c, first + i:first + i + 1, :], (n_seq, LANES))
                    out_ref[i, :, _lanes(c)] = jnp.where(mine, new, out_ref[i, :, _lanes(c)])

    @pl.when((t_c == nt - 1) & (s > 1))
    def _():
        ns_ref[0] = s_scr[...]


def _sample_kernel(*refs, seqs, steps):
    x_ref, p_ref, sc_ref, sdn_ref, s_ref = refs[:5]
    nw = len(_WEIGHT_NAMES)
    w = dict(zip(_WEIGHT_NAMES, refs[5:5 + nw]))
    y_ref, nconv_ref, ndn_ref, ns_ref = refs[5 + nw:9 + nw]
    ustage, cstage, qstage, qcs = refs[9 + nw:]
    rows = seqs * steps
    kw = w["dw_w"].shape[0]
    ks = w["dn_w"].shape[0]
    hist, qhist = sc_ref.shape[0], sdn_ref.shape[0]
    by_time = lambda ref, c, t: ref[c, pl.ds(t, seqs, stride=steps), :]

    x = x_ref[...]
    proj = _Projection(x, w)
    hs = range(N_HEADS)

    def conv_branch():
        u = proj.glu()
        n_uc = ustage.shape[0]
        for c in range(n_uc):
            ustage[c] = u[:, _lanes(c)]
        u_tm = [[by_time(ustage, c, t) for c in range(n_uc)] for t in range(steps)]
        yield
        for t in range(steps):
            for c in range(n_uc):
                acc = jnp.zeros((seqs, LANES), F32)
                for j in range(kw):
                    i = t + j
                    src = sc_ref[i, :, _lanes(c)] if i < hist else u_tm[i - hist][c]
                    acc = acc + src * w["dw_w"][j:j + 1, _lanes(c)]
                cstage[c, pl.ds(t, seqs, stride=steps), :] = acc
            yield
        for i in range(hist):
            nconv_ref[i] = sc_ref[i + steps] if i + steps < hist else jnp.concatenate(u_tm[i + steps - hist], axis=1)
        c_gate = proj.c_gate()
        yield
        return _conv_branch_tail(jnp.concatenate([cstage[c] for c in range(n_uc)], axis=1), c_gate, w)

    def delta_branch():
        n_qc = qstage.shape[0]
        for part in range(3):
            qkv_p = proj.qkv(part)
            for c in range(N_HEADS):
                qstage[part * N_HEADS + c] = qkv_p[:, _lanes(c)]
            yield
        q_tm = [[by_time(qstage, c, t) for c in range(n_qc)] for t in range(steps)]
        for t in range(steps):
            for c in range(n_qc):
                acc = jnp.zeros((seqs, LANES), F32)
                for j in range(ks):
                    i = t + j
                    src = sdn_ref[i, :, _lanes(c)] if i < qhist else q_tm[i - qhist][c]
                    acc = acc + src * w["dn_w"][j:j + 1, _lanes(c)]
                qcs[c, pl.ds(t, seqs, stride=steps), :] = acc
            if t % 2:
                yield
        for i in range(qhist):
            ndn_ref[i] = jnp.concatenate(q_tm[steps - qhist + i], axis=1)
        qs, kss, vs = yield from _qkv_heads(jnp.concatenate([qcs[c] for c in range(n_qc)], axis=1))
        bg = proj.beta_g()
        yield
        uu, wk, qk, qd, kd, gl, _ = yield from _chunk_local(qs, kss, vs, bg, steps)
        seq_of_col = lax.broadcasted_iota(jnp.int32, (seqs, 1, rows), 2) // steps
        seq_id = lax.broadcasted_iota(jnp.int32, (seqs, 1, rows), 0)
        col_mask = (seq_of_col == seq_id).astype(F32)
        s_old = [s_ref[:, h] for h in hs]
        m1 = []
        for h in hs:
            lhs = jnp.concatenate([wk[h].reshape(seqs, steps, HEAD_DIM), qd[h].reshape(seqs, steps, HEAD_DIM)], axis=1)
            m1.append(lax.dot_general(lhs.astype(BF16), s_old[h].astype(BF16), (((2,), (1,)), ((0,), (0,))),
                                      preferred_element_type=F32))
        yield
        v_new = [uu[h] - m1[h][:, :steps, :].reshape(rows, HEAD_DIM) for h in hs]
        o_heads = [m1[h][:, steps:, :].reshape(rows, HEAD_DIM) + _mm(qk[h], v_new[h]) for h in hs]
        yield
        for h in hs:
            kd_rows = (kd[h].T[None, :, :] * col_mask).reshape(seqs * HEAD_DIM, rows)
            ds = _mm(kd_rows, v_new[h]).reshape(seqs, HEAD_DIM, HEAD_DIM)
            gl_seq = jnp.exp(gl[h].reshape(seqs, steps, 1)[:, 0:1, :])
            ns_ref[:, h] = s_old[h] * gl_seq + ds
        return o_heads

    c_out, o_heads = _interleave([conv_branch(), delta_branch()], [10, 24])
    z = proj.z()
    y_ref[...] = _run(_finish(x, c_out, o_heads, z, p_ref[...], w))


def _full_spec(a):
    nd = a.ndim
    return pl.BlockSpec(a.shape, lambda *_: (0,) * nd, pipeline_mode=pl.Buffered(1))


def kernel(x_prompt, x_sample, state_conv, state_dn_conv, state_dn_S, p_prompt, p_sample, norm_mix_g, w_in, conv_dw_w, conv_dw_b, conv_ln_g, conv_ln_b, conv_pw_w, dn_conv_w, dn_a_log, dn_dt_bias, dn_norm_g, w_out, ple_norm_g, ple_gate_w, ple_proj_w, final_norm_g):
    depth = w_in.shape[0]
    assert depth == 1, "single trunk layer"
    bsz, seqlen, d_model = x_prompt.shape
    dec_b, dec_l, _ = x_sample.shape
    d_conv = conv_dw_w.shape[-1]
    d_dn = N_HEADS * HEAD_DIM
    kw, ks = conv_dw_w.shape[1], dn_conv_w.shape[1]
    d_main = 3 * d_conv + 4 * d_dn
    assert w_in.shape[-1] == d_main + 2 * N_HEADS
    assert dn_conv_w.shape[-1] == 3 * d_dn and dn_norm_g.shape[-1] == HEAD_DIM
    tile, chunk = min(PROMPT_TILE, seqlen), min(PROMPT_CHUNK, seqlen)
    assert seqlen % tile == 0 and tile % chunk == 0 and tile % (2 * CONV_ROWS) == 0
    assert dec_b % SAMPLE_SEQS == 0 and SAMPLE_SEQS % SUBLANES == 0 and dec_l == SUBLANES
    assert 2 * chunk == HEAD_DIM and tile % HEAD_DIM == 0

    row = lambda v: v.reshape(1, -1)
    head_lanes = lambda v: jnp.pad(v, (N_HEADS, LANES - 2 * N_HEADS)).reshape(1, LANES)
    weights = dict(
        g_mix=row(norm_mix_g[0]),
        w_in=w_in[0].astype(BF16),
        w_tail=jnp.pad(w_in[0, :, d_main:].astype(BF16), ((0, 0), (0, LANES - 2 * N_HEADS))),
        dw_w=conv_dw_w[0], dw_b=row(conv_dw_b[0]), ln_g=row(conv_ln_g[0]), ln_b=row(conv_ln_b[0]),
        pw=conv_pw_w[0],
        dn_w=dn_conv_w[0],
        a_log=head_lanes(dn_a_log[0]),
        dt_bias=head_lanes(dn_dt_bias[0]),
        dn_g=row(dn_norm_g[0]),
        w_out=w_out[0],
        ple_g=row(ple_norm_g[0]), ple_gate=ple_gate_w[0], ple_proj=ple_proj_w[0],
        fin_g=row(final_norm_g),
    )
    wlist = [weights[n] for n in _WEIGHT_NAMES]
    wspecs = [_full_spec(a) for a in wlist]
    params = dict(vmem_limit_bytes=VMEM_LIMIT_BYTES)

    nt = seqlen // tile
    upad = pl.cdiv(kw - 1, SUBLANES) * SUBLANES
    qpad = pl.cdiv(ks - 1, SUBLANES) * SUBLANES
    assert nt > 1
    n_tiles = bsz * nt
    front = lambda s: (jnp.minimum(s, n_tiles - 1) // nt, jnp.minimum(s, n_tiles - 1) % nt)
    back = lambda s: (jnp.maximum(s - 2, 0) // nt, jnp.maximum(s - 2, 0) % nt)
    stage = lambda cols: pltpu.VMEM((tile, cols), F32)
    slabs = lambda cols, rows: pltpu.VMEM((cols // LANES, rows, LANES), F32)
    y_p, nconv_p, ndn_p, ns_p = pl.pallas_call(
        functools.partial(_prompt_kernel, tile=tile, chunk=chunk, nt=nt),
        grid=(n_tiles + 2,),
        in_specs=[pl.BlockSpec((1, tile, d_model), lambda s: (*front(s), 0)),
                  pl.BlockSpec((1, tile, d_model), lambda s: (*back(s), 0)),
                  pl.BlockSpec((1, tile, p_prompt.shape[-1]), lambda s: (*back(s), 0))] + wspecs,
        out_specs=[pl.BlockSpec((1, tile, d_model), lambda s: (*back(s), 0)),
                   pl.BlockSpec((None, kw - 1, bsz, d_conv), lambda s: (0, 0, 0, 0)),
                   pl.BlockSpec((None, ks - 1, bsz, 3 * d_dn), lambda s: (0, 0, 0, 0)),
                   pl.BlockSpec((1, N_HEADS, HEAD_DIM, HEAD_DIM), lambda s: (back(s)[0], 0, 0, 0))],
        out_shape=[jax.ShapeDtypeStruct((bsz, seqlen, d_model), F32),
                   jax.ShapeDtypeStruct((1, kw - 1, bsz, d_conv), F32),
                   jax.ShapeDtypeStruct((1, ks - 1, bsz, 3 * d_dn), F32),
                   jax.ShapeDtypeStruct((bsz, N_HEADS, HEAD_DIM, HEAD_DIM), F32)],
        scratch_shapes=[slabs(d_conv, upad + tile),
                        slabs(3 * d_dn, qpad + tile),
                        slabs(d_conv, tile),
                        pltpu.VMEM((N_HEADS, HEAD_DIM, HEAD_DIM), F32),
                        slabs(3 * d_dn, tile),
                        pltpu.VMEM((3, tile, d_conv), F32),
                        pltpu.VMEM((3, tile, d_dn), F32),
                        stage(d_dn), stage(d_dn), stage(d_dn), stage(LANES),
                        stage(d_dn), stage(d_dn), stage(d_dn), stage(d_dn),
                        pltpu.VMEM((d_dn, tile), F32), stage(LANES)],
        compiler_params=pltpu.CompilerParams(dimension_semantics=("arbitrary",), **params),
        name="prompt_layer",
    )(x_prompt, x_prompt, p_prompt[0], *wlist)

    seqs = SAMPLE_SEQS
    rows = seqs * dec_l
    xs = x_sample.reshape(dec_b * dec_l, d_model)
    ps = p_sample[0].reshape(dec_b * dec_l, -1)
    tm = lambda a: jnp.transpose(a, (0, 2, 1, 3))
    conv_spec = pl.BlockSpec((None, kw - 1, seqs, d_conv), lambda i: (0, 0, i, 0))
    dn_spec = pl.BlockSpec((None, ks - 1, seqs, 3 * d_dn), lambda i: (0, 0, i, 0))
    state_spec = pl.BlockSpec((None, seqs, N_HEADS, HEAD_DIM, HEAD_DIM), lambda i: (0, i, 0, 0, 0))
    y_s, nconv_s, ndn_s, ns_s = pl.pallas_call(
        functools.partial(_sample_kernel, seqs=seqs, steps=dec_l),
        grid=(dec_b // seqs,),
        in_specs=[pl.BlockSpec((rows, d_model), lambda i: (i, 0)),
                  pl.BlockSpec((rows, ps.shape[-1]), lambda i: (i, 0)),
                  conv_spec, dn_spec, state_spec] + wspecs,
        out_specs=[pl.BlockSpec((rows, d_model), lambda i: (i, 0)), conv_spec, dn_spec, state_spec],
        out_shape=[jax.ShapeDtypeStruct((dec_b * dec_l, d_model), F32),
                   jax.ShapeDtypeStruct((1, kw - 1, dec_b, d_conv), F32),
                   jax.ShapeDtypeStruct((1, ks - 1, dec_b, 3 * d_dn), F32),
                   jax.ShapeDtypeStruct((1, dec_b, N_HEADS, HEAD_DIM, HEAD_DIM), F32)],
        scratch_shapes=[slabs(d_conv, rows), slabs(d_conv, rows), slabs(3 * d_dn, rows), slabs(3 * d_dn, rows)],
        compiler_params=pltpu.CompilerParams(dimension_semantics=("arbitrary",), **params),
        name="sample_layer",
    )(xs, ps, tm(state_conv), tm(state_dn_conv), state_dn_S, *wlist)

    return (y_p, y_s.reshape(dec_b, dec_l, d_model), tm(nconv_p), tm(ndn_p), ns_p[None],
            tm(nconv_s), tm(ndn_s), ns_s)
```

```python
import functools

import jax
import jax.numpy as jnp
from jax import lax
from jax.experimental import pallas as pl
from jax.experimental.pallas import tpu as pltpu

EPS = 1e-6
N_HEADS = 4
HEAD_DIM = 128
LANES = 128
SUBLANES = 8
PROMPT_TILE = 256
PROMPT_CHUNK = 64
SAMPLE_SEQS = 16
CONV_ROWS = 32
VMEM_LIMIT_BYTES = 56 * 1024 * 1024

F32 = jnp.float32
BF16 = jnp.bfloat16


def _interleave(gens, weights):
    n = len(gens)
    done, alive, out = [0] * n, [True] * n, [None] * n
    while any(alive):
        k = min((i for i in range(n) if alive[i]), key=lambda i: (done[i] + 1) / weights[i])
        try:
            next(gens[k])
            done[k] += 1
        except StopIteration as stop:
            out[k], alive[k] = stop.value, False
    return out


def _mm(a, b):
    return jnp.dot(a.astype(BF16), b.astype(BF16), preferred_element_type=F32)


def _mm_w(a, weight):
    return jnp.dot(a, weight, preferred_element_type=F32)


def _mm_nt(a, b):
    return lax.dot_general(a.astype(BF16), b.astype(BF16), (((1,), (1,)), ((), ())),
                           preferred_element_type=F32)


def _mmb(a, b_bf16):
    if a.shape[0] % 16:
        return jnp.dot(a.astype(F32), b_bf16.astype(F32), preferred_element_type=F32)
    return jnp.dot(a.astype(BF16), b_bf16, preferred_element_type=F32)


def _rms(x, g):
    return x * lax.rsqrt(jnp.mean(x * x, axis=-1, keepdims=True) + EPS) * g


def _silu(x):
    return x * jax.nn.sigmoid(x)


def _softplus(x):
    return jnp.maximum(x, 0.0) + jnp.log(1.0 + jnp.exp(-jnp.abs(x)))


def _head(a, h):
    return a[:, h * HEAD_DIM:(h + 1) * HEAD_DIM]


def _lanes(c):
    return slice(c * LANES, (c + 1) * LANES)


class _Projection:
    def __init__(self, x, w):
        self.w, self.d_conv, self.d_dn = w, w["dw_w"].shape[-1], N_HEADS * HEAD_DIM
        self.h = _rms(x, w["g_mix"][...]).astype(BF16)

    def _cols(self, lo, n):
        return jnp.dot(self.h, self.w["w_in"][:, lo:lo + n], preferred_element_type=F32)

    def glu(self):
        ab = self._cols(0, 2 * self.d_conv)
        return ab[:, :self.d_conv] * jax.nn.sigmoid(ab[:, self.d_conv:])

    def c_gate(self):
        return self._cols(2 * self.d_conv, self.d_conv)

    def qkv(self, part):
        return self._cols(3 * self.d_conv + part * self.d_dn, self.d_dn)

    def z(self):
        return self._cols(3 * self.d_conv + 3 * self.d_dn, self.d_dn)

    def beta_g(self):
        w = self.w
        tail = jnp.dot(self.h, w["w_tail"][...], preferred_element_type=F32)
        lane = lax.broadcasted_iota(jnp.int32, tail.shape, 1)
        beta = jax.nn.sigmoid(tail)
        g = -jnp.exp(w["a_log"][...]) * _softplus(tail + w["dt_bias"][...])
        return jnp.where(lane < N_HEADS, beta, jnp.where(lane < 2 * N_HEADS, g, 0.0))


def _conv_branch_tail(c, c_gate, w):
    c = c + w["dw_b"][...]
    cc = c - jnp.mean(c, axis=-1, keepdims=True)
    c = cc * lax.rsqrt(jnp.mean(cc * cc, axis=-1, keepdims=True) + EPS) * w["ln_g"][...] + w["ln_b"][...]
    c = _silu(c)
    return _mm_w(c, w["pw"][...]) * _silu(c_gate)


def _qkv_heads(qkv_c):
    d_dn = N_HEADS * HEAD_DIM
    qs, ks, vs = [], [], []
    for h in range(N_HEADS):
        q = _silu(_head(qkv_c, h))
        k = _silu(_head(qkv_c[:, d_dn:2 * d_dn], h))
        v = _silu(_head(qkv_c[:, 2 * d_dn:], h))
        q = q * (lax.rsqrt(jnp.sum(q * q, axis=-1, keepdims=True) + EPS) * (HEAD_DIM ** -0.5))
        k = k * lax.rsqrt(jnp.sum(k * k, axis=-1, keepdims=True) + EPS)
        qs.append(q); ks.append(k); vs.append(v)
        yield
    return qs, ks, vs


def _chunk_masks(rows, chunk):
    ri = lax.broadcasted_iota(jnp.int32, (rows, rows), 0)
    ci = lax.broadcasted_iota(jnp.int32, (rows, rows), 1)
    same = (ri // chunk) == (ci // chunk)
    incl = same & (ci <= ri)
    strict = same & (ci < ri)
    return incl, strict


def _tri_inverse(a_list, chunk):
    rows = a_list[0].shape[0]
    nc = rows // chunk
    hs = range(len(a_list))
    pi = lax.broadcasted_iota(jnp.int32, (chunk, rows), 0)
    pl_ = lax.broadcasted_iota(jnp.int32, (chunk, rows), 1)
    lane_in, lane_blk = pl_ % chunk, pl_ // chunk
    eye_pan = (lane_in == pi).astype(F32)
    packed = chunk % 16 == 0
    blk_masks = [(lane_blk == c).astype(BF16 if packed else F32) for c in range(nc)]

    def fold(full):
        out = full[0:chunk]
        for c in range(1, nc):
            out = out + full[c * chunk:(c + 1) * chunk]
        return out

    def expand(pan):
        src = pan.astype(BF16) if packed else pan
        return jnp.concatenate([src * blk_masks[c] for c in range(nc)], axis=0).astype(BF16)

    def off_halves(size):
        return ((pi // size) == (lane_in // size)) & ((pi // (size // 2)) != (lane_in // (size // 2)))

    a_pan = [fold(a_list[h]) for h in hs]
    inv_pan = [eye_pan - jnp.where(off_halves(2), a_pan[h], 0.0) for h in hs]
    size = 4
    while size <= chunk:
        inv_full = [expand(inv_pan[h]) for h in hs]
        am_full = [expand(jnp.where(off_halves(size), a_pan[h], 0.0)) for h in hs]
        x_pan = [_mmb(inv_pan[h], am_full[h]) for h in hs]
        yield
        inv_pan = [inv_pan[h] - _mmb(x_pan[h], inv_full[h]) for h in hs]
        yield
        size *= 2
    return [expand(inv_pan[h]) for h in hs]


def _gate_scalars(bg, chunk):
    rows, lanes = bg.shape
    pos = lax.broadcasted_iota(jnp.int32, bg.shape, 0) % chunk
    gc, s = bg, 1
    while s < chunk:
        gc = gc + jnp.where(pos >= s, pltpu.roll(gc, s, 0), 0.0)
        s *= 2
    gtot = jnp.concatenate([jnp.broadcast_to(gc[e - 1:e, :], (chunk, lanes)) for e in range(chunk, rows + 1, chunk)],
                           axis=0)
    return gc, gc.T, gtot


def _chunk_local(qs, ks, vs, bg, chunk):
    incl, strict = _chunk_masks(bg.shape[0], chunk)
    gc, gct, gtot = _gate_scalars(bg, chunk)
    hs = range(N_HEADS)
    beta = [bg[:, h:h + 1] for h in hs]
    gcc = [gc[:, N_HEADS + h:N_HEADS + h + 1] for h in hs]
    gl = [gtot[:, N_HEADS + h:N_HEADS + h + 1] for h in hs]
    yield
    decay = [jnp.where(incl, jnp.exp(jnp.where(incl, gcc[h] - gct[N_HEADS + h:N_HEADS + h + 1, :], 0.0)), 0.0)
             for h in hs]
    kb = [ks[h] * beta[h] for h in hs]
    yield
    a = [jnp.where(strict, _mm_nt(kb[h], ks[h]) * decay[h], 0.0) for h in hs]
    yield
    qk = [_mm_nt(qs[h], ks[h]) * decay[h] for h in hs]
    egc = [jnp.exp(gcc[h]) for h in hs]
    rhs = [jnp.concatenate([vs[h] * beta[h], kb[h] * egc[h]], axis=1).astype(BF16) for h in hs]
    qd = [qs[h] * egc[h] for h in hs]
    kd = [ks[h] * jnp.exp(gl[h] - gcc[h]) for h in hs]
    yield
    tinv = yield from _tri_inverse(a, chunk)
    y = [_mmb(tinv[h], rhs[h]) for h in hs]
    yield
    u = [y[h][:, :HEAD_DIM] for h in hs]
    wk = [y[h][:, HEAD_DIM:] for h in hs]
    return u, wk, qk, qd, kd, gl, gtot


def _finish(x, c_out, o_heads, z, p_emb, w):
    outs = []
    for h in range(N_HEADS):
        o = o_heads[h]
        o = o * lax.rsqrt(jnp.mean(o * o, axis=-1, keepdims=True) + EPS) * w["dn_g"][...]
        outs.append(o * _silu(_head(z, h)))
    mix_in = jnp.concatenate([c_out] + outs, axis=1)
    x = x + _mm_w(mix_in, w["w_out"][...])
    yield
    gate = jax.nn.sigmoid(_mm_w(_rms(x, w["ple_g"][...]), w["ple_gate"][...]))
    yield
    x = x + gate * _mm_w(p_emb, w["ple_proj"][...])
    return _rms(x, w["fin_g"][...])


_WEIGHT_NAMES = ("g_mix", "w_in", "w_tail", "dw_w", "dw_b", "ln_g", "ln_b", "pw", "dn_w", "a_log",
                 "dt_bias", "dn_g", "w_out", "ple_g", "ple_gate", "ple_proj", "fin_g")


def _prompt_kernel(*refs, tile, chunk, nt):
    x_ref, xb_ref, pb_ref = refs[0], refs[1], refs[2]
    nw = len(_WEIGHT_NAMES)
    w = dict(zip(_WEIGHT_NAMES, refs[3:3 + nw]))
    y_ref, nconv_ref, ndn_ref, ns_ref = refs[3 + nw:7 + nw]
    (ubuf, qbuf, cbuf, s_scr, qcv, cring, zring, st_q, st_k, st_v, st_bg,
     s2_u, s2_w, s2_qd, s2_qk, s2_kt, s2_gl) = refs[7 + nw:]
    s = pl.program_id(0)
    slot_new, slot_old = s % 3, (s + 1) % 3
    t_a = s % nt
    t_c = (s + nt - 2) % nt
    kw = w["dw_w"].shape[0]
    ks = w["dn_w"].shape[0]
    hist, qhist = kw - 1, ks - 1
    upad, qpad = ubuf.shape[1] - tile, qbuf.shape[1] - tile
    hs = range(N_HEADS)
    half = HEAD_DIM // 2

    @pl.when(s == 0)
    def _():
        for ref in (s_scr, nconv_ref, ndn_ref):
            ref[...] = jnp.zeros(ref.shape, F32)

    @pl.when(t_a == 0)
    def _():
        ubuf[:, 0:upad, :] = jnp.zeros((ubuf.shape[0], upad, LANES), F32)
        qbuf[:, 0:qpad, :] = jnp.zeros((qbuf.shape[0], qpad, LANES), F32)

    def stage3():
        lane_c = lax.broadcasted_iota(jnp.int32, (chunk, HEAD_DIM), 1)
        lane_t = lax.broadcasted_iota(jnp.int32, (HEAD_DIM, HEAD_DIM), 1)
        hc = lambda h: slice(h * HEAD_DIM, (h + 1) * HEAD_DIM)
        zs = jnp.zeros((HEAD_DIM, HEAD_DIM), BF16)
        zv = jnp.zeros((chunk, HEAD_DIM), BF16)
        state = [jnp.where(t_c == 0, 0.0, s_scr[h]) for h in hs]
        o_rows = [[] for _ in hs]
        for n in range(tile // chunk):
            r = slice(n * chunk, (n + 1) * chunk)
            col, odd = divmod(n * chunk, HEAD_DIM)
            cs = slice(col * HEAD_DIM, (col + 1) * HEAD_DIM)
            pairs = range(0, N_HEADS, 2)
            pair_cols = lambda h0: slice(h0 * HEAD_DIM, (h0 + 2) * HEAD_DIM)
            m1 = {}
            for h0 in pairs:
                h1 = h0 + 1
                lhs1 = jnp.concatenate([s2_w[r, pair_cols(h0)], s2_qd[r, pair_cols(h0)]], axis=0)
                s0, s1 = state[h0].astype(BF16), state[h1].astype(BF16)
                sbd = jnp.concatenate([jnp.concatenate([s0, zs], axis=1), jnp.concatenate([zs, s1], axis=1)], axis=0)
                m1[h0] = _mmb(lhs1, sbd)
            yield
            new_state = list(state)
            decay_row = jnp.exp(s2_gl[n * chunk:n * chunk + 1, :])
            for h0 in pairs:
                h1 = h0 + 1
                v0 = s2_u[r, hc(h0)] - m1[h0][:chunk, :HEAD_DIM]
                v1 = s2_u[r, hc(h1)] - m1[h0][:chunk, HEAD_DIM:]
                vbd = jnp.concatenate([jnp.concatenate([v0.astype(BF16), zv], axis=1),
                                       jnp.concatenate([zv, v1.astype(BF16)], axis=1)], axis=0)
                k0, k1 = s2_kt[hc(h0), cs], s2_kt[hc(h1), cs]
                if odd:
                    kpair = jnp.where(lane_t < half, pltpu.roll(k0, half, 1), k1)
                else:
                    kpair = jnp.where(lane_t < half, k0, pltpu.roll(k1, half, 1))
                qpair = jnp.where(lane_c < half, s2_qk[r, hc(h0)], s2_qk[r, hc(h1)])
                m2 = _mmb(jnp.concatenate([qpair, kpair], axis=0), vbd)
                o_rows[h0].append(m1[h0][chunk:, :HEAD_DIM] + m2[:chunk, :HEAD_DIM])
                o_rows[h1].append(m1[h0][chunk:, HEAD_DIM:] + m2[:chunk, HEAD_DIM:])
                new_state[h0] = state[h0] * decay_row[:, N_HEADS + h0:N_HEADS + h0 + 1] + m2[chunk:, :HEAD_DIM]
                new_state[h1] = state[h1] * decay_row[:, N_HEADS + h1:N_HEADS + h1 + 1] + m2[chunk:, HEAD_DIM:]
            state = new_state
            yield
        for h in hs:
            s_scr[h] = state[h]
        return [jnp.concatenate(o_rows[h], axis=0) for h in hs]

    def stage3_out(o_heads):
        y_ref[0] = yield from _finish(xb_ref[0], cring[slot_old], o_heads, zring[slot_old], pb_ref[0], w)

    def stage2():
        heads_of = lambda ref: [ref[:, h * HEAD_DIM:(h + 1) * HEAD_DIM] for h in hs]
        uu2, wk2, qk2, qd2, kd2, _, gtot2 = yield from _chunk_local(
            heads_of(st_q), heads_of(st_k), heads_of(st_v), st_bg[...], chunk)
        folds = []
        for h in hs:
            f = qk2[h][:, 0:HEAD_DIM]
            for c in range(1, tile // HEAD_DIM):
                f = f + qk2[h][:, c * HEAD_DIM:(c + 1) * HEAD_DIM]
            folds.append(f + pltpu.roll(f, half, 1))
        yield
        return [(s2_u, jnp.concatenate(uu2, axis=1)), (s2_w, jnp.concatenate(wk2, axis=1)),
                (s2_qd, jnp.concatenate(qd2, axis=1)), (s2_qk, jnp.concatenate(folds, axis=1)),
                (s2_kt, jnp.concatenate([kd2[h].T for h in hs], axis=0)), (s2_gl, gtot2)]

    def stage1_project():
        proj = _Projection(x_ref[0], w)
        piece = 2 * LANES
        d_conv, d_dn = ubuf.shape[0] * LANES, N_HEADS * HEAD_DIM
        ab = []
        for lo in range(0, 2 * d_conv, piece):
            ab.append(proj._cols(lo, piece))
            yield
        n = len(ab) // 2
        for i in range(n):
            u = ab[i] * jax.nn.sigmoid(ab[n + i])
            for c in range(piece // LANES):
                ubuf[i * (piece // LANES) + c, upad:upad + tile, :] = u[:, _lanes(c)]
        gate = []
        for lo in range(2 * d_conv, 3 * d_conv, piece):
            gate.append(proj._cols(lo, piece))
            yield
        for i, lo in enumerate(range(3 * d_conv, 3 * d_conv + 3 * d_dn, piece)):
            cols = proj._cols(lo, piece)
            for c in range(piece // LANES):
                qbuf[i * (piece // LANES) + c, qpad:qpad + tile, :] = cols[:, _lanes(c)]
            yield
        for lo in range(0, d_dn, piece):
            zring[slot_new, :, lo:lo + piece] = proj._cols(3 * d_conv + 3 * d_dn + lo, piece)
            yield
        st_bg[...] = proj.beta_g()
        return jnp.concatenate(gate, axis=1)

    def stage1_conv(c_gate):
        starts = [(r0, p) for r0 in range(0, tile, 2 * CONV_ROWS) for p in range(2)]
        groups = [starts[:len(starts) // 2], starts[len(starts) // 2:]]
        for c in range(ubuf.shape[0]):
            for group in groups:
                accs = [jnp.zeros((CONV_ROWS, LANES), F32) for _ in group]
                for j in range(kw):
                    w_row = jnp.broadcast_to(w["dw_w"][j:j + 1, _lanes(c)], (CONV_ROWS, LANES))
                    for i, (r0, p) in enumerate(group):
                        win = ubuf[c, pl.ds(upad - hist + r0 + p + j, CONV_ROWS, stride=2), :]
                        accs[i] = accs[i] + win * w_row
                for (r0, p), acc in zip(group, accs):
                    cbuf[c, pl.ds(r0 + p, CONV_ROWS, stride=2), :] = acc
                yield
        c_raw = jnp.concatenate([cbuf[c] for c in range(cbuf.shape[0])], axis=1)
        cring[slot_new] = _conv_branch_tail(c_raw, c_gate, w)
        yield
        half_tile = tile // 2
        for c in range(qbuf.shape[0]):
            for p in range(2):
                acc = jnp.zeros((half_tile, LANES), F32)
                for j in range(ks):
                    win = qbuf[c, pl.ds(qpad - qhist + p + j, half_tile, stride=2), :]
                    acc = acc + win * w["dn_w"][j:j + 1, _lanes(c)]
                qcv[c, pl.ds(p, half_tile, stride=2), :] = acc
            if c % 3 == 2:
                yield
        qkv_c = jnp.concatenate([qcv[c] for c in range(qcv.shape[0])], axis=1)
        qa, ka, va = yield from _qkv_heads(qkv_c)
        st_q[...] = jnp.concatenate(qa, axis=1)
        st_k[...] = jnp.concatenate(ka, axis=1)
        st_v[...] = jnp.concatenate(va, axis=1)
        for c in range(ubuf.shape[0]):
            ubuf[c, 0:upad, :] = ubuf[c, tile:tile + upad, :]
        for c in range(qbuf.shape[0]):
            qbuf[c, 0:qpad, :] = qbuf[c, tile:tile + qpad, :]

    def step(first, second, third):
        chains = [(stage3, 8)] * third + [(stage2, 16)] * second + [(stage1_project, 15)] * first
        out = _interleave([g() for g, _ in chains], [n for _, n in chains])
        c_gate = out.pop() if first else None
        delta_local = out.pop() if second else []
        for ref, value in delta_local:
            ref[...] = value
        tails = ([(stage3_out(out[0]), 3)] if third else []) + ([(stage1_conv(c_gate), 17)] if first else [])
        _interleave([g for g, _ in tails], [n for _, n in tails])

    tiles = pl.num_programs(0) - 2
    pl.when(s == 0)(lambda: step(True, False, False))
    pl.when(s == 1)(lambda: step(True, True, False))
    pl.when((s >= 2) & (s < tiles))(lambda: step(True, True, True))
    pl.when(s == tiles)(lambda: step(False, True, True))
    pl.when(s == tiles + 1)(lambda: step(False, False, True))

    @pl.when((t_a == nt - 1) & (s < pl.num_programs(0) - 2))
    def _():
        n_seq = nconv_ref.shape[1]
        mine = jax.lax.broadcasted_iota(jnp.int32, (n_seq, LANES), 0) == s // nt
        for out_ref, buf, first, n in ((nconv_ref, ubuf, upad + tile - hist, hist),
                                       (ndn_ref, qbuf, qpad + tile - qhist, qhist)):
            for c in range(buf.shape[0]):
                for i in range(n):
                    new = jnp.broadcast_to(buf[c, first + i:first + i + 1, :], (n_seq, LANES))
                    out_ref[i, :, _lanes(c)] = jnp.where(mine, new, out_ref[i, :, _lanes(c)])

    @pl.when((t_c == nt - 1) & (s > 1))
    def _():
        ns_ref[0] = s_scr[...]


def _sample_kernel(*refs, seqs, steps):
    x_ref, xb_ref, pb_ref, sc_ref, sdn_ref, s_ref = refs[:6]
    nw = len(_WEIGHT_NAMES)
    w = dict(zip(_WEIGHT_NAMES, refs[6:6 + nw]))
    y_ref, nconv_ref, ndn_ref, ns_ref = refs[6 + nw:10 + nw]
    ustage, cstage, qstage, qcs, c_keep, o_keep, z_keep = refs[10 + nw:]
    rows = seqs * steps
    kw = w["dw_w"].shape[0]
    ks = w["dn_w"].shape[0]
    hist, qhist = sc_ref.shape[0], sdn_ref.shape[0]
    by_time = lambda ref, c, t: ref[c, pl.ds(t, seqs, stride=steps), :]
    hs = range(N_HEADS)
    block = pl.program_id(0)
    blocks = pl.num_programs(0) - 1

    def output_path():
        o_heads = [_head(o_keep[...], h) for h in hs]
        y_ref[...] = yield from _finish(xb_ref[...], c_keep[...], o_heads, z_keep[...], pb_ref[...], w)

    def step(front, back):
        gens = ([(conv_branch(), 10), (delta_branch(), 24)] if front else []) + ([(output_path(), 3)] if back else [])
        out = _interleave([g for g, _ in gens], [n for _, n in gens])
        if front:
            c_keep[...] = out[0]
            o_keep[...] = jnp.concatenate(out[1], axis=1)
            z_keep[...] = proj.z()

    proj = _Projection(x_ref[...], w)

    def conv_branch():
        u = proj.glu()
        n_uc = ustage.shape[0]
        for c in range(n_uc):
            ustage[c] = u[:, _lanes(c)]
        u_tm = [[by_time(ustage, c, t) for c in range(n_uc)] for t in range(steps)]
        yield
        for t in range(steps):
            for c in range(n_uc):
                acc = jnp.zeros((seqs, LANES), F32)
                for j in range(kw):
                    i = t + j
                    src = sc_ref[i, :, _lanes(c)] if i < hist else u_tm[i - hist][c]
                    acc = acc + src * w["dw_w"][j:j + 1, _lanes(c)]
                cstage[c, pl.ds(t, seqs, stride=steps), :] = acc
            yield
        for i in range(hist):
            nconv_ref[i] = sc_ref[i + steps] if i + steps < hist else jnp.concatenate(u_tm[i + steps - hist], axis=1)
        c_gate = proj.c_gate()
        yield
        return _conv_branch_tail(jnp.concatenate([cstage[c] for c in range(n_uc)], axis=1), c_gate, w)

    def delta_branch():
        n_qc = qstage.shape[0]
        for part in range(3):
            qkv_p = proj.qkv(part)
            for c in range(N_HEADS):
                qstage[part * N_HEADS + c] = qkv_p[:, _lanes(c)]
            yield
        q_tm = [[by_time(qstage, c, t) for c in range(n_qc)] for t in range(steps)]
        for t in range(steps):
            for c in range(n_qc):
                acc = jnp.zeros((seqs, LANES), F32)
                for j in range(ks):
                    i = t + j
                    src = sdn_ref[i, :, _lanes(c)] if i < qhist else q_tm[i - qhist][c]
                    acc = acc + src * w["dn_w"][j:j + 1, _lanes(c)]
                qcs[c, pl.ds(t, seqs, stride=steps), :] = acc
            if t % 2:
                yield
        for i in range(qhist):
            ndn_ref[i] = jnp.concatenate(q_tm[steps - qhist + i], axis=1)
        qs, kss, vs = yield from _qkv_heads(jnp.concatenate([qcs[c] for c in range(n_qc)], axis=1))
        bg = proj.beta_g()
        yield
        uu, wk, qk, qd, kd, gl, _ = yield from _chunk_local(qs, kss, vs, bg, steps)
        seq_of_col = lax.broadcasted_iota(jnp.int32, (seqs, 1, rows), 2) // steps
        seq_id = lax.broadcasted_iota(jnp.int32, (seqs, 1, rows), 0)
        col_mask = (seq_of_col == seq_id).astype(F32)
        s_old = [s_ref[:, h] for h in hs]
        m1 = []
        for h in hs:
            lhs = jnp.concatenate([wk[h].reshape(seqs, steps, HEAD_DIM), qd[h].reshape(seqs, steps, HEAD_DIM)], axis=1)
            m1.append(lax.dot_general(lhs.astype(BF16), s_old[h].astype(BF16), (((2,), (1,)), ((0,), (0,))),
                                      preferred_element_type=F32))
        yield
        v_new = [uu[h] - m1[h][:, :steps, :].reshape(rows, HEAD_DIM) for h in hs]
        o_heads = [m1[h][:, steps:, :].reshape(rows, HEAD_DIM) + _mm(qk[h], v_new[h]) for h in hs]
        yield
        for h in hs:
            kd_rows = (kd[h].T[None, :, :] * col_mask).reshape(seqs * HEAD_DIM, rows)
            ds = _mm(kd_rows, v_new[h]).reshape(seqs, HEAD_DIM, HEAD_DIM)
            gl_seq = jnp.exp(gl[h].reshape(seqs, steps, 1)[:, 0:1, :])
            ns_ref[:, h] = s_old[h] * gl_seq + ds
        return o_heads

    pl.when(block == 0)(lambda: step(True, False))
    pl.when((block > 0) & (block < blocks))(lambda: step(True, True))
    pl.when(block == blocks)(lambda: step(False, True))


def _full_spec(a):
    nd = a.ndim
    return pl.BlockSpec(a.shape, lambda *_: (0,) * nd, pipeline_mode=pl.Buffered(1))


def kernel(x_prompt, x_sample, state_conv, state_dn_conv, state_dn_S, p_prompt, p_sample, norm_mix_g, w_in, conv_dw_w, conv_dw_b, conv_ln_g, conv_ln_b, conv_pw_w, dn_conv_w, dn_a_log, dn_dt_bias, dn_norm_g, w_out, ple_norm_g, ple_gate_w, ple_proj_w, final_norm_g):
    depth = w_in.shape[0]
    assert depth == 1, "single trunk layer"
    bsz, seqlen, d_model = x_prompt.shape
    dec_b, dec_l, _ = x_sample.shape
    d_conv = conv_dw_w.shape[-1]
    d_dn = N_HEADS * HEAD_DIM
    kw, ks = conv_dw_w.shape[1], dn_conv_w.shape[1]
    d_main = 3 * d_conv + 4 * d_dn
    assert w_in.shape[-1] == d_main + 2 * N_HEADS
    assert dn_conv_w.shape[-1] == 3 * d_dn and dn_norm_g.shape[-1] == HEAD_DIM
    tile, chunk = min(PROMPT_TILE, seqlen), min(PROMPT_CHUNK, seqlen)
    assert seqlen % tile == 0 and tile % chunk == 0 and tile % (2 * CONV_ROWS) == 0
    assert dec_b % SAMPLE_SEQS == 0 and SAMPLE_SEQS % SUBLANES == 0 and dec_l == SUBLANES
    assert 2 * chunk == HEAD_DIM and tile % HEAD_DIM == 0

    row = lambda v: v.reshape(1, -1)
    head_lanes = lambda v: jnp.pad(v, (N_HEADS, LANES - 2 * N_HEADS)).reshape(1, LANES)
    weights = dict(
        g_mix=row(norm_mix_g[0]),
        w_in=w_in[0].astype(BF16),
        w_tail=jnp.pad(w_in[0, :, d_main:].astype(BF16), ((0, 0), (0, LANES - 2 * N_HEADS))),
        dw_w=conv_dw_w[0], dw_b=row(conv_dw_b[0]), ln_g=row(conv_ln_g[0]), ln_b=row(conv_ln_b[0]),
        pw=conv_pw_w[0],
        dn_w=dn_conv_w[0],
        a_log=head_lanes(dn_a_log[0]),
        dt_bias=head_lanes(dn_dt_bias[0]),
        dn_g=row(dn_norm_g[0]),
        w_out=w_out[0],
        ple_g=row(ple_norm_g[0]), ple_gate=ple_gate_w[0], ple_proj=ple_proj_w[0],
        fin_g=row(final_norm_g),
    )
    wlist = [weights[n] for n in _WEIGHT_NAMES]
    wspecs = [_full_spec(a) for a in wlist]
    params = dict(vmem_limit_bytes=VMEM_LIMIT_BYTES)

    nt = seqlen // tile
    upad = pl.cdiv(kw - 1, SUBLANES) * SUBLANES
    qpad = pl.cdiv(ks - 1, SUBLANES) * SUBLANES
    assert nt > 1
    n_tiles = bsz * nt
    front = lambda s: (jnp.minimum(s, n_tiles - 1) // nt, jnp.minimum(s, n_tiles - 1) % nt)
    back = lambda s: (jnp.maximum(s - 2, 0) // nt, jnp.maximum(s - 2, 0) % nt)
    stage = lambda cols: pltpu.VMEM((tile, cols), F32)
    slabs = lambda cols, rows: pltpu.VMEM((cols // LANES, rows, LANES), F32)
    y_p, nconv_p, ndn_p, ns_p = pl.pallas_call(
        functools.partial(_prompt_kernel, tile=tile, chunk=chunk, nt=nt),
        grid=(n_tiles + 2,),
        in_specs=[pl.BlockSpec((1, tile, d_model), lambda s: (*front(s), 0)),
                  pl.BlockSpec((1, tile, d_model), lambda s: (*back(s), 0)),
                  pl.BlockSpec((1, tile, p_prompt.shape[-1]), lambda s: (*back(s), 0))] + wspecs,
        out_specs=[pl.BlockSpec((1, tile, d_model), lambda s: (*back(s), 0)),
                   pl.BlockSpec((None, kw - 1, bsz, d_conv), lambda s: (0, 0, 0, 0)),
                   pl.BlockSpec((None, ks - 1, bsz, 3 * d_dn), lambda s: (0, 0, 0, 0)),
                   pl.BlockSpec((1, N_HEADS, HEAD_DIM, HEAD_DIM), lambda s: (back(s)[0], 0, 0, 0))],
        out_shape=[jax.ShapeDtypeStruct((bsz, seqlen, d_model), F32),
                   jax.ShapeDtypeStruct((1, kw - 1, bsz, d_conv), F32),
                   jax.ShapeDtypeStruct((1, ks - 1, bsz, 3 * d_dn), F32),
                   jax.ShapeDtypeStruct((bsz, N_HEADS, HEAD_DIM, HEAD_DIM), F32)],
        scratch_shapes=[slabs(d_conv, upad + tile),
                        slabs(3 * d_dn, qpad + tile),
                        slabs(d_conv, tile),
                        pltpu.VMEM((N_HEADS, HEAD_DIM, HEAD_DIM), F32),
                        slabs(3 * d_dn, tile),
                        pltpu.VMEM((3, tile, d_conv), F32),
                        pltpu.VMEM((3, tile, d_dn), F32),
                        stage(d_dn), stage(d_dn), stage(d_dn), stage(LANES),
                        stage(d_dn), stage(d_dn), stage(d_dn), stage(d_dn),
                        pltpu.VMEM((d_dn, tile), F32), stage(LANES)],
        compiler_params=pltpu.CompilerParams(dimension_semantics=("arbitrary",), **params),
        name="prompt_layer",
    )(x_prompt, x_prompt, p_prompt[0], *wlist)

    seqs = SAMPLE_SEQS
    rows = seqs * dec_l
    xs = x_sample.reshape(dec_b * dec_l, d_model)
    ps = p_sample[0].reshape(dec_b * dec_l, -1)
    tm = lambda a: jnp.transpose(a, (0, 2, 1, 3))
    n_blocks = dec_b // seqs
    new = lambda i: jnp.minimum(i, n_blocks - 1)
    old = lambda i: jnp.maximum(i - 1, 0)
    conv_spec = pl.BlockSpec((None, kw - 1, seqs, d_conv), lambda i: (0, 0, new(i), 0))
    dn_spec = pl.BlockSpec((None, ks - 1, seqs, 3 * d_dn), lambda i: (0, 0, new(i), 0))
    state_spec = pl.BlockSpec((None, seqs, N_HEADS, HEAD_DIM, HEAD_DIM), lambda i: (0, new(i), 0, 0, 0))
    keep = lambda cols: pltpu.VMEM((rows, cols), F32)
    y_s, nconv_s, ndn_s, ns_s = pl.pallas_call(
        functools.partial(_sample_kernel, seqs=seqs, steps=dec_l),
        grid=(n_blocks + 1,),
        in_specs=[pl.BlockSpec((rows, d_model), lambda i: (new(i), 0)),
                  pl.BlockSpec((rows, d_model), lambda i: (old(i), 0)),
                  pl.BlockSpec((rows, ps.shape[-1]), lambda i: (old(i), 0)),
                  conv_spec, dn_spec, state_spec] + wspecs,
        out_specs=[pl.BlockSpec((rows, d_model), lambda i: (old(i), 0)), conv_spec, dn_spec, state_spec],
        out_shape=[jax.ShapeDtypeStruct((dec_b * dec_l, d_model), F32),
                   jax.ShapeDtypeStruct((1, kw - 1, dec_b, d_conv), F32),
                   jax.ShapeDtypeStruct((1, ks - 1, dec_b, 3 * d_dn), F32),
                   jax.ShapeDtypeStruct((1, dec_b, N_HEADS, HEAD_DIM, HEAD_DIM), F32)],
        scratch_shapes=[slabs(d_conv, rows), slabs(d_conv, rows), slabs(3 * d_dn, rows), slabs(3 * d_dn, rows),
                        keep(d_conv), keep(d_dn), keep(d_dn)],
        compiler_params=pltpu.CompilerParams(dimension_semantics=("arbitrary",), **params),
        name="sample_layer",
    )(xs, xs, ps, tm(state_conv), tm(state_dn_conv), state_dn_S, *wlist)

    return (y_p, y_s.reshape(dec_b, dec_l, d_model), tm(nconv_p), tm(ndn_p), ns_p[None],
            tm(nconv_s), tm(ndn_s), ns_s)
```

```python
import functools

import jax
import jax.numpy as jnp
from jax import lax
from jax.experimental import pallas as pl
from jax.experimental.pallas import tpu as pltpu

EPS = 1e-6
N_HEADS = 4
HEAD_DIM = 128
LANES = 128
SUBLANES = 8
PROMPT_TILE = 256
PROMPT_CHUNK = 64
SAMPLE_SEQS = 16
CONV_ROWS = 32
VMEM_LIMIT_BYTES = 56 * 1024 * 1024
SAMPLE_VMEM_LIMIT_BYTES = 62 * 1024 * 1024

F32 = jnp.float32
BF16 = jnp.bfloat16


def _run(gen):
    try:
        while True:
            next(gen)
    except StopIteration as stop:
        return stop.value


def _interleave(gens, weights):
    n = len(gens)
    done, alive, out = [0] * n, [True] * n, [None] * n
    while any(alive):
        k = min((i for i in range(n) if alive[i]), key=lambda i: (done[i] + 1) / weights[i])
        try:
            next(gens[k])
            done[k] += 1
        except StopIteration as stop:
            out[k], alive[k] = stop.value, False
    return out


def _mm(a, b):
    return jnp.dot(a.astype(BF16), b.astype(BF16), preferred_element_type=F32)


def _mm_w(a, weight):
    return jnp.dot(a, weight, preferred_element_type=F32)


def _mm_nt(a, b):
    return lax.dot_general(a.astype(BF16), b.astype(BF16), (((1,), (1,)), ((), ())),
                           preferred_element_type=F32)


def _mmb(a, b_bf16):
    if a.shape[0] % 16:
        return jnp.dot(a.astype(F32), b_bf16.astype(F32), preferred_element_type=F32)
    return jnp.dot(a.astype(BF16), b_bf16, preferred_element_type=F32)


def _rms(x, g):
    return x * lax.rsqrt(jnp.mean(x * x, axis=-1, keepdims=True) + EPS) * g


def _silu(x):
    return x * jax.nn.sigmoid(x)


def _softplus(x):
    return jnp.maximum(x, 0.0) + jnp.log(1.0 + jnp.exp(-jnp.abs(x)))


def _head(a, h):
    return a[:, h * HEAD_DIM:(h + 1) * HEAD_DIM]


def _lanes(c):
    return slice(c * LANES, (c + 1) * LANES)


class _Projection:
    def __init__(self, x, w):
        self.w, self.d_conv, self.d_dn = w, w["dw_w"].shape[-1], N_HEADS * HEAD_DIM
        self.h = _rms(x, w["g_mix"][...]).astype(BF16)

    def _cols(self, lo, n):
        return jnp.dot(self.h, self.w["w_in"][:, lo:lo + n], preferred_element_type=F32)

    def glu(self):
        ab = self._cols(0, 2 * self.d_conv)
        return ab[:, :self.d_conv] * jax.nn.sigmoid(ab[:, self.d_conv:])

    def c_gate(self):
        return self._cols(2 * self.d_conv, self.d_conv)

    def qkv(self, part):
        return self._cols(3 * self.d_conv + part * self.d_dn, self.d_dn)

    def z(self):
        return self._cols(3 * self.d_conv + 3 * self.d_dn, self.d_dn)

    def beta_g(self):
        w = self.w
        tail = jnp.dot(self.h, w["w_tail"][...], preferred_element_type=F32)
        lane = lax.broadcasted_iota(jnp.int32, tail.shape, 1)
        beta = jax.nn.sigmoid(tail)
        g = -jnp.exp(w["a_log"][...]) * _softplus(tail + w["dt_bias"][...])
        return jnp.where(lane < N_HEADS, beta, jnp.where(lane < 2 * N_HEADS, g, 0.0))


def _conv_branch_tail(c, c_gate, w):
    c = c + w["dw_b"][...]
    cc = c - jnp.mean(c, axis=-1, keepdims=True)
    c = cc * lax.rsqrt(jnp.mean(cc * cc, axis=-1, keepdims=True) + EPS) * w["ln_g"][...] + w["ln_b"][...]
    c = _silu(c)
    return _mm_w(c, w["pw"][...]) * _silu(c_gate)


def _qkv_heads(qkv_c):
    d_dn = N_HEADS * HEAD_DIM
    qs, ks, vs = [], [], []
    for h in range(N_HEADS):
        q = _silu(_head(qkv_c, h))
        k = _silu(_head(qkv_c[:, d_dn:2 * d_dn], h))
        v = _silu(_head(qkv_c[:, 2 * d_dn:], h))
        q = q * (lax.rsqrt(jnp.sum(q * q, axis=-1, keepdims=True) + EPS) * (HEAD_DIM ** -0.5))
        k = k * lax.rsqrt(jnp.sum(k * k, axis=-1, keepdims=True) + EPS)
        qs.append(q); ks.append(k); vs.append(v)
        yield
    return qs, ks, vs


def _chunk_masks(rows, chunk):
    ri = lax.broadcasted_iota(jnp.int32, (rows, rows), 0)
    ci = lax.broadcasted_iota(jnp.int32, (rows, rows), 1)
    same = (ri // chunk) == (ci // chunk)
    incl = same & (ci <= ri)
    strict = same & (ci < ri)
    return incl, strict


def _tri_inverse(a_list, chunk):
    rows = a_list[0].shape[0]
    nc = rows // chunk
    hs = range(len(a_list))
    pi = lax.broadcasted_iota(jnp.int32, (chunk, rows), 0)
    pl_ = lax.broadcasted_iota(jnp.int32, (chunk, rows), 1)
    lane_in, lane_blk = pl_ % chunk, pl_ // chunk
    eye_pan = (lane_in == pi).astype(F32)
    packed = chunk % 16 == 0
    blk_masks = [(lane_blk == c).astype(BF16 if packed else F32) for c in range(nc)]

    def fold(full):
        out = full[0:chunk]
        for c in range(1, nc):
            out = out + full[c * chunk:(c + 1) * chunk]
        return out

    def expand(pan):
        src = pan.astype(BF16) if packed else pan
        return jnp.concatenate([src * blk_masks[c] for c in range(nc)], axis=0).astype(BF16)

    def off_halves(size):
        return ((pi // size) == (lane_in // size)) & ((pi // (size // 2)) != (lane_in // (size // 2)))

    a_pan = [fold(a_list[h]) for h in hs]
    inv_pan = [eye_pan - jnp.where(off_halves(2), a_pan[h], 0.0) for h in hs]
    size = 4
    while size <= chunk:
        inv_full = [expand(inv_pan[h]) for h in hs]
        am_full = [expand(jnp.where(off_halves(size), a_pan[h], 0.0)) for h in hs]
        x_pan = [_mmb(inv_pan[h], am_full[h]) for h in hs]
        yield
        inv_pan = [inv_pan[h] - _mmb(x_pan[h], inv_full[h]) for h in hs]
        yield
        size *= 2
    return [expand(inv_pan[h]) for h in hs]


def _gate_scalars(bg, chunk):
    rows, lanes = bg.shape
    pos = lax.broadcasted_iota(jnp.int32, bg.shape, 0) % chunk
    gc, s = bg, 1
    while s < chunk:
        gc = gc + jnp.where(pos >= s, pltpu.roll(gc, s, 0), 0.0)
        s *= 2
    gtot = jnp.concatenate([jnp.broadcast_to(gc[e - 1:e, :], (chunk, lanes)) for e in range(chunk, rows + 1, chunk)],
                           axis=0)
    return gc, gc.T, gtot


def _chunk_local(qs, ks, vs, bg, chunk):
    incl, strict = _chunk_masks(bg.shape[0], chunk)
    gc, gct, gtot = _gate_scalars(bg, chunk)
    hs = range(N_HEADS)
    beta = [bg[:, h:h + 1] for h in hs]
    gcc = [gc[:, N_HEADS + h:N_HEADS + h + 1] for h in hs]
    gl = [gtot[:, N_HEADS + h:N_HEADS + h + 1] for h in hs]
    yield
    decay = [jnp.where(incl, jnp.exp(jnp.where(incl, gcc[h] - gct[N_HEADS + h:N_HEADS + h + 1, :], 0.0)), 0.0)
             for h in hs]
    kb = [ks[h] * beta[h] for h in hs]
    yield
    a = [jnp.where(strict, _mm_nt(kb[h], ks[h]) * decay[h], 0.0) for h in hs]
    yield
    qk = [_mm_nt(qs[h], ks[h]) * decay[h] for h in hs]
    egc = [jnp.exp(gcc[h]) for h in hs]
    rhs = [jnp.concatenate([vs[h] * beta[h], kb[h] * egc[h]], axis=1).astype(BF16) for h in hs]
    qd = [qs[h] * egc[h] for h in hs]
    kd = [ks[h] * jnp.exp(gl[h] - gcc[h]) for h in hs]
    yield
    tinv = yield from _tri_inverse(a, chunk)
    y = [_mmb(tinv[h], rhs[h]) for h in hs]
    yield
    u = [y[h][:, :HEAD_DIM] for h in hs]
    wk = [y[h][:, HEAD_DIM:] for h in hs]
    return u, wk, qk, qd, kd, gl, gtot


def _finish(x, c_out, o_heads, z, p_emb, w):
    outs = []
    for h in range(N_HEADS):
        o = o_heads[h]
        o = o * lax.rsqrt(jnp.mean(o * o, axis=-1, keepdims=True) + EPS) * w["dn_g"][...]
        outs.append(o * _silu(_head(z, h)))
    mix_in = jnp.concatenate([c_out] + outs, axis=1)
    x = x + _mm_w(mix_in, w["w_out"][...])
    yield
    gate = jax.nn.sigmoid(_mm_w(_rms(x, w["ple_g"][...]), w["ple_gate"][...]))
    yield
    x = x + gate * _mm_w(p_emb, w["ple_proj"][...])
    return _rms(x, w["fin_g"][...])


_WEIGHT_NAMES = ("g_mix", "w_in", "w_tail", "dw_w", "dw_b", "ln_g", "ln_b", "pw", "dn_w", "a_log",
                 "dt_bias", "dn_g", "w_out", "ple_g", "ple_gate", "ple_proj", "fin_g")


def _prompt_kernel(*refs, tile, chunk, nt):
    x_ref, xb_ref, pb_ref = refs[0], refs[1], refs[2]
    nw = len(_WEIGHT_NAMES)
    w = dict(zip(_WEIGHT_NAMES, refs[3:3 + nw]))
    y_ref, nconv_ref, ndn_ref, ns_ref = refs[3 + nw:7 + nw]
    (ubuf, qbuf, cbuf, s_scr, qcv, cring, zring, st_q, st_k, st_v, st_bg,
     s2_u, s2_w, s2_qd, s2_qk, s2_kt, s2_gl) = refs[7 + nw:]
    s = pl.program_id(0)
    slot_new, slot_old = s % 3, (s + 1) % 3
    t_a = s % nt
    t_c = (s + nt - 2) % nt
    kw = w["dw_w"].shape[0]
    ks = w["dn_w"].shape[0]
    hist, qhist = kw - 1, ks - 1
    upad, qpad = ubuf.shape[1] - tile, qbuf.shape[1] - tile
    hs = range(N_HEADS)
    half = HEAD_DIM // 2

    @pl.when(s == 0)
    def _():
        for ref in (s_scr, nconv_ref, ndn_ref):
            ref[...] = jnp.zeros(ref.shape, F32)

    @pl.when(t_a == 0)
    def _():
        ubuf[:, 0:upad, :] = jnp.zeros((ubuf.shape[0], upad, LANES), F32)
        qbuf[:, 0:qpad, :] = jnp.zeros((qbuf.shape[0], qpad, LANES), F32)

    def stage3():
        lane_c = lax.broadcasted_iota(jnp.int32, (chunk, HEAD_DIM), 1)
        lane_t = lax.broadcasted_iota(jnp.int32, (HEAD_DIM, HEAD_DIM), 1)
        hc = lambda h: slice(h * HEAD_DIM, (h + 1) * HEAD_DIM)
        zs = jnp.zeros((HEAD_DIM, HEAD_DIM), BF16)
        zv = jnp.zeros((chunk, HEAD_DIM), BF16)
        state = [jnp.where(t_c == 0, 0.0, s_scr[h]) for h in hs]
        o_rows = [[] for _ in hs]
        for n in range(tile // chunk):
            r = slice(n * chunk, (n + 1) * chunk)
            col, odd = divmod(n * chunk, HEAD_DIM)
            cs = slice(col * HEAD_DIM, (col + 1) * HEAD_DIM)
            pairs = range(0, N_HEADS, 2)
            pair_cols = lambda h0: slice(h0 * HEAD_DIM, (h0 + 2) * HEAD_DIM)
            m1 = {}
            for h0 in pairs:
                h1 = h0 + 1
                lhs1 = jnp.concatenate([s2_w[r, pair_cols(h0)], s2_qd[r, pair_cols(h0)]], axis=0)
                s0, s1 = state[h0].astype(BF16), state[h1].astype(BF16)
                sbd = jnp.concatenate([jnp.concatenate([s0, zs], axis=1), jnp.concatenate([zs, s1], axis=1)], axis=0)
                m1[h0] = _mmb(lhs1, sbd)
            yield
            new_state = list(state)
            decay_row = jnp.exp(s2_gl[n * chunk:n * chunk + 1, :])
            for h0 in pairs:
                h1 = h0 + 1
                v0 = s2_u[r, hc(h0)] - m1[h0][:chunk, :HEAD_DIM]
                v1 = s2_u[r, hc(h1)] - m1[h0][:chunk, HEAD_DIM:]
                vbd = jnp.concatenate([jnp.concatenate([v0.astype(BF16), zv], axis=1),
                                       jnp.concatenate([zv, v1.astype(BF16)], axis=1)], axis=0)
                k0, k1 = s2_kt[hc(h0), cs], s2_kt[hc(h1), cs]
                if odd:
                    kpair = jnp.where(lane_t < half, pltpu.roll(k0, half, 1), k1)
                else:
                    kpair = jnp.where(lane_t < half, k0, pltpu.roll(k1, half, 1))
                qpair = jnp.where(lane_c < half, s2_qk[r, hc(h0)], s2_qk[r, hc(h1)])
                m2 = _mmb(jnp.concatenate([qpair, kpair], axis=0), vbd)
                o_rows[h0].append(m1[h0][chunk:, :HEAD_DIM] + m2[:chunk, :HEAD_DIM])
                o_rows[h1].append(m1[h0][chunk:, HEAD_DIM:] + m2[:chunk, HEAD_DIM:])
                new_state[h0] = state[h0] * decay_row[:, N_HEADS + h0:N_HEADS + h0 + 1] + m2[chunk:, :HEAD_DIM]
                new_state[h1] = state[h1] * decay_row[:, N_HEADS + h1:N_HEADS + h1 + 1] + m2[chunk:, HEAD_DIM:]
            state = new_state
            yield
        for h in hs:
            s_scr[h] = state[h]
        return [jnp.concatenate(o_rows[h], axis=0) for h in hs]

    def stage3_out(o_heads):
        y_ref[0] = yield from _finish(xb_ref[0], cring[slot_old], o_heads, zring[slot_old], pb_ref[0], w)

    def stage2():
        heads_of = lambda ref: [ref[:, h * HEAD_DIM:(h + 1) * HEAD_DIM] for h in hs]
        uu2, wk2, qk2, qd2, kd2, _, gtot2 = yield from _chunk_local(
            heads_of(st_q), heads_of(st_k), heads_of(st_v), st_bg[...], chunk)
        folds = []
        for h in hs:
            f = qk2[h][:, 0:HEAD_DIM]
            for c in range(1, tile // HEAD_DIM):
                f = f + qk2[h][:, c * HEAD_DIM:(c + 1) * HEAD_DIM]
            folds.append(f + pltpu.roll(f, half, 1))
        yield
        return [(s2_u, jnp.concatenate(uu2, axis=1)), (s2_w, jnp.concatenate(wk2, axis=1)),
                (s2_qd, jnp.concatenate(qd2, axis=1)), (s2_qk, jnp.concatenate(folds, axis=1)),
                (s2_kt, jnp.concatenate([kd2[h].T for h in hs], axis=0)), (s2_gl, gtot2)]

    def stage1_project():
        proj = _Projection(x_ref[0], w)
        piece = 2 * LANES
        d_conv, d_dn = ubuf.shape[0] * LANES, N_HEADS * HEAD_DIM
        ab = []
        for lo in range(0, 2 * d_conv, piece):
            ab.append(proj._cols(lo, piece))
            yield
        n = len(ab) // 2
        for i in range(n):
            u = ab[i] * jax.nn.sigmoid(ab[n + i])
            for c in range(piece // LANES):
                ubuf[i * (piece // LANES) + c, upad:upad + tile, :] = u[:, _lanes(c)]
        gate = []
        for lo in range(2 * d_conv, 3 * d_conv, piece):
            gate.append(proj._cols(lo, piece))
            yield
        for i, lo in enumerate(range(3 * d_conv, 3 * d_conv + 3 * d_dn, piece)):
            cols = proj._cols(lo, piece)
            for c in range(piece // LANES):
                qbuf[i * (piece // LANES) + c, qpad:qpad + tile, :] = cols[:, _lanes(c)]
            yield
        for lo in range(0, d_dn, piece):
            zring[slot_new, :, lo:lo + piece] = proj._cols(3 * d_conv + 3 * d_dn + lo, piece)
            yield
        st_bg[...] = proj.beta_g()
        return jnp.concatenate(gate, axis=1)

    def stage1_conv(c_gate):
        starts = [(r0, p) for r0 in range(0, tile, 2 * CONV_ROWS) for p in range(2)]
        groups = [starts[:len(starts) // 2], starts[len(starts) // 2:]]
        for c in range(ubuf.shape[0]):
            for group in groups:
                accs = [jnp.zeros((CONV_ROWS, LANES), F32) for _ in group]
                for j in range(kw):
                    w_row = jnp.broadcast_to(w["dw_w"][j:j + 1, _lanes(c)], (CONV_ROWS, LANES))
                    for i, (r0, p) in enumerate(group):
                        win = ubuf[c, pl.ds(upad - hist + r0 + p + j, CONV_ROWS, stride=2), :]
                        accs[i] = accs[i] + win * w_row
                for (r0, p), acc in zip(group, accs):
                    cbuf[c, pl.ds(r0 + p, CONV_ROWS, stride=2), :] = acc
                yield
        c_raw = jnp.concatenate([cbuf[c] for c in range(cbuf.shape[0])], axis=1)
        cring[slot_new] = _conv_branch_tail(c_raw, c_gate, w)
        yield
        half_tile = tile // 2
        for c in range(qbuf.shape[0]):
            for p in range(2):
                acc = jnp.zeros((half_tile, LANES), F32)
                for j in range(ks):
                    win = qbuf[c, pl.ds(qpad - qhist + p + j, half_tile, stride=2), :]
                    acc = acc + win * w["dn_w"][j:j + 1, _lanes(c)]
                qcv[c, pl.ds(p, half_tile, stride=2), :] = acc
            if c % 3 == 2:
                yield
        qkv_c = jnp.concatenate([qcv[c] for c in range(qcv.shape[0])], axis=1)
        qa, ka, va = yield from _qkv_heads(qkv_c)
        st_q[...] = jnp.concatenate(qa, axis=1)
        st_k[...] = jnp.concatenate(ka, axis=1)
        st_v[...] = jnp.concatenate(va, axis=1)
        for c in range(ubuf.shape[0]):
            ubuf[c, 0:upad, :] = ubuf[c, tile:tile + upad, :]
        for c in range(qbuf.shape[0]):
            qbuf[c, 0:qpad, :] = qbuf[c, tile:tile + qpad, :]

    def step(first, second, third):
        chains = [(stage3, 8)] * third + [(stage2, 16)] * second + [(stage1_project, 15)] * first
        out = _interleave([g() for g, _ in chains], [n for _, n in chains])
        c_gate = out.pop() if first else None
        delta_local = out.pop() if second else []
        for ref, value in delta_local:
            ref[...] = value
        tails = ([(stage3_out(out[0]), 3)] if third else []) + ([(stage1_conv(c_gate), 17)] if first else [])
        _interleave([g for g, _ in tails], [n for _, n in tails])

    tiles = pl.num_programs(0) - 2
    pl.when(s == 0)(lambda: step(True, False, False))
    pl.when(s == 1)(lambda: step(True, True, False))
    pl.when((s >= 2) & (s < tiles))(lambda: step(True, True, True))
    pl.when(s == tiles)(lambda: step(False, True, True))
    pl.when(s == tiles + 1)(lambda: step(False, False, True))

    @pl.when((t_a == nt - 1) & (s < pl.num_programs(0) - 2))
    def _():
        n_seq = nconv_ref.shape[1]
        mine = jax.lax.broadcasted_iota(jnp.int32, (n_seq, LANES), 0) == s // nt
        for out_ref, buf, first, n in ((nconv_ref, ubuf, upad + tile - hist, hist),
                                       (ndn_ref, qbuf, qpad + tile - qhist, qhist)):
            for c in range(buf.shape[0]):
                for i in range(n):
                    new = jnp.broadcast_to(buf[c, first + i:first + i + 1, :], (n_seq, LANES))
                    out_ref[i, :, _lanes(c)] = jnp.where(mine, new, out_ref[i, :, _lanes(c)])

    @pl.when((t_c == nt - 1) & (s > 1))
    def _():
        ns_ref[0] = s_scr[...]


def _sample_kernel(*refs, seqs, steps):
    x_ref, p_ref, sc_ref, sdn_ref, s_ref = refs[:5]
    nw = len(_WEIGHT_NAMES)
    w = dict(zip(_WEIGHT_NAMES, refs[5:5 + nw]))
    y_ref, nconv_ref, ndn_ref, ns_ref = refs[5 + nw:9 + nw]
    ustage, cstage, qstage, qcs, c_keep, z_keep, o_keep, u_keep, w_keep, qd_keep, kd_keep, qk_keep, g_keep = refs[9 + nw:]
    rows = seqs * steps
    sub = s_ref.shape[0]
    sub_rows = sub * steps
    half = pl.program_id(1)
    kw = w["dw_w"].shape[0]
    ks = w["dn_w"].shape[0]
    hist, qhist = sc_ref.shape[0], sdn_ref.shape[0]
    by_time = lambda ref, c, t: ref[c, pl.ds(t, seqs, stride=steps), :]

    x = x_ref[...]
    proj = _Projection(x, w)
    hs = range(N_HEADS)

    def conv_branch():
        u = proj.glu()
        n_uc = ustage.shape[0]
        for c in range(n_uc):
            ustage[c] = u[:, _lanes(c)]
        u_tm = [[by_time(ustage, c, t) for c in range(n_uc)] for t in range(steps)]
        yield
        for t in range(steps):
            for c in range(n_uc):
                acc = jnp.zeros((seqs, LANES), F32)
                for j in range(kw):
                    i = t + j
                    src = sc_ref[i, :, _lanes(c)] if i < hist else u_tm[i - hist][c]
                    acc = acc + src * w["dw_w"][j:j + 1, _lanes(c)]
                cstage[c, pl.ds(t, seqs, stride=steps), :] = acc
            yield
        for i in range(hist):
            nconv_ref[i] = sc_ref[i + steps] if i + steps < hist else jnp.concatenate(u_tm[i + steps - hist], axis=1)
        c_gate = proj.c_gate()
        yield
        return _conv_branch_tail(jnp.concatenate([cstage[c] for c in range(n_uc)], axis=1), c_gate, w)

    def delta_branch():
        n_qc = qstage.shape[0]
        for part in range(3):
            qkv_p = proj.qkv(part)
            for c in range(N_HEADS):
                qstage[part * N_HEADS + c] = qkv_p[:, _lanes(c)]
            yield
        q_tm = [[by_time(qstage, c, t) for c in range(n_qc)] for t in range(steps)]
        for t in range(steps):
            for c in range(n_qc):
                acc = jnp.zeros((seqs, LANES), F32)
                for j in range(ks):
                    i = t + j
                    src = sdn_ref[i, :, _lanes(c)] if i < qhist else q_tm[i - qhist][c]
                    acc = acc + src * w["dn_w"][j:j + 1, _lanes(c)]
                qcs[c, pl.ds(t, seqs, stride=steps), :] = acc
            if t % 2:
                yield
        for i in range(qhist):
            ndn_ref[i] = jnp.concatenate(q_tm[steps - qhist + i], axis=1)
        qs, kss, vs = yield from _qkv_heads(jnp.concatenate([qcs[c] for c in range(n_qc)], axis=1))
        bg = proj.beta_g()
        yield
        uu, wk, qk, qd, kd, _, gtot = yield from _chunk_local(qs, kss, vs, bg, steps)
        lo, hi = slice(0, sub_rows), slice(sub_rows, rows)
        u_keep[...] = jnp.concatenate([a[hi] for a in uu], axis=1)
        w_keep[...] = jnp.concatenate([a[hi] for a in wk], axis=1)
        qd_keep[...] = jnp.concatenate([a[hi] for a in qd], axis=1)
        kd_keep[...] = jnp.concatenate([a[hi] for a in kd], axis=1)
        qk_keep[...] = jnp.concatenate([a[hi, hi] for a in qk], axis=1)
        g_keep[...] = gtot[hi]
        o_first = yield from state_part([a[lo] for a in uu], [a[lo] for a in wk], [a[lo, lo] for a in qk],
                                        [a[lo] for a in qd], [a[lo] for a in kd], gtot[lo])
        return o_first

    def state_part(uu, wk, qk, qd, kd, gtot):
        gl = [gtot[:, N_HEADS + h:N_HEADS + h + 1] for h in hs]
        seq_of_col = lax.broadcasted_iota(jnp.int32, (sub, 1, sub_rows), 2) // steps
        seq_id = lax.broadcasted_iota(jnp.int32, (sub, 1, sub_rows), 0)
        col_mask = (seq_of_col == seq_id).astype(F32)
        s_old = [s_ref[:, h] for h in hs]
        m1 = []
        for h in hs:
            lhs = jnp.concatenate([wk[h].reshape(sub, steps, HEAD_DIM), qd[h].reshape(sub, steps, HEAD_DIM)], axis=1)
            m1.append(lax.dot_general(lhs.astype(BF16), s_old[h].astype(BF16), (((2,), (1,)), ((0,), (0,))),
                                      preferred_element_type=F32))
        yield
        v_new = [uu[h] - m1[h][:, :steps, :].reshape(sub_rows, HEAD_DIM) for h in hs]
        o_heads = [m1[h][:, steps:, :].reshape(sub_rows, HEAD_DIM) + _mm(qk[h], v_new[h]) for h in hs]
        yield
        for h in hs:
            kd_rows = (kd[h].T[None, :, :] * col_mask).reshape(sub * HEAD_DIM, sub_rows)
            ds = _mm(kd_rows, v_new[h]).reshape(sub, HEAD_DIM, HEAD_DIM)
            gl_seq = jnp.exp(gl[h].reshape(sub, steps, 1)[:, 0:1, :])
            ns_ref[:, h] = s_old[h] * gl_seq + ds
        return o_heads

    @pl.when(half == 0)
    def _():
        c_out, o_first = _interleave([conv_branch(), delta_branch()], [10, 24])
        c_keep[...] = c_out
        z_keep[...] = proj.z()
        o_keep[...] = jnp.concatenate(o_first, axis=1)

    @pl.when(half == 1)
    def _():
        heads = lambda ref: [_head(ref[...], h) for h in hs]
        qk = [qk_keep[:, h * sub_rows:(h + 1) * sub_rows] for h in hs]
        o_second = _run(state_part(heads(u_keep), heads(w_keep), qk, heads(qd_keep), heads(kd_keep), g_keep[...]))
        o_heads = [jnp.concatenate([_head(o_keep[...], h), o_second[h]], axis=0) for h in hs]
        y_ref[...] = _run(_finish(x, c_keep[...], o_heads, z_keep[...], p_ref[...], w))


def _full_spec(a):
    nd = a.ndim
    return pl.BlockSpec(a.shape, lambda *_: (0,) * nd, pipeline_mode=pl.Buffered(1))


def kernel(x_prompt, x_sample, state_conv, state_dn_conv, state_dn_S, p_prompt, p_sample, norm_mix_g, w_in, conv_dw_w, conv_dw_b, conv_ln_g, conv_ln_b, conv_pw_w, dn_conv_w, dn_a_log, dn_dt_bias, dn_norm_g, w_out, ple_norm_g, ple_gate_w, ple_proj_w, final_norm_g):
    depth = w_in.shape[0]
    assert depth == 1, "single trunk layer"
    bsz, seqlen, d_model = x_prompt.shape
    dec_b, dec_l, _ = x_sample.shape
    d_conv = conv_dw_w.shape[-1]
    d_dn = N_HEADS * HEAD_DIM
    kw, ks = conv_dw_w.shape[1], dn_conv_w.shape[1]
    d_main = 3 * d_conv + 4 * d_dn
    assert w_in.shape[-1] == d_main + 2 * N_HEADS
    assert dn_conv_w.shape[-1] == 3 * d_dn and dn_norm_g.shape[-1] == HEAD_DIM
    tile, chunk = min(PROMPT_TILE, seqlen), min(PROMPT_CHUNK, seqlen)
    assert seqlen % tile == 0 and tile % chunk == 0 and tile % (2 * CONV_ROWS) == 0
    assert dec_b % SAMPLE_SEQS == 0 and SAMPLE_SEQS % SUBLANES == 0 and dec_l == SUBLANES
    assert 2 * chunk == HEAD_DIM and tile % HEAD_DIM == 0

    row = lambda v: v.reshape(1, -1)
    head_lanes = lambda v: jnp.pad(v, (N_HEADS, LANES - 2 * N_HEADS)).reshape(1, LANES)
    weights = dict(
        g_mix=row(norm_mix_g[0]),
        w_in=w_in[0].astype(BF16),
        w_tail=jnp.pad(w_in[0, :, d_main:].astype(BF16), ((0, 0), (0, LANES - 2 * N_HEADS))),
        dw_w=conv_dw_w[0], dw_b=row(conv_dw_b[0]), ln_g=row(conv_ln_g[0]), ln_b=row(conv_ln_b[0]),
        pw=conv_pw_w[0],
        dn_w=dn_conv_w[0],
        a_log=head_lanes(dn_a_log[0]),
        dt_bias=head_lanes(dn_dt_bias[0]),
        dn_g=row(dn_norm_g[0]),
        w_out=w_out[0],
        ple_g=row(ple_norm_g[0]), ple_gate=ple_gate_w[0], ple_proj=ple_proj_w[0],
        fin_g=row(final_norm_g),
    )
    wlist = [weights[n] for n in _WEIGHT_NAMES]
    wspecs = [_full_spec(a) for a in wlist]
    params = dict(vmem_limit_bytes=VMEM_LIMIT_BYTES)

    nt = seqlen // tile
    upad = pl.cdiv(kw - 1, SUBLANES) * SUBLANES
    qpad = pl.cdiv(ks - 1, SUBLANES) * SUBLANES
    assert nt > 1
    n_tiles = bsz * nt
    front = lambda s: (jnp.minimum(s, n_tiles - 1) // nt, jnp.minimum(s, n_tiles - 1) % nt)
    back = lambda s: (jnp.maximum(s - 2, 0) // nt, jnp.maximum(s - 2, 0) % nt)
    stage = lambda cols: pltpu.VMEM((tile, cols), F32)
    slabs = lambda cols, rows: pltpu.VMEM((cols // LANES, rows, LANES), F32)
    y_p, nconv_p, ndn_p, ns_p = pl.pallas_call(
        functools.partial(_prompt_kernel, tile=tile, chunk=chunk, nt=nt),
        grid=(n_tiles + 2,),
        in_specs=[pl.BlockSpec((1, tile, d_model), lambda s: (*front(s), 0)),
                  pl.BlockSpec((1, tile, d_model), lambda s: (*back(s), 0)),
                  pl.BlockSpec((1, tile, p_prompt.shape[-1]), lambda s: (*back(s), 0))] + wspecs,
        out_specs=[pl.BlockSpec((1, tile, d_model), lambda s: (*back(s), 0)),
                   pl.BlockSpec((None, kw - 1, bsz, d_conv), lambda s: (0, 0, 0, 0)),
                   pl.BlockSpec((None, ks - 1, bsz, 3 * d_dn), lambda s: (0, 0, 0, 0)),
                   pl.BlockSpec((1, N_HEADS, HEAD_DIM, HEAD_DIM), lambda s: (back(s)[0], 0, 0, 0))],
        out_shape=[jax.ShapeDtypeStruct((bsz, seqlen, d_model), F32),
                   jax.ShapeDtypeStruct((1, kw - 1, bsz, d_conv), F32),
                   jax.ShapeDtypeStruct((1, ks - 1, bsz, 3 * d_dn), F32),
                   jax.ShapeDtypeStruct((bsz, N_HEADS, HEAD_DIM, HEAD_DIM), F32)],
        scratch_shapes=[slabs(d_conv, upad + tile),
                        slabs(3 * d_dn, qpad + tile),
                        slabs(d_conv, tile),
                        pltpu.VMEM((N_HEADS, HEAD_DIM, HEAD_DIM), F32),
                        slabs(3 * d_dn, tile),
                        pltpu.VMEM((3, tile, d_conv), F32),
                        pltpu.VMEM((3, tile, d_dn), F32),
                        stage(d_dn), stage(d_dn), stage(d_dn), stage(LANES),
                        stage(d_dn), stage(d_dn), stage(d_dn), stage(d_dn),
                        pltpu.VMEM((d_dn, tile), F32), stage(LANES)],
        compiler_params=pltpu.CompilerParams(dimension_semantics=("arbitrary",), **params),
        name="prompt_layer",
    )(x_prompt, x_prompt, p_prompt[0], *wlist)

    sub = SAMPLE_SEQS
    seqs = 2 * sub
    assert dec_b % seqs == 0
    rows = seqs * dec_l
    xs = x_sample.reshape(dec_b * dec_l, d_model)
    ps = p_sample[0].reshape(dec_b * dec_l, -1)
    tm = lambda a: jnp.transpose(a, (0, 2, 1, 3))
    conv_spec = pl.BlockSpec((None, kw - 1, seqs, d_conv), lambda i, j: (0, 0, i, 0))
    dn_spec = pl.BlockSpec((None, ks - 1, seqs, 3 * d_dn), lambda i, j: (0, 0, i, 0))
    state_spec = pl.BlockSpec((None, sub, N_HEADS, HEAD_DIM, HEAD_DIM), lambda i, j: (0, 2 * i + j, 0, 0, 0))
    keep = lambda n, cols: pltpu.VMEM((n, cols), F32)
    sub_rows = sub * dec_l
    y_s, nconv_s, ndn_s, ns_s = pl.pallas_call(
        functools.partial(_sample_kernel, seqs=seqs, steps=dec_l),
        grid=(dec_b // seqs, 2),
        in_specs=[pl.BlockSpec((rows, d_model), lambda i, j: (i, 0)),
                  pl.BlockSpec((rows, ps.shape[-1]), lambda i, j: (i, 0)),
                  conv_spec, dn_spec, state_spec] + wspecs,
        out_specs=[pl.BlockSpec((rows, d_model), lambda i, j: (i, 0)), conv_spec, dn_spec, state_spec],
        out_shape=[jax.ShapeDtypeStruct((dec_b * dec_l, d_model), F32),
                   jax.ShapeDtypeStruct((1, kw - 1, dec_b, d_conv), F32),
                   jax.ShapeDtypeStruct((1, ks - 1, dec_b, 3 * d_dn), F32),
                   jax.ShapeDtypeStruct((1, dec_b, N_HEADS, HEAD_DIM, HEAD_DIM), F32)],
        scratch_shapes=[slabs(d_conv, rows), slabs(d_conv, rows), slabs(3 * d_dn, rows), slabs(3 * d_dn, rows),
                        keep(rows, d_conv), keep(rows, d_dn), keep(sub_rows, d_dn),
                        keep(sub_rows, d_dn), keep(sub_rows, d_dn), keep(sub_rows, d_dn), keep(sub_rows, d_dn),
                        keep(sub_rows, N_HEADS * sub_rows), keep(sub_rows, LANES)],
        compiler_params=pltpu.CompilerParams(dimension_semantics=("arbitrary", "arbitrary"),
                                             vmem_limit_bytes=SAMPLE_VMEM_LIMIT_BYTES),
        name="sample_layer",
    )(xs, ps, tm(state_conv), tm(state_dn_conv), state_dn_S, *wlist)

    return (y_p, y_s.reshape(dec_b, dec_l, d_model), tm(nconv_p), tm(ndn_p), ns_p[None],
            tm(nconv_s), tm(ndn_s), ns_s)
```
